```python
import jax, jax.numpy as jnp
from jax import lax
import numpy as np

D_MODEL = 2048
BATCH = 2
SEQ = 8192
DEPTH = 1
DEC_BATCH = 16
DEC_SEQ = 32
PAST_LEN = 2048

CHUNK = 64
CONV_DIM = D_MODEL
CONV_WIDTH = 31
SSM_INNER = 2 * D_MODEL
SSM_HEAD_DIM = 64
SSM_HEADS = SSM_INNER // SSM_HEAD_DIM
SSM_GROUPS = 8
HEADS_PER_GROUP = SSM_HEADS // SSM_GROUPS
SSM_STATE = 128
SSM_CONV_WIDTH = 4
SSM_XBC_DIM = SSM_INNER + 2 * SSM_GROUPS * SSM_STATE
D_FF = 4 * D_MODEL
N_BRANCH = 2
N_IN = 2 * CONV_DIM + SSM_INNER + SSM_XBC_DIM + SSM_HEADS + N_BRANCH * D_MODEL
EPS = 1e-5

kernel_name = "gated_conformer_conv_mamba2_streaming_step"


def rmsnorm(x, w):
    xf = x.astype(jnp.float32)
    y = xf * lax.rsqrt(jnp.mean(xf * xf, axis=-1, keepdims=True) + EPS)
    return (y * w.astype(jnp.float32)).astype(x.dtype)


def layernorm(x, w, b):
    xf = x.astype(jnp.float32)
    mu = jnp.mean(xf, axis=-1, keepdims=True)
    xc = xf - mu
    var = jnp.mean(xc * xc, axis=-1, keepdims=True)
    y = xc * lax.rsqrt(var + EPS) * w.astype(jnp.float32) + b.astype(jnp.float32)
    return y.astype(x.dtype)


def group_rmsnorm(x, w, groups):
    shp = x.shape
    xf = x.astype(jnp.float32).reshape(*shp[:-1], groups, shp[-1] // groups)
    y = xf * lax.rsqrt(jnp.mean(xf * xf, axis=-1, keepdims=True) + EPS)
    return (y.reshape(shp) * w.astype(jnp.float32)).astype(x.dtype)


def causal_dwconv(x, cache, w, b):
    k = w.shape[0]
    xp = jnp.concatenate([cache.astype(x.dtype), x], axis=1)
    y = lax.conv_general_dilated(xp, w[:, None, :].astype(x.dtype), (1,), 'VALID',
                                 dimension_numbers=('NWC', 'WIO', 'NWC'),
                                 feature_group_count=x.shape[-1])
    return y + b.astype(x.dtype), xp[:, xp.shape[1] - (k - 1):]


def ssd_block(h0, x, dt, A, Bm, Cm):
    b, l, _, p = x.shape
    g, r, n = SSM_GROUPS, HEADS_PER_GROUP, SSM_STATE
    acs = jnp.cumsum(dt * A, axis=1)
    seg = acs[:, :, None, :] - acs[:, None, :, :]
    causal = jnp.tril(jnp.ones((l, l), dtype=bool))[None, :, :, None]
    decay = jnp.exp(jnp.where(causal, seg, -jnp.inf)).reshape(b, l, l, g, r)
    cb = jnp.einsum('blgn,bsgn->blsg', Cm, Bm)
    xr = x.reshape(b, l, g, r, p)
    dtr = dt.reshape(b, l, g, r)
    mix = cb[..., None] * decay * dtr[:, None]
    y_diag = jnp.einsum('blsgr,bsgrp->blgrp', mix, xr)
    h0r = h0.reshape(b, g, r, p, n)
    y_off = jnp.einsum('blgn,bgrpn->blgrp', Cm, h0r) * jnp.exp(acs).reshape(b, l, g, r)[..., None]
    w_end = (jnp.exp(acs[:, -1:] - acs) * dt).reshape(b, l, g, r)
    h_new = (jnp.exp(acs[:, -1]).reshape(b, g, r)[..., None, None] * h0r
             + jnp.einsum('blgn,blgr,blgrp->bgrpn', Bm, w_end, xr))
    return (y_diag + y_off).reshape(b, l, SSM_HEADS, p), h_new.reshape(b, SSM_HEADS, p, n)


def ssd_scan(x, dt, A, Bm, Cm, h0):
    b, L = x.shape[:2]
    if L <= CHUNK:
        return ssd_block(h0, x, dt, A, Bm, Cm)
    nc = L // CHUNK

    def to_blocks(t):
        return t.reshape(b, nc, CHUNK, *t.shape[2:]).swapaxes(0, 1)

    def step(h, inp):
        xc, dtc, bc, cc = inp
        y, h = ssd_block(h, xc, dtc, A, bc, cc)
        return h, y

    h_last, ys = lax.scan(step, h0, (to_blocks(x), to_blocks(dt), to_blocks(Bm), to_blocks(Cm)))
    return ys.swapaxes(0, 1).reshape(b, L, SSM_HEADS, SSM_HEAD_DIM), h_last


def token_mixers(h, conv_cache, xbc_cache, ssm_h0, w_in, conv_dw_w, conv_dw_b, conv_ln_w, conv_ln_b,
                 w_conv_out, ssm_conv_w, ssm_conv_b, dt_bias, a_log, d_skip, ssm_norm_w, w_ssm_out, w_out):
    b, L, _ = h.shape
    u = h @ w_in
    idx = np.cumsum([2 * CONV_DIM, SSM_INNER, SSM_XBC_DIM, SSM_HEADS])
    conv_in, z, xbc, dt_raw, gates = jnp.split(u, idx, axis=-1)
    a_val, a_gate = jnp.split(conv_in, 2, axis=-1)
    a = a_val * jax.nn.sigmoid(a_gate)
    a, new_conv_cache = causal_dwconv(a, conv_cache, conv_dw_w, conv_dw_b)
    a = jax.nn.silu(layernorm(a, conv_ln_w, conv_ln_b))
    y_a = a @ w_conv_out
    xbc_c, new_xbc_cache = causal_dwconv(xbc, xbc_cache, ssm_conv_w, ssm_conv_b)
    xbc_c = jax.nn.silu(xbc_c)
    xs, bm, cm = jnp.split(xbc_c, [SSM_INNER, SSM_INNER + SSM_GROUPS * SSM_STATE], axis=-1)
    f32 = jnp.float32
    dt = jax.nn.softplus(dt_raw.astype(f32) + dt_bias.astype(f32))
    A = -jnp.exp(a_log.astype(f32))
    xh = xs.astype(f32).reshape(b, L, SSM_HEADS, SSM_HEAD_DIM)
    y, h_new = ssd_scan(xh, dt, A,
                        bm.astype(f32).reshape(b, L, SSM_GROUPS, SSM_STATE),
                        cm.astype(f32).reshape(b, L, SSM_GROUPS, SSM_STATE),
                        ssm_h0.astype(f32))
    y = (y + d_skip.astype(f32)[:, None] * xh).reshape(b, L, SSM_INNER).astype(h.dtype)
    y = group_rmsnorm(y * jax.nn.silu(z), ssm_norm_w, SSM_GROUPS)
    y_b = y @ w_ssm_out
    g_a, g_b = jnp.split(gates, 2, axis=-1)
    m = jax.nn.sigmoid(g_a) * y_a + jax.nn.sigmoid(g_b) * y_b
    return m @ w_out, new_conv_cache, new_xbc_cache, h_new.astype(h.dtype)


def trunk(x, cache_conv, cache_ssm_conv, state_ssm, params, final_norm_w):
    (mix_norm_w, w_in, conv_dw_w, conv_dw_b, conv_ln_w, conv_ln_b, w_conv_out, ssm_conv_w, ssm_conv_b,
     dt_bias, a_log, d_skip, ssm_norm_w, w_ssm_out, w_out, mlp_norm_w, w_up, w_down) = params
    conv_states, xbc_states, ssm_states = [], [], []
    for i in range(DEPTH):
        h = rmsnorm(x, mix_norm_w[i])
        mix, c_new, xbc_new, s_new = token_mixers(
            h, cache_conv[i], cache_ssm_conv[i], state_ssm[i], w_in[i], conv_dw_w[i], conv_dw_b[i],
            conv_ln_w[i], conv_ln_b[i], w_conv_out[i], ssm_conv_w[i], ssm_conv_b[i], dt_bias[i], a_log[i],
            d_skip[i], ssm_norm_w[i], w_ssm_out[i], w_out[i])
        x = x + mix
        h2 = rmsnorm(x, mlp_norm_w[i])
        x = x + jnp.square(jax.nn.relu(h2 @ w_up[i])) @ w_down[i]
        conv_states.append(c_new)
        xbc_states.append(xbc_new)
        ssm_states.append(s_new)
    return (rmsnorm(x, final_norm_w), jnp.stack(conv_states), jnp.stack(xbc_states), jnp.stack(ssm_states))


def setup_inputs(seed: int = 0) -> dict:
    key = jax.random.key(seed)
    ks = jax.random.split(key, 26)
    nrm = jax.random.normal
    f = jnp.float32

    def w(k, shape, fan_in):
        return nrm(k, shape, f) * (fan_in ** -0.5)

    dt0 = jnp.exp(jax.random.uniform(ks[10], (DEPTH, SSM_HEADS), f, np.log(1e-3), np.log(1e-1)))
    return {
        "x_prompt": nrm(ks[0], (BATCH, SEQ, D_MODEL), f),
        "x_sample": nrm(ks[1], (DEC_BATCH, DEC_SEQ, D_MODEL), f),
        "cache_conv": 0.5 * nrm(ks[2], (DEPTH, DEC_BATCH, CONV_WIDTH - 1, CONV_DIM), f),
        "cache_ssm_conv": nrm(ks[3], (DEPTH, DEC_BATCH, SSM_CONV_WIDTH - 1, SSM_XBC_DIM), f),
        "state_ssm": 0.5 * nrm(ks[4], (DEPTH, DEC_BATCH, SSM_HEADS, SSM_HEAD_DIM, SSM_STATE), f),
        "mix_norm_w": 1.0 + 0.05 * nrm(ks[5], (DEPTH, D_MODEL), f),
        "w_in": w(ks[6], (DEPTH, D_MODEL, N_IN), D_MODEL),
        "conv_dw_w": w(ks[7], (DEPTH, CONV_WIDTH, CONV_DIM), CONV_WIDTH),
        "conv_dw_b": 0.02 * nrm(ks[8], (DEPTH, CONV_DIM), f),
        "conv_ln_w": 1.0 + 0.05 * nrm(ks[9], (DEPTH, CONV_DIM), f),
        "conv_ln_b": 0.02 * nrm(ks[11], (DEPTH, CONV_DIM), f),
        "w_conv_out": w(ks[12], (DEPTH, CONV_DIM, D_MODEL), CONV_DIM),
        "ssm_conv_w": w(ks[13], (DEPTH, SSM_CONV_WIDTH, SSM_XBC_DIM), SSM_CONV_WIDTH),
        "ssm_conv_b": 0.02 * nrm(ks[14], (DEPTH, SSM_XBC_DIM), f),
        "dt_bias": dt0 + jnp.log(-jnp.expm1(-dt0)),
        "a_log": jnp.log(jax.random.uniform(ks[15], (DEPTH, SSM_HEADS), f, 1.0, 16.0)),
        "d_skip": 1.0 + 0.1 * nrm(ks[16], (DEPTH, SSM_HEADS), f),
        "ssm_norm_w": 1.0 + 0.05 * nrm(ks[17], (DEPTH, SSM_INNER), f),
        "w_ssm_out": w(ks[18], (DEPTH, SSM_INNER, D_MODEL), SSM_INNER),
        "w_out": w(ks[19], (DEPTH, D_MODEL, D_MODEL), D_MODEL),
        "mlp_norm_w": 1.0 + 0.05 * nrm(ks[20], (DEPTH, D_MODEL), f),
        "w_up": w(ks[21], (DEPTH, D_MODEL, D_FF), D_MODEL),
        "w_down": w(ks[22], (DEPTH, D_FF, D_MODEL), D_FF),
        "final_norm_w": 1.0 + 0.05 * nrm(ks[23], (D_MODEL,), f),
    }


def reference(x_prompt, x_sample, cache_conv, cache_ssm_conv, state_ssm, mix_norm_w, w_in, conv_dw_w,
              conv_dw_b, conv_ln_w, conv_ln_b, w_conv_out, ssm_conv_w, ssm_conv_b, dt_bias, a_log, d_skip,
              ssm_norm_w, w_ssm_out, w_out, mlp_norm_w, w_up, w_down, final_norm_w):
    params = (mix_norm_w, w_in, conv_dw_w, conv_dw_b, conv_ln_w, conv_ln_b, w_conv_out, ssm_conv_w,
              ssm_conv_b, dt_bias, a_log, d_skip, ssm_norm_w, w_ssm_out, w_out, mlp_norm_w, w_up, w_down)
    bp = x_prompt.shape[0]
    dtp = x_prompt.dtype
    y_prompt, conv_p, xbc_p, ssm_p = trunk(
        x_prompt,
        jnp.zeros((DEPTH, bp, CONV_WIDTH - 1, CONV_DIM), dtp),
        jnp.zeros((DEPTH, bp, SSM_CONV_WIDTH - 1, SSM_XBC_DIM), dtp),
        jnp.zeros((DEPTH, bp, SSM_HEADS, SSM_HEAD_DIM, SSM_STATE), dtp),
        params, final_norm_w)
    y_sample, conv_s, xbc_s, ssm_s = trunk(x_sample, cache_conv, cache_ssm_conv, state_ssm, params, final_norm_w)
    return (y_prompt, y_sample, conv_p, xbc_p, ssm_p, conv_s, xbc_s, ssm_s)
```

```python
import functools

import jax
import jax.numpy as jnp
from jax import lax
from jax.experimental import pallas as pl
from jax.experimental.pallas import tpu as pltpu

F32 = jnp.float32
BF16 = jnp.bfloat16

D_MODEL = 2048
CONV_DIM = D_MODEL
CONV_WIDTH = 31
SSM_INNER = 2 * D_MODEL
SSM_HEAD_DIM = 64
SSM_HEADS = SSM_INNER // SSM_HEAD_DIM
SSM_GROUPS = 8
HEADS_PER_GROUP = SSM_HEADS // SSM_GROUPS
SSM_STATE = 128
SSM_CONV_WIDTH = 4
SSM_XBC_DIM = SSM_INNER + 2 * SSM_GROUPS * SSM_STATE
D_FF = 4 * D_MODEL
EPS = 1e-5

V7X_LANES = 128
V7X_VMEM_BYTES = 64 * 1024 * 1024
V7X_VMEM_REQUEST_CAP = 56 * 1024 * 1024
COMPILER_TEMP_BYTES = 12 * 1024 * 1024

SSD_CHUNK = 128
CONV_HALO = 32
SSM_HALO = 16
ROW_TILE = 512


def _vmem_limit(block_bytes, scratch_bytes=0):
    need = 2 * sum(block_bytes) + scratch_bytes + COMPILER_TEMP_BYTES
    return int(min(need, V7X_VMEM_REQUEST_CAP))


def _nbytes(shape, dtype):
    n = 1
    for s in shape:
        n *= s
    return n * jnp.dtype(dtype).itemsize


def _sigmoid(x):
    return 1.0 / (1.0 + jnp.exp(-x))


def _softplus(x):
    return jnp.maximum(x, 0.0) + jnp.log1p(jnp.exp(-jnp.abs(x)))


def _rms(x, w):
    ms = jnp.mean(x * x, axis=-1, keepdims=True)
    return x * lax.rsqrt(ms + EPS) * w


def _norm_kernel(xp_ref, xs_ref, w_ref, o_ref, *, n_prompt_tiles):
    i = pl.program_id(0)

    @pl.when(i < n_prompt_tiles)
    def _():
        o_ref[...] = _rms(xp_ref[...], w_ref[...]).astype(o_ref.dtype)

    @pl.when(i >= n_prompt_tiles)
    def _():
        o_ref[...] = _rms(xs_ref[...], w_ref[...]).astype(o_ref.dtype)


def _norm_call(xp, xs, w):
    tp, d = xp.shape
    ts = xs.shape[0]
    assert ts == ROW_TILE and tp % ROW_TILE == 0
    npt = tp // ROW_TILE
    blocks = [_nbytes((ROW_TILE, d), F32)] * 2 + [_nbytes((ROW_TILE, d), BF16)]
    return pl.pallas_call(
        functools.partial(_norm_kernel, n_prompt_tiles=npt),
        grid=(npt + 1,),
        in_specs=[
            pl.BlockSpec((ROW_TILE, d), lambda i: (jnp.minimum(i, npt - 1), 0)),
            pl.BlockSpec((ROW_TILE, d), lambda i: (0, 0)),
            pl.BlockSpec((1, d), lambda i: (0, 0)),
        ],
        out_specs=pl.BlockSpec((ROW_TILE, d), lambda i: (i, 0)),
        out_shape=jax.ShapeDtypeStruct((tp + ts, d), BF16),
        compiler_params=pltpu.CompilerParams(
            dimension_semantics=("arbitrary",), vmem_limit_bytes=_vmem_limit(blocks)),
        name="mix_rmsnorm",
    )(xp, xs, w)


def _mm_kernel(*refs, n_w, n_ex, n_p, epilogue):
    lhs_ref = refs[0]
    w_refs = refs[1:1 + n_w]
    ex_refs = refs[1 + n_w:1 + n_w + n_ex]
    p_refs = refs[1 + n_w + n_ex:1 + n_w + n_ex + n_p]
    o_ref = refs[-1]
    lhs = lhs_ref[...]
    accs = [jnp.dot(lhs, w[...], preferred_element_type=F32) for w in w_refs]
    out = epilogue(accs, [e[...].astype(F32) for e in ex_refs], [p[...] for p in p_refs])
    o_ref[...] = out.astype(o_ref.dtype)


def _mm_call(lhs, weights, epilogue, out_dtype, *, tm, tn, extras=(), params=(), name):
    t, k = lhs.shape
    n = weights[0].shape[1]
    assert t % tm == 0 and n % tn == 0
    in_specs = [pl.BlockSpec((tm, k), lambda i, j: (i, 0))]
    in_specs += [pl.BlockSpec((k, tn), lambda i, j: (0, j)) for _ in weights]
    operands = [lhs, *weights]
    blocks = [_nbytes((tm, k), lhs.dtype)] + [_nbytes((k, tn), w.dtype) for w in weights]
    for arr, col0 in extras:
        assert col0 % tn == 0
        in_specs.append(pl.BlockSpec((tm, tn), lambda i, j, o=col0 // tn: (i, j + o)))
        operands.append(arr)
        blocks.append(_nbytes((tm, tn), arr.dtype))
    for p in params:
        in_specs.append(pl.BlockSpec((1, tn), lambda i, j: (0, j)))
        operands.append(p)
        blocks.append(_nbytes((8, tn), p.dtype))
    blocks.append(_nbytes((tm, tn), out_dtype))
    return pl.pallas_call(
        functools.partial(_mm_kernel, n_w=len(weights), n_ex=len(extras), n_p=len(params),
                          epilogue=epilogue),
        grid=(t // tm, n // tn),
        in_specs=in_specs,
        out_specs=pl.BlockSpec((tm, tn), lambda i, j: (i, j)),
        out_shape=jax.ShapeDtypeStruct((t, n), out_dtype),
        compiler_params=pltpu.CompilerParams(
            dimension_semantics=("arbitrary", "arbitrary"), vmem_limit_bytes=_vmem_limit(blocks)),
        name=name,
    )(*operands)


def _ep_glu(accs, extras, params):
    return accs[0] * _sigmoid(accs[1])


def _ep_identity(accs, extras, params):
    return accs[0]


def _ep_sigmoid(accs, extras, params):
    return _sigmoid(accs[0])


def _ep_softplus_bias(accs, extras, params):
    return _softplus(accs[0] + params[0])


def _ep_merge(accs, extras, params):
    return extras[0] + extras[1] * accs[0]


CONV_ROW_BLOCK = 32
CONV_COL_BLOCK = 256
LN_ROW_BLOCK = 64


def _conf_kernel(a_ref, halo_ref, g_ref, dww_ref, dwb_ref, lnw_ref, lnb_ref, w_ref,
                 o_ref, cache_ref, xs_ref, y_ref, s_ref, *, tm, n_tiles, halo_from_cache):
    i = pl.program_id(1)
    hist = CONV_WIDTH - 1
    if halo_from_cache:
        xs_ref[0:CONV_HALO - hist, :] = jnp.zeros((CONV_HALO - hist, CONV_DIM), F32)
        xs_ref[CONV_HALO - hist:CONV_HALO, :] = halo_ref[0]
    else:
        @pl.when(i == 0)
        def _():
            xs_ref[0:CONV_HALO, :] = jnp.zeros((CONV_HALO, CONV_DIM), F32)

        @pl.when(i > 0)
        def _():
            xs_ref[0:CONV_HALO, :] = halo_ref[...].astype(F32)
    xs_ref[CONV_HALO:CONV_HALO + tm, :] = a_ref[...].astype(F32)

    base = CONV_HALO - hist

    def conv_cols(cb, carry):
        cols = pl.ds(pl.multiple_of(cb * CONV_COL_BLOCK, CONV_COL_BLOCK), CONV_COL_BLOCK)
        bias = jnp.broadcast_to(dwb_ref[:, cols], (CONV_ROW_BLOCK, CONV_COL_BLOCK))
        for r0 in range(0, tm, CONV_ROW_BLOCK):
            acc = bias
            for k in range(CONV_WIDTH):
                acc = acc + (dww_ref[k:k + 1, cols]
                             * xs_ref[r0 + base + k:r0 + base + k + CONV_ROW_BLOCK, cols])
            y_ref[r0:r0 + CONV_ROW_BLOCK, cols] = acc
        return carry

    lax.fori_loop(0, CONV_DIM // CONV_COL_BLOCK, conv_cols, 0)

    ln_rows_per_step = min(LN_ROW_BLOCK, tm)

    def ln_rows(rb, carry):
        r0 = pl.multiple_of(rb * ln_rows_per_step, ln_rows_per_step)
        y = y_ref[pl.ds(r0, ln_rows_per_step), :]
        mu = jnp.mean(y, axis=-1, keepdims=True)
        yc = y - mu
        var = jnp.mean(yc * yc, axis=-1, keepdims=True)
        yn = yc * lax.rsqrt(var + EPS) * lnw_ref[...] + lnb_ref[...]
        s_ref[pl.ds(r0, ln_rows_per_step), :] = (yn * _sigmoid(yn)).astype(BF16)
        return carry

    lax.fori_loop(0, tm // ln_rows_per_step, ln_rows, 0)

    ya = jnp.dot(s_ref[...], w_ref[...], preferred_element_type=F32)
    o_ref[...] = (g_ref[...].astype(F32) * ya).astype(o_ref.dtype)

    @pl.when(i == n_tiles - 1)
    def _():
        cache_ref[0] = xs_ref[CONV_HALO + tm - hist:CONV_HALO + tm, :]


def _conf_call(a, gates, cache, dww, dwb, lnw, lnb, w, *, row0, n_seq, seq_len, tm, name):
    assert seq_len % tm == 0 and row0 % tm == 0 and tm % CONV_HALO == 0
    n_tiles = seq_len // tm
    halo_from_cache = cache is not None
    hist = CONV_WIDTH - 1
    blk0 = row0 // tm

    def row_blk(b, i):
        return blk0 + b * n_tiles + i

    if halo_from_cache:
        assert n_tiles == 1
        halo_arr = cache
        halo_spec = pl.BlockSpec((1, hist, CONV_DIM), lambda b, i: (b, 0, 0))
        halo_bytes = _nbytes((32, CONV_DIM), F32)
    else:
        per = tm // CONV_HALO
        halo_arr = a
        halo_spec = pl.BlockSpec(
            (CONV_HALO, CONV_DIM), lambda b, i: (jnp.maximum(row_blk(b, i) * per - 1, 0), 0))
        halo_bytes = _nbytes((CONV_HALO, CONV_DIM), BF16)
    blocks = [_nbytes((tm, CONV_DIM), BF16), halo_bytes, _nbytes((tm, D_MODEL), BF16),
              _nbytes((32, CONV_DIM), F32), _nbytes((8, CONV_DIM), F32) * 3,
              _nbytes((CONV_DIM, D_MODEL), BF16), _nbytes((tm, D_MODEL), BF16),
              _nbytes((32, CONV_DIM), F32)]
    scratch = [pltpu.VMEM((CONV_HALO + tm, CONV_DIM), F32), pltpu.VMEM((tm, CONV_DIM), F32),
               pltpu.VMEM((tm, CONV_DIM), BF16)]
    scratch_bytes = (_nbytes((CONV_HALO + tm, CONV_DIM), F32) + _nbytes((tm, CONV_DIM), F32)
                     + _nbytes((tm, CONV_DIM), BF16))
    return pl.pallas_call(
        functools.partial(_conf_kernel, tm=tm, n_tiles=n_tiles, halo_from_cache=halo_from_cache),
        grid=(n_seq, n_tiles),
        in_specs=[
            pl.BlockSpec((tm, CONV_DIM), lambda b, i: (row_blk(b, i), 0)),
            halo_spec,
            pl.BlockSpec((tm, D_MODEL), lambda b, i: (row_blk(b, i), 0)),
            pl.BlockSpec((CONV_WIDTH, CONV_DIM), lambda b, i: (0, 0)),
            pl.BlockSpec((1, CONV_DIM), lambda b, i: (0, 0)),
            pl.BlockSpec((1, CONV_DIM), lambda b, i: (0, 0)),
            pl.BlockSpec((1, CONV_DIM), lambda b, i: (0, 0)),
            pl.BlockSpec((CONV_DIM, D_MODEL), lambda b, i: (0, 0)),
        ],
        out_specs=[
            pl.BlockSpec((tm, D_MODEL), lambda b, i: (b * n_tiles + i, 0)),
            pl.BlockSpec((1, hist, CONV_DIM), lambda b, i: (b, 0, 0)),
        ],
        out_shape=[
            jax.ShapeDtypeStruct((n_seq * seq_len, D_MODEL), BF16),
            jax.ShapeDtypeStruct((n_seq, hist, CONV_DIM), F32),
        ],
        scratch_shapes=scratch,
        compiler_params=pltpu.CompilerParams(
            dimension_semantics=("arbitrary", "arbitrary"),
            vmem_limit_bytes=_vmem_limit(blocks, scratch_bytes)),
        name=name,
    )(a, halo_arr, gates, dww, dwb, lnw, lnb, w)


SSM_CONV_COL_BLOCK = 512
GROUP_COLS = HEADS_PER_GROUP * SSM_HEAD_DIM
B_COL0 = SSM_INNER
C_COL0 = SSM_INNER + SSM_GROUPS * SSM_STATE
NEG_BIG = -1e30


def _split3(x):
    p1 = x.astype(BF16)
    r1 = x - p1.astype(F32)
    p2 = r1.astype(BF16)
    r2 = r1 - p2.astype(F32)
    return p1, p2, r2.astype(BF16)


def _ssd_kernel(*refs, ql, n_chunks, has_h0, halo_from_cache):
    it = iter(refs)
    xbc_ref = next(it)
    halo_ref = next(it)
    dt_ref = next(it)
    z_ref = next(it)
    cw_ref = next(it)
    cb_ref = next(it)
    alog_ref = next(it)
    dskip_ref = next(it)
    nw_ref = next(it)
    h0_ref = next(it) if has_h0 else None
    y_ref = next(it)
    cache_ref = next(it)
    hout_ref = next(it)
    xs_ref = next(it)
    act_ref = next(it)
    acs_ref = next(it)
    acst_ref = next(it)
    dtt_ref = next(it)
    wendt_ref = next(it)
    ht_ref = next(it)

    q = SSD_CHUNK
    c_idx = pl.program_id(1)
    hist = SSM_CONV_WIDTH - 1

    @pl.when(c_idx == 0)
    def _():
        for g in range(SSM_GROUPS):
            if has_h0:
                ht_ref[g] = h0_ref[0, g * GROUP_COLS:(g + 1) * GROUP_COLS, :].T
            else:
                ht_ref[g] = jnp.zeros((SSM_STATE, GROUP_COLS), F32)

    if halo_from_cache:
        xs_ref[0:SSM_HALO - hist, :] = jnp.zeros((SSM_HALO - hist, SSM_XBC_DIM), F32)
        xs_ref[SSM_HALO - hist:SSM_HALO, :] = halo_ref[0]
    else:
        @pl.when(c_idx == 0)
        def _():
            xs_ref[0:SSM_HALO, :] = jnp.zeros((SSM_HALO, SSM_XBC_DIM), F32)

        @pl.when(c_idx > 0)
        def _():
            xs_ref[0:SSM_HALO, :] = halo_ref[...].astype(F32)
    xs_ref[SSM_HALO:SSM_HALO + ql, :] = xbc_ref[...].astype(F32)
    if ql < q:
        xs_ref[SSM_HALO + ql:SSM_HALO + q, :] = jnp.zeros((q - ql, SSM_XBC_DIM), F32)

    base = SSM_HALO - hist
    row_ok = lax.broadcasted_iota(jnp.int32, (q, SSM_CONV_COL_BLOCK), 0) < ql

    def conv_cols(cb, carry):
        c0 = pl.multiple_of(cb * SSM_CONV_COL_BLOCK, SSM_CONV_COL_BLOCK)
        cols = pl.ds(c0, SSM_CONV_COL_BLOCK)
        acc = jnp.broadcast_to(cb_ref[:, cols], (q, SSM_CONV_COL_BLOCK))
        for k in range(SSM_CONV_WIDTH):
            acc = acc + cw_ref[k:k + 1, cols] * xs_ref[base + k:base + k + q, cols]
        act = acc * _sigmoid(acc)
        if ql < q:
            act = jnp.where(row_ok, act, 0.0)
        act_ref[:, cols] = act
        return carry

    lax.fori_loop(0, SSM_XBC_DIM // SSM_CONV_COL_BLOCK, conv_cols, 0)

    dt = dt_ref[...]
    if ql < q:
        dt = jnp.concatenate([dt, jnp.zeros((q - ql, V7X_LANES), F32)], axis=0)
    a_neg = -jnp.exp(alog_ref[...])
    dta = dt * a_neg
    r_i = lax.broadcasted_iota(jnp.int32, (q, q), 0)
    c_i = lax.broadcasted_iota(jnp.int32, (q, q), 1)
    causal = r_i >= c_i
    tri = jnp.where(causal, 1.0, 0.0).astype(BF16)
    p1, p2, p3 = _split3(dta)
    acs = (jnp.dot(tri, p1, preferred_element_type=F32) + jnp.dot(tri, p2, preferred_element_type=F32)
           + jnp.dot(tri, p3, preferred_element_type=F32))
    acs_ref[...] = acs
    acs_t = acs.T
    dt_t = dt.T
    acst_ref[...] = acs_t
    dtt_ref[...] = dt_t
    wendt_ref[...] = jnp.exp(acs_t[:, q - 1:q] - acs_t) * dt_t

    left = c_i < SSM_HEAD_DIM
    left_row = left[0:1, :]

    for g in range(SSM_GROUPS):
        bg = act_ref[:, B_COL0 + g * SSM_STATE:B_COL0 + (g + 1) * SSM_STATE]
        cg = act_ref[:, C_COL0 + g * SSM_STATE:C_COL0 + (g + 1) * SSM_STATE].astype(BF16)
        cbm = lax.dot_general(cg, bg.astype(BF16), (((1,), (1,)), ((), ())),
                              preferred_element_type=F32)
        bg_t = bg.T
        ht_g = ht_ref[g]
        yoff = jnp.dot(cg, ht_g.astype(BF16), preferred_element_type=F32)
        y_parts, inc_parts, dec_parts = [], [], []
        for j in range(HEADS_PER_GROUP // 2):
            h_a = g * HEADS_PER_GROUP + 2 * j
            h_b = h_a + 1
            lo = h_a * SSM_HEAD_DIM
            xp = act_ref[:, lo:lo + V7X_LANES]
            lhs_parts = []
            cols = []
            for h in (h_a, h_b):
                col = jnp.broadcast_to(acs_ref[:, h:h + 1], (q, q))
                cols.append(col)
                seg = jnp.where(causal, col - acst_ref[h:h + 1, :], NEG_BIG)
                m_h = cbm * jnp.exp(seg) * dtt_ref[h:h + 1, :]
                w_h = bg_t * wendt_ref[h:h + 1, :]
                lhs_parts.append(jnp.concatenate([m_h, w_h], axis=0).astype(BF16))
            rhs = jnp.concatenate([jnp.where(left, xp, 0.0).astype(BF16),
                                   jnp.where(left, 0.0, xp).astype(BF16)], axis=0)
            res = jnp.dot(jnp.concatenate(lhs_parts, axis=1), rhs, preferred_element_type=F32)
            colsel = jnp.where(left, cols[0], cols[1])
            y_parts.append(res[0:q] + jnp.exp(colsel) * yoff[:, j * V7X_LANES:(j + 1) * V7X_LANES]
                           + dskip_ref[:, lo:lo + V7X_LANES] * xp)
            inc_parts.append(res[q:2 * q])
            last_a = jnp.broadcast_to(acs_ref[q - 1:q, h_a:h_a + 1], (1, V7X_LANES))
            last_b = jnp.broadcast_to(acs_ref[q - 1:q, h_b:h_b + 1], (1, V7X_LANES))
            dec_parts.append(jnp.exp(jnp.where(left_row, last_a, last_b)))
        ht_ref[g] = jnp.concatenate(dec_parts, axis=1) * ht_g + jnp.concatenate(inc_parts, axis=1)
        gc = slice(g * GROUP_COLS, (g + 1) * GROUP_COLS)
        yg = jnp.concatenate(y_parts, axis=1)[0:ql]
        zg = z_ref[:, gc].astype(F32)
        yz = yg * (zg * _sigmoid(zg))
        ms = jnp.mean(yz * yz, axis=-1, keepdims=True)
        y_ref[:, gc] = (yz * lax.rsqrt(ms + EPS) * nw_ref[:, gc]).astype(y_ref.dtype)

    @pl.when(c_idx == n_chunks - 1)
    def _():
        cache_ref[0] = xs_ref[SSM_HALO + ql - hist:SSM_HALO + ql, :]
        for g in range(SSM_GROUPS):
            hout_ref[0, g * GROUP_COLS:(g + 1) * GROUP_COLS, :] = ht_ref[g].T


def _ssd_call(xbc, dt, z, cache, h0, cw, cb, alog, dskip, nw, *, row0, n_seq, seq_len, ql, name):
    assert seq_len % ql == 0 and row0 % ql == 0 and ql % SSM_HALO == 0 and ql <= SSD_CHUNK
    n_chunks = seq_len // ql
    halo_from_cache = cache is not None
    has_h0 = h0 is not None
    hist = SSM_CONV_WIDTH - 1
    blk0 = row0 // ql
    q = SSD_CHUNK

    def row_blk(b, c):
        return blk0 + b * n_chunks + c

    if halo_from_cache:
        assert n_chunks == 1
        halo_arr = cache
        halo_spec = pl.BlockSpec((1, hist, SSM_XBC_DIM), lambda b, c: (b, 0, 0))
    else:
        per = ql // SSM_HALO
        halo_arr = xbc
        halo_spec = pl.BlockSpec(
            (SSM_HALO, SSM_XBC_DIM), lambda b, c: (jnp.maximum(row_blk(b, c) * per - 1, 0), 0))
    in_specs = [
        pl.BlockSpec((ql, SSM_XBC_DIM), lambda b, c: (row_blk(b, c), 0)),
        halo_spec,
        pl.BlockSpec((ql, V7X_LANES), lambda b, c: (row_blk(b, c), 0)),
        pl.BlockSpec((ql, SSM_INNER), lambda b, c: (row_blk(b, c), 0)),
        pl.BlockSpec((SSM_CONV_WIDTH, SSM_XBC_DIM), lambda b, c: (0, 0)),
        pl.BlockSpec((1, SSM_XBC_DIM), lambda b, c: (0, 0)),
        pl.BlockSpec((1, V7X_LANES), lambda b, c: (0, 0)),
        pl.BlockSpec((1, SSM_INNER), lambda b, c: (0, 0)),
        pl.BlockSpec((1, SSM_INNER), lambda b, c: (0, 0)),
    ]
    operands = [xbc, halo_arr, dt, z, cw, cb, alog, dskip, nw]
    if has_h0:
        in_specs.append(pl.BlockSpec((1, SSM_INNER, SSM_STATE), lambda b, c: (b, 0, 0)))
        operands.append(h0)
    state_bytes = _nbytes((SSM_INNER, SSM_STATE), F32)
    blocks = [_nbytes((ql, SSM_XBC_DIM), BF16), _nbytes((SSM_HALO, SSM_XBC_DIM), F32),
              _nbytes((ql, V7X_LANES), F32), _nbytes((ql, SSM_INNER), BF16),
              _nbytes((8, SSM_XBC_DIM), F32) * 2, _nbytes((8, SSM_INNER), F32) * 2,
              _nbytes((ql, SSM_INNER), BF16), _nbytes((8, SSM_XBC_DIM), F32),
              state_bytes * (2 if has_h0 else 1)]
    scratch = [
        pltpu.VMEM((SSM_HALO + q, SSM_XBC_DIM), F32),
        pltpu.VMEM((q, SSM_XBC_DIM), F32),
        pltpu.VMEM((q, V7X_LANES), F32),
        pltpu.VMEM((V7X_LANES, q), F32),
        pltpu.VMEM((V7X_LANES, q), F32),
        pltpu.VMEM((V7X_LANES, q), F32),
        pltpu.VMEM((SSM_GROUPS, SSM_STATE, GROUP_COLS), F32),
    ]
    scratch_bytes = (_nbytes((SSM_HALO + q, SSM_XBC_DIM), F32) + _nbytes((q, SSM_XBC_DIM), F32)
                     + 4 * _nbytes((q, V7X_LANES), F32) + state_bytes)
    return pl.pallas_call(
        functools.partial(_ssd_kernel, ql=ql, n_chunks=n_chunks, has_h0=has_h0,
                          halo_from_cache=halo_from_cache),
        grid=(n_seq, n_chunks),
        in_specs=in_specs,
        out_specs=[
            pl.BlockSpec((ql, SSM_INNER), lambda b, c: (b * n_chunks + c, 0)),
            pl.BlockSpec((1, hist, SSM_XBC_DIM), lambda b, c: (b, 0, 0)),
            pl.BlockSpec((1, SSM_INNER, SSM_STATE), lambda b, c: (b, 0, 0)),
        ],
        out_shape=[
            jax.ShapeDtypeStruct((n_seq * seq_len, SSM_INNER), BF16),
            jax.ShapeDtypeStruct((n_seq, hist, SSM_XBC_DIM), F32),
            jax.ShapeDtypeStruct((n_seq, SSM_INNER, SSM_STATE), F32),
        ],
        scratch_shapes=scratch,
        compiler_params=pltpu.CompilerParams(
            dimension_semantics=("arbitrary", "arbitrary"),
            vmem_limit_bytes=_vmem_limit(blocks, scratch_bytes)),
        name=name,
    )(*operands)


def _outproj_kernel(m_ref, w_ref, xp_ref, xs_ref, o_ref, *, n_prompt_tiles):
    i = pl.program_id(0)
    acc = jnp.dot(m_ref[...], w_ref[...], preferred_element_type=F32)

    @pl.when(i < n_prompt_tiles)
    def _():
        o_ref[...] = xp_ref[...] + acc

    @pl.when(i >= n_prompt_tiles)
    def _():
        o_ref[...] = xs_ref[...] + acc


def _outproj_call(m, w, xp, xs):
    t, d = m.shape
    npt = xp.shape[0] // ROW_TILE
    blocks = [_nbytes((ROW_TILE, d), BF16), _nbytes((d, d), BF16)] + [_nbytes((ROW_TILE, d), F32)] * 3
    return pl.pallas_call(
        functools.partial(_outproj_kernel, n_prompt_tiles=npt),
        grid=(t // ROW_TILE,),
        in_specs=[
            pl.BlockSpec((ROW_TILE, d), lambda i: (i, 0)),
            pl.BlockSpec((d, d), lambda i: (0, 0)),
            pl.BlockSpec((ROW_TILE, d), lambda i: (jnp.minimum(i, npt - 1), 0)),
            pl.BlockSpec((ROW_TILE, d), lambda i: (0, 0)),
        ],
        out_specs=pl.BlockSpec((ROW_TILE, d), lambda i: (i, 0)),
        out_shape=jax.ShapeDtypeStruct((t, d), F32),
        compiler_params=pltpu.CompilerParams(
            dimension_semantics=("arbitrary",), vmem_limit_bytes=_vmem_limit(blocks)),
        name="out_proj_residual",
    )(m, w, xp, xs)


MLP_FF_TILE = 1024


def _mlp_kernel(x1_ref, nw_ref, wup_ref, wdn_ref, fw_ref, yp_ref, ys_ref, h2_ref, acc_ref,
                *, n_prompt_tiles, n_ff):
    i = pl.program_id(0)
    f = pl.program_id(1)

    @pl.when(f == 0)
    def _():
        h2_ref[...] = _rms(x1_ref[...], nw_ref[...]).astype(BF16)
        acc_ref[...] = jnp.zeros_like(acc_ref)

    hid = jnp.dot(h2_ref[...], wup_ref[...], preferred_element_type=F32)
    hid = jnp.square(jnp.maximum(hid, 0.0))
    acc_ref[...] += jnp.dot(hid.astype(BF16), wdn_ref[...], preferred_element_type=F32)

    @pl.when(f == n_ff - 1)
    def _():
        out = _rms(x1_ref[...] + acc_ref[...], fw_ref[...])

        @pl.when(i < n_prompt_tiles)
        def _():
            yp_ref[...] = out

        @pl.when(i >= n_prompt_tiles)
        def _():
            ys_ref[...] = out


def _mlp_call(x1, nw, wup, wdn, fw, *, n_prompt_rows):
    t, d = x1.shape
    npt = n_prompt_rows // ROW_TILE
    n_ff = D_FF // MLP_FF_TILE
    blocks = [_nbytes((ROW_TILE, d), F32), _nbytes((d, MLP_FF_TILE), BF16),
              _nbytes((MLP_FF_TILE, d), BF16), _nbytes((ROW_TILE, d), F32) * 2]
    scratch_bytes = _nbytes((ROW_TILE, d), BF16) + _nbytes((ROW_TILE, d), F32)
    return pl.pallas_call(
        functools.partial(_mlp_kernel, n_prompt_tiles=npt, n_ff=n_ff),
        grid=(t // ROW_TILE, n_ff),
        in_specs=[
            pl.BlockSpec((ROW_TILE, d), lambda i, f: (i, 0)),
            pl.BlockSpec((1, d), lambda i, f: (0, 0)),
            pl.BlockSpec((d, MLP_FF_TILE), lambda i, f: (0, f)),
            pl.BlockSpec((MLP_FF_TILE, d), lambda i, f: (f, 0)),
            pl.BlockSpec((1, d), lambda i, f: (0, 0)),
        ],
        out_specs=[
            pl.BlockSpec((ROW_TILE, d), lambda i, f: (jnp.minimum(i, npt - 1), 0)),
            pl.BlockSpec((ROW_TILE, d), lambda i, f: (0, 0)),
        ],
        out_shape=[
            jax.ShapeDtypeStruct((n_prompt_rows, d), F32),
            jax.ShapeDtypeStruct((t - n_prompt_rows, d), F32),
        ],
        scratch_shapes=[pltpu.VMEM((ROW_TILE, d), BF16), pltpu.VMEM((ROW_TILE, d), F32)],
        compiler_params=pltpu.CompilerParams(
            dimension_semantics=("arbitrary", "arbitrary"),
            vmem_limit_bytes=_vmem_limit(blocks, scratch_bytes)),
        name="mlp_final_norm",
    )(x1, nw, wup, wdn, fw)


def kernel(x_prompt, x_sample, cache_conv, cache_ssm_conv, state_ssm, mix_norm_w, w_in, conv_dw_w,
           conv_dw_b, conv_ln_w, conv_ln_b, w_conv_out, ssm_conv_w, ssm_conv_b, dt_bias, a_log, d_skip,
           ssm_norm_w, w_ssm_out, w_out, mlp_norm_w, w_up, w_down, final_norm_w):
    bp, lp, d = x_prompt.shape
    bs, ls, _ = x_sample.shape
    tp, ts = bp * lp, bs * ls
    xp = x_prompt.reshape(tp, d)
    xs = x_sample.reshape(ts, d)

    wi = w_in[0]
    o_z = 2 * CONV_DIM
    o_xbc = o_z + SSM_INNER
    o_dt = o_xbc + SSM_XBC_DIM
    o_g = o_dt + SSM_HEADS
    w_val = wi[:, 0:CONV_DIM].astype(BF16)
    w_gate = wi[:, CONV_DIM:o_z].astype(BF16)
    w_z = wi[:, o_z:o_xbc].astype(BF16)
    w_xbc = wi[:, o_xbc:o_dt].astype(BF16)
    w_dt = jnp.pad(wi[:, o_dt:o_g], ((0, 0), (0, V7X_LANES - SSM_HEADS))).astype(BF16)
    w_g = wi[:, o_g:].astype(BF16)
    dt_b = jnp.pad(dt_bias[0], (0, V7X_LANES - SSM_HEADS)).reshape(1, V7X_LANES)
    alog = jnp.pad(a_log[0], (0, V7X_LANES - SSM_HEADS)).reshape(1, V7X_LANES)
    dskip = jnp.repeat(d_skip[0], SSM_HEAD_DIM).reshape(1, SSM_INNER)
    row = lambda v: v.reshape(1, -1)

    h = _norm_call(xp, xs, row(mix_norm_w[0]))

    tm, tn = ROW_TILE, 1024
    a = _mm_call(h, [w_val, w_gate], _ep_glu, BF16, tm=tm, tn=tn, name="in_proj_glu")
    z = _mm_call(h, [w_z], _ep_identity, BF16, tm=tm, tn=tn, name="in_proj_z")
    xbc = _mm_call(h, [w_xbc], _ep_identity, BF16, tm=tm, tn=tn, name="in_proj_xbc")
    dt = _mm_call(h, [w_dt], _ep_softplus_bias, F32, tm=tm, tn=V7X_LANES, params=[dt_b], name="in_proj_dt")
    gates = _mm_call(h, [w_g], _ep_sigmoid, BF16, tm=tm, tn=tn, name="in_proj_gates")

    conf_args = (conv_dw_w[0], row(conv_dw_b[0]), row(conv_ln_w[0]), row(conv_ln_b[0]),
                 w_conv_out[0].astype(BF16))
    ma_p, conv_p = _conf_call(a, gates, None, *conf_args, row0=0, n_seq=bp, seq_len=lp, tm=ROW_TILE,
                              name="conformer_prompt")
    ma_s, conv_s = _conf_call(a, gates, cache_conv[0], *conf_args, row0=tp, n_seq=bs, seq_len=ls, tm=ls,
                              name="conformer_sample")

    ssd_args = (ssm_conv_w[0], row(ssm_conv_b[0]), alog, dskip, row(ssm_norm_w[0]))
    yn_p, xbc_p, ssm_p = _ssd_call(xbc, dt, z, None, None, *ssd_args, row0=0, n_seq=bp, seq_len=lp,
                                   ql=SSD_CHUNK, name="ssd_prompt")
    h0 = state_ssm[0].reshape(bs, SSM_INNER, SSM_STATE)
    yn_s, xbc_s, ssm_s = _ssd_call(xbc, dt, z, cache_ssm_conv[0], h0, *ssd_args, row0=tp, n_seq=bs,
                                   seq_len=ls, ql=ls, name="ssd_sample")

    m_a = jnp.concatenate([ma_p, ma_s], axis=0)
    yn = jnp.concatenate([yn_p, yn_s], axis=0)
    m = _mm_call(yn, [w_ssm_out[0].astype(BF16)], _ep_merge, BF16, tm=tm, tn=tn,
                 extras=[(m_a, 0), (gates, D_MODEL)], name="ssm_out_merge")
    x1 = _outproj_call(m, w_out[0].astype(BF16), xp, xs)
    y_p, y_s = _mlp_call(x1, row(mlp_norm_w[0]), w_up[0].astype(BF16), w_down[0].astype(BF16),
                         row(final_norm_w), n_prompt_rows=tp)

    hshape = (SSM_HEADS, SSM_HEAD_DIM, SSM_STATE)
    return (y_p.reshape(bp, lp, d), y_s.reshape(bs, ls, d),
            conv_p[None], xbc_p[None], ssm_p.reshape(1, bp, *hshape),
            conv_s[None], xbc_s[None], ssm_s.reshape(1, bs, *hshape))
```

```python
import functools

import jax
import jax.numpy as jnp
from jax import lax
from jax.experimental import pallas as pl
from jax.experimental.pallas import tpu as pltpu

F32 = jnp.float32
BF16 = jnp.bfloat16

D_MODEL = 2048
CONV_DIM = D_MODEL
CONV_WIDTH = 31
SSM_INNER = 2 * D_MODEL
SSM_HEAD_DIM = 64
SSM_HEADS = SSM_INNER // SSM_HEAD_DIM
SSM_GROUPS = 8
HEADS_PER_GROUP = SSM_HEADS // SSM_GROUPS
SSM_STATE = 128
SSM_CONV_WIDTH = 4
SSM_XBC_DIM = SSM_INNER + 2 * SSM_GROUPS * SSM_STATE
D_FF = 4 * D_MODEL
EPS = 1e-5

V7X_LANES = 128
V7X_VMEM_BYTES = 64 * 1024 * 1024
V7X_VMEM_REQUEST_CAP = 56 * 1024 * 1024
COMPILER_TEMP_BYTES = 12 * 1024 * 1024

SSD_CHUNK = 128
CONV_HALO = 32
SSM_HALO = 16
ROW_TILE = 512
MM_ROW_TILE = 1536


def _vmem_limit(block_bytes, scratch_bytes=0):
    need = 2 * sum(block_bytes) + scratch_bytes + COMPILER_TEMP_BYTES
    return int(min(need, V7X_VMEM_REQUEST_CAP))


def _nbytes(shape, dtype):
    n = 1
    for s in shape:
        n *= s
    return n * jnp.dtype(dtype).itemsize


def _sigmoid(x):
    return 1.0 / (1.0 + jnp.exp(-x))


def _softplus(x):
    return jnp.maximum(x, 0.0) + jnp.log1p(jnp.exp(-jnp.abs(x)))


def _rms(x, w):
    ms = jnp.mean(x * x, axis=-1, keepdims=True)
    return x * lax.rsqrt(ms + EPS) * w


def _norm_kernel(xp_ref, xs_ref, w_ref, o_ref, *, n_prompt_tiles):
    i = pl.program_id(0)

    @pl.when(i < n_prompt_tiles)
    def _():
        o_ref[...] = _rms(xp_ref[...], w_ref[...]).astype(o_ref.dtype)

    @pl.when(i >= n_prompt_tiles)
    def _():
        o_ref[...] = _rms(xs_ref[...], w_ref[...]).astype(o_ref.dtype)


def _norm_call(xp, xs, w):
    tp, d = xp.shape
    ts = xs.shape[0]
    assert ts == ROW_TILE and tp % ROW_TILE == 0
    npt = tp // ROW_TILE
    blocks = [_nbytes((ROW_TILE, d), F32)] * 2 + [_nbytes((ROW_TILE, d), BF16)]
    return pl.pallas_call(
        functools.partial(_norm_kernel, n_prompt_tiles=npt),
        grid=(npt + 1,),
        in_specs=[
            pl.BlockSpec((ROW_TILE, d), lambda i: (jnp.minimum(i, npt - 1), 0)),
            pl.BlockSpec((ROW_TILE, d), lambda i: (0, 0)),
            pl.BlockSpec((1, d), lambda i: (0, 0)),
        ],
        out_specs=pl.BlockSpec((ROW_TILE, d), lambda i: (i, 0)),
        out_shape=jax.ShapeDtypeStruct((tp + ts, d), BF16),
        compiler_params=pltpu.CompilerParams(
            dimension_semantics=("arbitrary",), vmem_limit_bytes=_vmem_limit(blocks)),
        name="mix_rmsnorm",
    )(xp, xs, w)


def _mm_kernel(*refs, n_w, n_ex, n_p, epilogue):
    lhs_ref = refs[0]
    w_refs = refs[1:1 + n_w]
    ex_refs = refs[1 + n_w:1 + n_w + n_ex]
    p_refs = refs[1 + n_w + n_ex:1 + n_w + n_ex + n_p]
    o_ref = refs[-1]
    lhs = lhs_ref[...]
    accs = [jnp.dot(lhs, w[...], preferred_element_type=F32) for w in w_refs]
    out = epilogue(accs, [e[...].astype(F32) for e in ex_refs], [p[...] for p in p_refs])
    o_ref[...] = out.astype(o_ref.dtype)


def _mm_call(lhs, weights, epilogue, out_dtype, *, tm, tn, extras=(), params=(), name):
    t, k = lhs.shape
    n = weights[0].shape[1]
    assert t % tm == 0 and n % tn == 0
    in_specs = [pl.BlockSpec((tm, k), lambda i, j: (i, 0))]
    in_specs += [pl.BlockSpec((k, tn), lambda i, j: (0, j)) for _ in weights]
    operands = [lhs, *weights]
    blocks = [_nbytes((tm, k), lhs.dtype)] + [_nbytes((k, tn), w.dtype) for w in weights]
    for arr, col0 in extras:
        assert col0 % tn == 0
        in_specs.append(pl.BlockSpec((tm, tn), lambda i, j, o=col0 // tn: (i, j + o)))
        operands.append(arr)
        blocks.append(_nbytes((tm, tn), arr.dtype))
    for p in params:
        in_specs.append(pl.BlockSpec((1, tn), lambda i, j: (0, j)))
        operands.append(p)
        blocks.append(_nbytes((8, tn), p.dtype))
    blocks.append(_nbytes((tm, tn), out_dtype))
    return pl.pallas_call(
        functools.partial(_mm_kernel, n_w=len(weights), n_ex=len(extras), n_p=len(params),
                          epilogue=epilogue),
        grid=(t // tm, n // tn),
        in_specs=in_specs,
        out_specs=pl.BlockSpec((tm, tn), lambda i, j: (i, j)),
        out_shape=jax.ShapeDtypeStruct((t, n), out_dtype),
        compiler_params=pltpu.CompilerParams(
            dimension_semantics=("arbitrary", "arbitrary"), vmem_limit_bytes=_vmem_limit(blocks)),
        name=name,
    )(*operands)


def _ep_glu(accs, extras, params):
    return accs[0] * _sigmoid(accs[1])


def _ep_identity(accs, extras, params):
    return accs[0]


def _ep_sigmoid(accs, extras, params):
    return _sigmoid(accs[0])


def _ep_softplus_bias(accs, extras, params):
    return _softplus(accs[0] + params[0])


def _ep_merge(accs, extras, params):
    return extras[0] + extras[1] * accs[0]


CONV_ROW_BLOCK = 64
CONV_COL_BLOCK = 128
LN_ROW_BLOCK = 256
SUBLANES = 8


def _conf_kernel(*refs, tm, n_tiles, halo_from_cache, aliased):
    (a_ref, halo_ref, g_ref, dww_ref, dwb_ref, lnw_ref, lnb_ref, w_ref) = refs[:8]
    o_ref, cache_ref, xs_ref, zs_ref, y_ref, s_ref = refs[8 + int(aliased):]
    i = pl.program_id(1)
    hist = CONV_WIDTH - 1
    if halo_from_cache:
        xs_ref[0:CONV_HALO - hist, :] = jnp.zeros((CONV_HALO - hist, CONV_DIM), F32)
        xs_ref[CONV_HALO - hist:CONV_HALO, :] = halo_ref[0]
    else:
        @pl.when(i == 0)
        def _():
            xs_ref[0:CONV_HALO, :] = jnp.zeros((CONV_HALO, CONV_DIM), F32)

        @pl.when(i > 0)
        def _():
            xs_ref[0:CONV_HALO, :] = halo_ref[...].astype(F32)
    xs_ref[CONV_HALO:CONV_HALO + tm, :] = a_ref[...].astype(F32)

    base = CONV_HALO - hist
    rb = min(CONV_ROW_BLOCK, tm)

    def conv_cols(cb, carry):
        cols = pl.ds(pl.multiple_of(cb * CONV_COL_BLOCK, CONV_COL_BLOCK), CONV_COL_BLOCK)
        for s in range(SUBLANES):
            rows = tm + SUBLANES * ((hist - s) // SUBLANES)
            zs_ref[s, 0:rows, :] = xs_ref[base + s:base + s + rows, cols]
        bias = jnp.broadcast_to(dwb_ref[:, cols], (rb, CONV_COL_BLOCK))
        for r0 in range(0, tm, rb):
            acc = bias
            for k in range(CONV_WIDTH):
                s, q = k % SUBLANES, k // SUBLANES
                acc = acc + dww_ref[k:k + 1, cols] * zs_ref[s, r0 + SUBLANES * q:r0 + SUBLANES * q + rb, :]
            y_ref[r0:r0 + rb, cols] = acc
        return carry

    lax.fori_loop(0, CONV_DIM // CONV_COL_BLOCK, conv_cols, 0)

    ln_rows_per_step = min(LN_ROW_BLOCK, tm)

    def ln_rows(rb, carry):
        r0 = pl.multiple_of(rb * ln_rows_per_step, ln_rows_per_step)
        y = y_ref[pl.ds(r0, ln_rows_per_step), :]
        mu = jnp.mean(y, axis=-1, keepdims=True)
        yc = y - mu
        var = jnp.mean(yc * yc, axis=-1, keepdims=True)
        yn = yc * lax.rsqrt(var + EPS) * lnw_ref[...] + lnb_ref[...]
        s_ref[pl.ds(r0, ln_rows_per_step), :] = (yn * _sigmoid(yn)).astype(BF16)
        return carry

    lax.fori_loop(0, tm // ln_rows_per_step, ln_rows, 0)

    ya = jnp.dot(s_ref[...], w_ref[...], preferred_element_type=F32)
    o_ref[...] = (g_ref[...].astype(F32) * ya).astype(o_ref.dtype)

    @pl.when(i == n_tiles - 1)
    def _():
        cache_ref[0] = xs_ref[CONV_HALO + tm - hist:CONV_HALO + tm, :]


def _conf_call(a, gates, cache, prev, dww, dwb, lnw, lnb, w, *, row0, n_seq, seq_len, tm, name):
    assert seq_len % tm == 0 and row0 % tm == 0 and tm % CONV_HALO == 0
    t = a.shape[0]
    n_tiles = seq_len // tm
    halo_from_cache = cache is not None
    hist = CONV_WIDTH - 1
    blk0 = row0 // tm

    def row_blk(b, i):
        return blk0 + b * n_tiles + i

    if halo_from_cache:
        assert n_tiles == 1
        halo_arr = cache
        halo_spec = pl.BlockSpec((1, hist, CONV_DIM), lambda b, i: (b, 0, 0))
        halo_bytes = _nbytes((32, CONV_DIM), F32)
    else:
        per = tm // CONV_HALO
        halo_arr = a
        halo_spec = pl.BlockSpec(
            (CONV_HALO, CONV_DIM), lambda b, i: (jnp.maximum(row_blk(b, i) * per - 1, 0), 0))
        halo_bytes = _nbytes((CONV_HALO, CONV_DIM), BF16)
    blocks = [_nbytes((tm, CONV_DIM), BF16), halo_bytes, _nbytes((tm, D_MODEL), BF16),
              _nbytes((32, CONV_DIM), F32), _nbytes((8, CONV_DIM), F32) * 3,
              _nbytes((CONV_DIM, D_MODEL), BF16), _nbytes((tm, D_MODEL), BF16),
              _nbytes((32, CONV_DIM), F32)]
    scratch_dims = [((CONV_HALO + tm, CONV_DIM), F32),
                    ((SUBLANES, CONV_HALO + tm, CONV_COL_BLOCK), F32),
                    ((tm, CONV_DIM), F32), ((tm, CONV_DIM), BF16)]
    in_specs = [
        pl.BlockSpec((tm, CONV_DIM), lambda b, i: (row_blk(b, i), 0)),
        halo_spec,
        pl.BlockSpec((tm, D_MODEL), lambda b, i: (row_blk(b, i), 0)),
        pl.BlockSpec((CONV_WIDTH, CONV_DIM), lambda b, i: (0, 0)),
        pl.BlockSpec((1, CONV_DIM), lambda b, i: (0, 0)),
        pl.BlockSpec((1, CONV_DIM), lambda b, i: (0, 0)),
        pl.BlockSpec((1, CONV_DIM), lambda b, i: (0, 0)),
        pl.BlockSpec((CONV_DIM, D_MODEL), lambda b, i: (0, 0)),
    ]
    operands = [a, halo_arr, gates, dww, dwb, lnw, lnb, w]
    aliases = {}
    if prev is not None:
        in_specs.append(pl.BlockSpec(memory_space=pl.ANY))
        operands.append(prev)
        aliases = {len(operands) - 1: 0}
    return pl.pallas_call(
        functools.partial(_conf_kernel, tm=tm, n_tiles=n_tiles, halo_from_cache=halo_from_cache,
                          aliased=prev is not None),
        grid=(n_seq, n_tiles),
        in_specs=in_specs,
        out_specs=[
            pl.BlockSpec((tm, D_MODEL), lambda b, i: (row_blk(b, i), 0)),
            pl.BlockSpec((1, hist, CONV_DIM), lambda b, i: (b, 0, 0)),
        ],
        out_shape=[
            jax.ShapeDtypeStruct((t, D_MODEL), BF16),
            jax.ShapeDtypeStruct((n_seq, hist, CONV_DIM), F32),
        ],
        scratch_shapes=[pltpu.VMEM(shape, dt) for shape, dt in scratch_dims],
        input_output_aliases=aliases,
        compiler_params=pltpu.CompilerParams(
            dimension_semantics=("arbitrary", "arbitrary"),
            vmem_limit_bytes=_vmem_limit(blocks, sum(_nbytes(sh, dt) for sh, dt in scratch_dims))),
        name=name,
    )(*operands)


SSM_CONV_COL_BLOCK = 512
GROUP_COLS = HEADS_PER_GROUP * SSM_HEAD_DIM
B_COL0 = SSM_INNER
C_COL0 = SSM_INNER + SSM_GROUPS * SSM_STATE
NEG_BIG = -1e30


def _split3(x):
    p1 = x.astype(BF16)
    r1 = x - p1.astype(F32)
    p2 = r1.astype(BF16)
    r2 = r1 - p2.astype(F32)
    return p1, p2, r2.astype(BF16)


def _ssd_kernel(*refs, ql, n_chunks, has_h0, halo_from_cache, aliased):
    it = iter(refs)
    xbc_ref = next(it)
    halo_ref = next(it)
    dt_ref = next(it)
    z_ref = next(it)
    cw_ref = next(it)
    cb_ref = next(it)
    alog_ref = next(it)
    dskip_ref = next(it)
    nw_ref = next(it)
    h0_ref = next(it) if has_h0 else None
    if aliased:
        next(it)
    y_ref = next(it)
    cache_ref = next(it)
    hout_ref = next(it)
    xs_ref = next(it)
    act_ref = next(it)
    acs_ref = next(it)
    acst_ref = next(it)
    dtt_ref = next(it)
    wendt_ref = next(it)
    ht_ref = next(it)

    q = SSD_CHUNK
    c_idx = pl.program_id(1)
    hist = SSM_CONV_WIDTH - 1

    @pl.when(c_idx == 0)
    def _():
        for g in range(SSM_GROUPS):
            if has_h0:
                ht_ref[g] = h0_ref[0, g * GROUP_COLS:(g + 1) * GROUP_COLS, :].T
            else:
                ht_ref[g] = jnp.zeros((SSM_STATE, GROUP_COLS), F32)

    if halo_from_cache:
        xs_ref[0:SSM_HALO - hist, :] = jnp.zeros((SSM_HALO - hist, SSM_XBC_DIM), F32)
        xs_ref[SSM_HALO - hist:SSM_HALO, :] = halo_ref[0]
    else:
        @pl.when(c_idx == 0)
        def _():
            xs_ref[0:SSM_HALO, :] = jnp.zeros((SSM_HALO, SSM_XBC_DIM), F32)

        @pl.when(c_idx > 0)
        def _():
            xs_ref[0:SSM_HALO, :] = halo_ref[...].astype(F32)
    xs_ref[SSM_HALO:SSM_HALO + ql, :] = xbc_ref[...].astype(F32)
    if ql < q:
        xs_ref[SSM_HALO + ql:SSM_HALO + q, :] = jnp.zeros((q - ql, SSM_XBC_DIM), F32)

    base = SSM_HALO - hist
    row_ok = lax.broadcasted_iota(jnp.int32, (q, SSM_CONV_COL_BLOCK), 0) < ql

    def conv_cols(cb, carry):
        c0 = pl.multiple_of(cb * SSM_CONV_COL_BLOCK, SSM_CONV_COL_BLOCK)
        cols = pl.ds(c0, SSM_CONV_COL_BLOCK)
        acc = jnp.broadcast_to(cb_ref[:, cols], (q, SSM_CONV_COL_BLOCK))
        for k in range(SSM_CONV_WIDTH):
            acc = acc + cw_ref[k:k + 1, cols] * xs_ref[base + k:base + k + q, cols]
        act = acc * _sigmoid(acc)
        if ql < q:
            act = jnp.where(row_ok, act, 0.0)
        act_ref[:, cols] = act
        return carry

    lax.fori_loop(0, SSM_XBC_DIM // SSM_CONV_COL_BLOCK, conv_cols, 0)

    dt = dt_ref[...]
    if ql < q:
        dt = jnp.concatenate([dt, jnp.zeros((q - ql, V7X_LANES), F32)], axis=0)
    a_neg = -jnp.exp(alog_ref[...])
    dta = dt * a_neg
    r_i = lax.broadcasted_iota(jnp.int32, (q, q), 0)
    c_i = lax.broadcasted_iota(jnp.int32, (q, q), 1)
    causal = r_i >= c_i
    tri = jnp.where(causal, 1.0, 0.0).astype(BF16)
    p1, p2, p3 = _split3(dta)
    acs = (jnp.dot(tri, p1, preferred_element_type=F32) + jnp.dot(tri, p2, preferred_element_type=F32)
           + jnp.dot(tri, p3, preferred_element_type=F32))
    acs_ref[...] = acs
    acs_t = acs.T
    dt_t = dt.T
    acst_ref[...] = acs_t
    dtt_ref[...] = dt_t
    wendt_ref[...] = jnp.exp(acs_t[:, q - 1:q] - acs_t) * dt_t

    left = c_i < SSM_HEAD_DIM
    left_row = left[0:1, :]

    for g in range(SSM_GROUPS):
        bg = act_ref[:, B_COL0 + g * SSM_STATE:B_COL0 + (g + 1) * SSM_STATE]
        cg = act_ref[:, C_COL0 + g * SSM_STATE:C_COL0 + (g + 1) * SSM_STATE].astype(BF16)
        cbm = lax.dot_general(cg, bg.astype(BF16), (((1,), (1,)), ((), ())),
                              preferred_element_type=F32)
        bg_t = bg.T
        ht_g = ht_ref[g]
        yoff = jnp.dot(cg, ht_g.astype(BF16), preferred_element_type=F32)
        y_parts, inc_parts, dec_parts = [], [], []
        for j in range(HEADS_PER_GROUP // 2):
            h_a = g * HEADS_PER_GROUP + 2 * j
            h_b = h_a + 1
            lo = h_a * SSM_HEAD_DIM
            xp = act_ref[:, lo:lo + V7X_LANES]
            lhs_parts = []
            cols = []
            for h in (h_a, h_b):
                col = jnp.broadcast_to(acs_ref[:, h:h + 1], (q, q))
                cols.append(col)
                seg = jnp.where(causal, col - acst_ref[h:h + 1, :], NEG_BIG)
                m_h = cbm * jnp.exp(seg) * dtt_ref[h:h + 1, :]
                w_h = bg_t * wendt_ref[h:h + 1, :]
                lhs_parts.append(jnp.concatenate([m_h, w_h], axis=0).astype(BF16))
            rhs = jnp.concatenate([jnp.where(left, xp, 0.0).astype(BF16),
                                   jnp.where(left, 0.0, xp).astype(BF16)], axis=0)
            res = jnp.dot(jnp.concatenate(lhs_parts, axis=1), rhs, preferred_element_type=F32)
            colsel = jnp.where(left, cols[0], cols[1])
            y_parts.append(res[0:q] + jnp.exp(colsel) * yoff[:, j * V7X_LANES:(j + 1) * V7X_LANES]
                           + dskip_ref[:, lo:lo + V7X_LANES] * xp)
            inc_parts.append(res[q:2 * q])
            last_a = jnp.broadcast_to(acs_ref[q - 1:q, h_a:h_a + 1], (1, V7X_LANES))
            last_b = jnp.broadcast_to(acs_ref[q - 1:q, h_b:h_b + 1], (1, V7X_LANES))
            dec_parts.append(jnp.exp(jnp.where(left_row, last_a, last_b)))
        ht_ref[g] = jnp.concatenate(dec_parts, axis=1) * ht_g + jnp.concatenate(inc_parts, axis=1)
        gc = slice(g * GROUP_COLS, (g + 1) * GROUP_COLS)
        yg = jnp.concatenate(y_parts, axis=1)[0:ql]
        zg = z_ref[:, gc].astype(F32)
        yz = yg * (zg * _sigmoid(zg))
        ms = jnp.mean(yz * yz, axis=-1, keepdims=True)
        y_ref[:, gc] = (yz * lax.rsqrt(ms + EPS) * nw_ref[:, gc]).astype(y_ref.dtype)

    @pl.when(c_idx == n_chunks - 1)
    def _():
        cache_ref[0] = xs_ref[SSM_HALO + ql - hist:SSM_HALO + ql, :]
        for g in range(SSM_GROUPS):
            hout_ref[0, g * GROUP_COLS:(g + 1) * GROUP_COLS, :] = ht_ref[g].T


def _ssd_call(xbc, dt, z, cache, h0, prev, cw, cb, alog, dskip, nw, *, row0, n_seq, seq_len, ql, name):
    assert seq_len % ql == 0 and row0 % ql == 0 and ql % SSM_HALO == 0 and ql <= SSD_CHUNK
    n_chunks = seq_len // ql
    halo_from_cache = cache is not None
    has_h0 = h0 is not None
    hist = SSM_CONV_WIDTH - 1
    blk0 = row0 // ql
    q = SSD_CHUNK

    def row_blk(b, c):
        return blk0 + b * n_chunks + c

    if halo_from_cache:
        assert n_chunks == 1
        halo_arr = cache
        halo_spec = pl.BlockSpec((1, hist, SSM_XBC_DIM), lambda b, c: (b, 0, 0))
    else:
        per = ql // SSM_HALO
        halo_arr = xbc
        halo_spec = pl.BlockSpec(
            (SSM_HALO, SSM_XBC_DIM), lambda b, c: (jnp.maximum(row_blk(b, c) * per - 1, 0), 0))
    in_specs = [
        pl.BlockSpec((ql, SSM_XBC_DIM), lambda b, c: (row_blk(b, c), 0)),
        halo_spec,
        pl.BlockSpec((ql, V7X_LANES), lambda b, c: (row_blk(b, c), 0)),
        pl.BlockSpec((ql, SSM_INNER), lambda b, c: (row_blk(b, c), 0)),
        pl.BlockSpec((SSM_CONV_WIDTH, SSM_XBC_DIM), lambda b, c: (0, 0)),
        pl.BlockSpec((1, SSM_XBC_DIM), lambda b, c: (0, 0)),
        pl.BlockSpec((1, V7X_LANES), lambda b, c: (0, 0)),
        pl.BlockSpec((1, SSM_INNER), lambda b, c: (0, 0)),
        pl.BlockSpec((1, SSM_INNER), lambda b, c: (0, 0)),
    ]
    operands = [xbc, halo_arr, dt, z, cw, cb, alog, dskip, nw]
    if has_h0:
        in_specs.append(pl.BlockSpec((1, SSM_INNER, SSM_STATE), lambda b, c: (b, 0, 0)))
        operands.append(h0)
    aliases = {}
    if prev is not None:
        in_specs.append(pl.BlockSpec(memory_space=pl.ANY))
        operands.append(prev)
        aliases = {len(operands) - 1: 0}
    state_bytes = _nbytes((SSM_INNER, SSM_STATE), F32)
    blocks = [_nbytes((ql, SSM_XBC_DIM), BF16), _nbytes((SSM_HALO, SSM_XBC_DIM), F32),
              _nbytes((ql, V7X_LANES), F32), _nbytes((ql, SSM_INNER), BF16),
              _nbytes((8, SSM_XBC_DIM), F32) * 2, _nbytes((8, SSM_INNER), F32) * 2,
              _nbytes((ql, SSM_INNER), BF16), _nbytes((8, SSM_XBC_DIM), F32),
              state_bytes * (2 if has_h0 else 1)]
    scratch_dims = [
        ((SSM_HALO + q, SSM_XBC_DIM), F32),
        ((q, SSM_XBC_DIM), F32),
        ((q, V7X_LANES), F32),
        ((V7X_LANES, q), F32),
        ((V7X_LANES, q), F32),
        ((V7X_LANES, q), F32),
        ((SSM_GROUPS, SSM_STATE, GROUP_COLS), F32),
    ]
    return pl.pallas_call(
        functools.partial(_ssd_kernel, ql=ql, n_chunks=n_chunks, has_h0=has_h0,
                          halo_from_cache=halo_from_cache, aliased=prev is not None),
        grid=(n_seq, n_chunks),
        in_specs=in_specs,
        out_specs=[
            pl.BlockSpec((ql, SSM_INNER), lambda b, c: (row_blk(b, c), 0)),
            pl.BlockSpec((1, hist, SSM_XBC_DIM), lambda b, c: (b, 0, 0)),
            pl.BlockSpec((1, SSM_INNER, SSM_STATE), lambda b, c: (b, 0, 0)),
        ],
        out_shape=[
            jax.ShapeDtypeStruct((xbc.shape[0], SSM_INNER), BF16),
            jax.ShapeDtypeStruct((n_seq, hist, SSM_XBC_DIM), F32),
            jax.ShapeDtypeStruct((n_seq, SSM_INNER, SSM_STATE), F32),
        ],
        scratch_shapes=[pltpu.VMEM(shape, dt_) for shape, dt_ in scratch_dims],
        input_output_aliases=aliases,
        compiler_params=pltpu.CompilerParams(
            dimension_semantics=("arbitrary", "arbitrary"),
            vmem_limit_bytes=_vmem_limit(blocks, sum(_nbytes(sh, dt_) for sh, dt_ in scratch_dims))),
        name=name,
    )(*operands)


def _outproj_kernel(m_ref, w_ref, xp_ref, xs_ref, o_ref, *, n_prompt_tiles):
    i = pl.program_id(0)
    acc = jnp.dot(m_ref[...], w_ref[...], preferred_element_type=F32)

    @pl.when(i < n_prompt_tiles)
    def _():
        o_ref[...] = xp_ref[...] + acc

    @pl.when(i >= n_prompt_tiles)
    def _():
        o_ref[...] = xs_ref[...] + acc


def _outproj_call(m, w, xp, xs):
    t, d = m.shape
    npt = xp.shape[0] // ROW_TILE
    blocks = [_nbytes((ROW_TILE, d), BF16), _nbytes((d, d), BF16)] + [_nbytes((ROW_TILE, d), F32)] * 3
    return pl.pallas_call(
        functools.partial(_outproj_kernel, n_prompt_tiles=npt),
        grid=(t // ROW_TILE,),
        in_specs=[
            pl.BlockSpec((ROW_TILE, d), lambda i: (i, 0)),
            pl.BlockSpec((d, d), lambda i: (0, 0)),
            pl.BlockSpec((ROW_TILE, d), lambda i: (jnp.minimum(i, npt - 1), 0)),
            pl.BlockSpec((ROW_TILE, d), lambda i: (0, 0)),
        ],
        out_specs=pl.BlockSpec((ROW_TILE, d), lambda i: (i, 0)),
        out_shape=jax.ShapeDtypeStruct((t, d), F32),
        compiler_params=pltpu.CompilerParams(
            dimension_semantics=("arbitrary",), vmem_limit_bytes=_vmem_limit(blocks)),
        name="out_proj_residual",
    )(m, w, xp, xs)


MLP_FF_TILE = 1024


def _mlp_kernel(x1_ref, nw_ref, wup_ref, wdn_ref, fw_ref, yp_ref, ys_ref, h2_ref, acc_ref,
                *, n_prompt_tiles, n_ff):
    i = pl.program_id(0)
    f = pl.program_id(1)

    @pl.when(f == 0)
    def _():
        h2_ref[...] = _rms(x1_ref[...], nw_ref[...]).astype(BF16)
        acc_ref[...] = jnp.zeros_like(acc_ref)

    hid = jnp.dot(h2_ref[...], wup_ref[...], preferred_element_type=F32)
    hid = jnp.square(jnp.maximum(hid, 0.0))
    acc_ref[...] += jnp.dot(hid.astype(BF16), wdn_ref[...], preferred_element_type=F32)

    @pl.when(f == n_ff - 1)
    def _():
        out = _rms(x1_ref[...] + acc_ref[...], fw_ref[...])

        @pl.when(i < n_prompt_tiles)
        def _():
            yp_ref[...] = out

        @pl.when(i >= n_prompt_tiles)
        def _():
            ys_ref[...] = out


def _mlp_call(x1, nw, wup, wdn, fw, *, n_prompt_rows):
    t, d = x1.shape
    npt = n_prompt_rows // ROW_TILE
    n_ff = D_FF // MLP_FF_TILE
    blocks = [_nbytes((ROW_TILE, d), F32), _nbytes((d, MLP_FF_TILE), BF16),
              _nbytes((MLP_FF_TILE, d), BF16), _nbytes((ROW_TILE, d), F32) * 2]
    scratch_bytes = _nbytes((ROW_TILE, d), BF16) + _nbytes((ROW_TILE, d), F32)
    return pl.pallas_call(
        functools.partial(_mlp_kernel, n_prompt_tiles=npt, n_ff=n_ff),
        grid=(t // ROW_TILE, n_ff),
        in_specs=[
            pl.BlockSpec((ROW_TILE, d), lambda i, f: (i, 0)),
            pl.BlockSpec((1, d), lambda i, f: (0, 0)),
            pl.BlockSpec((d, MLP_FF_TILE), lambda i, f: (0, f)),
            pl.BlockSpec((MLP_FF_TILE, d), lambda i, f: (f, 0)),
            pl.BlockSpec((1, d), lambda i, f: (0, 0)),
        ],
        out_specs=[
            pl.BlockSpec((ROW_TILE, d), lambda i, f: (jnp.minimum(i, npt - 1), 0)),
            pl.BlockSpec((ROW_TILE, d), lambda i, f: (0, 0)),
        ],
        out_shape=[
            jax.ShapeDtypeStruct((n_prompt_rows, d), F32),
            jax.ShapeDtypeStruct((t - n_prompt_rows, d), F32),
        ],
        scratch_shapes=[pltpu.VMEM((ROW_TILE, d), BF16), pltpu.VMEM((ROW_TILE, d), F32)],
        compiler_params=pltpu.CompilerParams(
            dimension_semantics=("arbitrary", "arbitrary"),
            vmem_limit_bytes=_vmem_limit(blocks, scratch_bytes)),
        name="mlp_final_norm",
    )(x1, nw, wup, wdn, fw)


def kernel(x_prompt, x_sample, cache_conv, cache_ssm_conv, state_ssm, mix_norm_w, w_in, conv_dw_w,
           conv_dw_b, conv_ln_w, conv_ln_b, w_conv_out, ssm_conv_w, ssm_conv_b, dt_bias, a_log, d_skip,
           ssm_norm_w, w_ssm_out, w_out, mlp_norm_w, w_up, w_down, final_norm_w):
    bp, lp, d = x_prompt.shape
    bs, ls, _ = x_sample.shape
    tp, ts = bp * lp, bs * ls
    xp = x_prompt.reshape(tp, d)
    xs = x_sample.reshape(ts, d)

    wi = w_in[0]
    o_z = 2 * CONV_DIM
    o_xbc = o_z + SSM_INNER
    o_dt = o_xbc + SSM_XBC_DIM
    o_g = o_dt + SSM_HEADS
    w_val = wi[:, 0:CONV_DIM].astype(BF16)
    w_gate = wi[:, CONV_DIM:o_z].astype(BF16)
    w_z = wi[:, o_z:o_xbc].astype(BF16)
    w_xbc = wi[:, o_xbc:o_dt].astype(BF16)
    w_dt = jnp.pad(wi[:, o_dt:o_g], ((0, 0), (0, V7X_LANES - SSM_HEADS))).astype(BF16)
    w_g = wi[:, o_g:].astype(BF16)
    dt_b = jnp.pad(dt_bias[0], (0, V7X_LANES - SSM_HEADS)).reshape(1, V7X_LANES)
    alog = jnp.pad(a_log[0], (0, V7X_LANES - SSM_HEADS)).reshape(1, V7X_LANES)
    dskip = jnp.repeat(d_skip[0], SSM_HEAD_DIM).reshape(1, SSM_INNER)
    row = lambda v: v.reshape(1, -1)

    h = _norm_call(xp, xs, row(mix_norm_w[0]))

    tm, tn = MM_ROW_TILE, 1024
    a = _mm_call(h, [w_val, w_gate], _ep_glu, BF16, tm=tm, tn=tn // 2, name="in_proj_glu")
    z = _mm_call(h, [w_z], _ep_identity, BF16, tm=tm, tn=tn, name="in_proj_z")
    xbc = _mm_call(h, [w_xbc], _ep_identity, BF16, tm=tm, tn=tn, name="in_proj_xbc")
    dt = _mm_call(h, [w_dt], _ep_softplus_bias, F32, tm=tm, tn=V7X_LANES, params=[dt_b], name="in_proj_dt")
    gates = _mm_call(h, [w_g], _ep_sigmoid, BF16, tm=tm, tn=tn, name="in_proj_gates")

    conf_args = (conv_dw_w[0], row(conv_dw_b[0]), row(conv_ln_w[0]), row(conv_ln_b[0]),
                 w_conv_out[0].astype(BF16))
    m_a, conv_p = _conf_call(a, gates, None, None, *conf_args, row0=0, n_seq=bp, seq_len=lp, tm=ROW_TILE,
                             name="conformer_prompt")
    m_a, conv_s = _conf_call(a, gates, cache_conv[0], m_a, *conf_args, row0=tp, n_seq=bs, seq_len=ls, tm=ls,
                             name="conformer_sample")

    ssd_args = (ssm_conv_w[0], row(ssm_conv_b[0]), alog, dskip, row(ssm_norm_w[0]))
    yn, xbc_p, ssm_p = _ssd_call(xbc, dt, z, None, None, None, *ssd_args, row0=0, n_seq=bp, seq_len=lp,
                                 ql=SSD_CHUNK, name="ssd_prompt")
    h0 = state_ssm[0].reshape(bs, SSM_INNER, SSM_STATE)
    yn, xbc_s, ssm_s = _ssd_call(xbc, dt, z, cache_ssm_conv[0], h0, yn, *ssd_args, row0=tp, n_seq=bs,
                                 seq_len=ls, ql=ls, name="ssd_sample")

    m = _mm_call(yn, [w_ssm_out[0].astype(BF16)], _ep_merge, BF16, tm=tm, tn=tn // 2,
                 extras=[(m_a, 0), (gates, D_MODEL)], name="ssm_out_merge")
    x1 = _outproj_call(m, w_out[0].astype(BF16), xp, xs)
    y_p, y_s = _mlp_call(x1, row(mlp_norm_w[0]), w_up[0].astype(BF16), w_down[0].astype(BF16),
                         row(final_norm_w), n_prompt_rows=tp)

    hshape = (SSM_HEADS, SSM_HEAD_DIM, SSM_STATE)
    return (y_p.reshape(bp, lp, d), y_s.reshape(bs, ls, d),
            conv_p[None], xbc_p[None], ssm_p.reshape(1, bp, *hshape),
            conv_s[None], xbc_s[None], ssm_s.reshape(1, bs, *hshape))
```

```python
import functools

import jax
import jax.numpy as jnp
from jax import lax
from jax.experimental import pallas as pl
from jax.experimental.pallas import tpu as pltpu

F32 = jnp.float32
BF16 = jnp.bfloat16

D_MODEL = 2048
CONV_DIM = D_MODEL
CONV_WIDTH = 31
SSM_INNER = 2 * D_MODEL
SSM_HEAD_DIM = 64
SSM_HEADS = SSM_INNER // SSM_HEAD_DIM
SSM_GROUPS = 8
HEADS_PER_GROUP = SSM_HEADS // SSM_GROUPS
SSM_STATE = 128
SSM_CONV_WIDTH = 4
SSM_XBC_DIM = SSM_INNER + 2 * SSM_GROUPS * SSM_STATE
D_FF = 4 * D_MODEL
EPS = 1e-5

V7X_LANES = 128
V7X_VMEM_BYTES = 64 * 1024 * 1024
V7X_VMEM_REQUEST_CAP = 56 * 1024 * 1024
COMPILER_TEMP_BYTES = 12 * 1024 * 1024

SSD_CHUNK = 128
CONV_HALO = 32
SSM_HALO = 16
ROW_TILE = 512
MM_ROW_TILE = 1536


def _vmem_limit(block_bytes, scratch_bytes=0):
    need = 2 * sum(block_bytes) + scratch_bytes + COMPILER_TEMP_BYTES
    return int(min(need, V7X_VMEM_REQUEST_CAP))


def _nbytes(shape, dtype):
    n = 1
    for s in shape:
        n *= s
    return n * jnp.dtype(dtype).itemsize


def _sigmoid(x):
    return 1.0 / (1.0 + jnp.exp(-x))


def _softplus(x):
    return jnp.maximum(x, 0.0) + jnp.log1p(jnp.exp(-jnp.abs(x)))


def _rms(x, w):
    ms = jnp.mean(x * x, axis=-1, keepdims=True)
    return x * lax.rsqrt(ms + EPS) * w


def _norm_kernel(xp_ref, xs_ref, w_ref, o_ref, *, n_prompt_tiles):
    i = pl.program_id(0)

    @pl.when(i < n_prompt_tiles)
    def _():
        o_ref[...] = _rms(xp_ref[...], w_ref[...]).astype(o_ref.dtype)

    @pl.when(i >= n_prompt_tiles)
    def _():
        o_ref[...] = _rms(xs_ref[...], w_ref[...]).astype(o_ref.dtype)


def _norm_call(xp, xs, w):
    tp, d = xp.shape
    ts = xs.shape[0]
    assert ts == ROW_TILE and tp % ROW_TILE == 0
    npt = tp // ROW_TILE
    blocks = [_nbytes((ROW_TILE, d), F32)] * 2 + [_nbytes((ROW_TILE, d), BF16)]
    return pl.pallas_call(
        functools.partial(_norm_kernel, n_prompt_tiles=npt),
        grid=(npt + 1,),
        in_specs=[
            pl.BlockSpec((ROW_TILE, d), lambda i: (jnp.minimum(i, npt - 1), 0)),
            pl.BlockSpec((ROW_TILE, d), lambda i: (0, 0)),
            pl.BlockSpec((1, d), lambda i: (0, 0)),
        ],
        out_specs=pl.BlockSpec((ROW_TILE, d), lambda i: (i, 0)),
        out_shape=jax.ShapeDtypeStruct((tp + ts, d), BF16),
        compiler_params=pltpu.CompilerParams(
            dimension_semantics=("arbitrary",), vmem_limit_bytes=_vmem_limit(blocks)),
        name="mix_rmsnorm",
    )(xp, xs, w)


def _mm_kernel(*refs, n_w, n_ex, n_p, epilogue):
    lhs_ref = refs[0]
    w_refs = refs[1:1 + n_w]
    ex_refs = refs[1 + n_w:1 + n_w + n_ex]
    p_refs = refs[1 + n_w + n_ex:1 + n_w + n_ex + n_p]
    o_ref = refs[-1]
    lhs = lhs_ref[...]
    accs = [jnp.dot(lhs, w[...], preferred_element_type=F32) for w in w_refs]
    out = epilogue(accs, [e[...].astype(F32) for e in ex_refs], [p[...] for p in p_refs])
    o_ref[...] = out.astype(o_ref.dtype)


def _mm_call(lhs, weights, epilogue, out_dtype, *, tm, tn, extras=(), params=(), name):
    t, k = lhs.shape
    n = weights[0].shape[1]
    assert t % tm == 0 and n % tn == 0
    in_specs = [pl.BlockSpec((tm, k), lambda i, j: (i, 0))]
    in_specs += [pl.BlockSpec((k, tn), lambda i, j: (0, j)) for _ in weights]
    operands = [lhs, *weights]
    blocks = [_nbytes((tm, k), lhs.dtype)] + [_nbytes((k, tn), w.dtype) for w in weights]
    for arr, col0 in extras:
        assert col0 % tn == 0
        in_specs.append(pl.BlockSpec((tm, tn), lambda i, j, o=col0 // tn: (i, j + o)))
        operands.append(arr)
        blocks.append(_nbytes((tm, tn), arr.dtype))
    for p in params:
        in_specs.append(pl.BlockSpec((1, tn), lambda i, j: (0, j)))
        operands.append(p)
        blocks.append(_nbytes((8, tn), p.dtype))
    blocks.append(_nbytes((tm, tn), out_dtype))
    return pl.pallas_call(
        functools.partial(_mm_kernel, n_w=len(weights), n_ex=len(extras), n_p=len(params),
                          epilogue=epilogue),
        grid=(t // tm, n // tn),
        in_specs=in_specs,
        out_specs=pl.BlockSpec((tm, tn), lambda i, j: (i, j)),
        out_shape=jax.ShapeDtypeStruct((t, n), out_dtype),
        compiler_params=pltpu.CompilerParams(
            dimension_semantics=("arbitrary", "arbitrary"), vmem_limit_bytes=_vmem_limit(blocks)),
        name=name,
    )(*operands)


def _ep_glu(accs, extras, params):
    return accs[0] * _sigmoid(accs[1])


def _ep_identity(accs, extras, params):
    return accs[0]


def _ep_sigmoid(accs, extras, params):
    return _sigmoid(accs[0])


def _ep_softplus_bias(accs, extras, params):
    return _softplus(accs[0] + params[0])


def _ep_merge(accs, extras, params):
    return extras[0] + extras[1] * accs[0]


XBC_ROW_TILE = 1024
XBC_COL_TILE = 2048
XBC_SUB_COLS = 256
SUBLANES = 8


def _xbc_conv_kernel(h_ref, w_ref, cw_ref, cb_ref, act_ref, tail_ref, carry_ref, stage_ref, *, tiles_per_seq):
    i = pl.program_id(0)
    j = pl.program_id(1)
    tm = h_ref.shape[0]
    hist = SSM_CONV_WIDTH - 1

    @pl.when((i % tiles_per_seq) == 0)
    def _():
        tile_cols = pl.ds(pl.multiple_of(j * XBC_COL_TILE, XBC_COL_TILE), XBC_COL_TILE)
        carry_ref[:, tile_cols] = jnp.zeros((SUBLANES, XBC_COL_TILE), F32)

    lhs = h_ref[...]
    for n in range(XBC_COL_TILE // XBC_SUB_COLS):
        sub = slice(n * XBC_SUB_COLS, (n + 1) * XBC_SUB_COLS)
        gcols = pl.ds(pl.multiple_of(j * XBC_COL_TILE + n * XBC_SUB_COLS, XBC_SUB_COLS), XBC_SUB_COLS)
        st = stage_ref.at[n % 2]
        st[SUBLANES:SUBLANES + tm, :] = jnp.dot(lhs, w_ref[:, sub], preferred_element_type=F32)
        st[0:SUBLANES, :] = carry_ref[:, gcols]
        last = st[tm:tm + SUBLANES, :]
        carry_ref[:, gcols] = last
        tail_ref[0, :, sub] = last
        conv = cb_ref[:, sub] + cw_ref[hist:hist + 1, sub] * st[SUBLANES:SUBLANES + tm, :]
        for k in range(hist):
            conv = conv + cw_ref[k:k + 1, sub] * st[SUBLANES - hist + k:SUBLANES - hist + k + tm, :]
        act_ref[:, sub] = (conv * _sigmoid(conv)).astype(act_ref.dtype)


def _xbc_conv_call(h, w, cw, cb, *, n_seq, seq_len):
    k = h.shape[1]
    n = w.shape[1]
    tm, tn = XBC_ROW_TILE, XBC_COL_TILE
    assert seq_len % tm == 0 and n % tn == 0
    tiles_per_seq = seq_len // tm
    stage_shape = (2, tm + SUBLANES, XBC_SUB_COLS)
    blocks = [_nbytes((tm, k), BF16), _nbytes((k, tn), BF16), _nbytes((8, tn), F32) * 3,
              _nbytes((tm, tn), BF16)]
    return pl.pallas_call(
        functools.partial(_xbc_conv_kernel, tiles_per_seq=tiles_per_seq),
        grid=(n_seq * tiles_per_seq, n // tn),
        in_specs=[
            pl.BlockSpec((tm, k), lambda i, j: (i, 0)),
            pl.BlockSpec((k, tn), lambda i, j: (0, j)),
            pl.BlockSpec((SSM_CONV_WIDTH, tn), lambda i, j: (0, j)),
            pl.BlockSpec((1, tn), lambda i, j: (0, j)),
        ],
        out_specs=[
            pl.BlockSpec((tm, tn), lambda i, j: (i, j)),
            pl.BlockSpec((1, SUBLANES, tn), lambda i, j: (i, 0, j)),
        ],
        out_shape=[
            jax.ShapeDtypeStruct((n_seq * seq_len, n), BF16),
            jax.ShapeDtypeStruct((n_seq * tiles_per_seq, SUBLANES, n), F32),
        ],
        scratch_shapes=[pltpu.VMEM((SUBLANES, n), F32), pltpu.VMEM(stage_shape, F32)],
        compiler_params=pltpu.CompilerParams(
            dimension_semantics=("arbitrary", "arbitrary"),
            vmem_limit_bytes=_vmem_limit(blocks, _nbytes((SUBLANES, n), F32) + _nbytes(stage_shape, F32))),
        name="in_proj_xbc_conv",
    )(h, w, cw, cb)


CONV_ROW_BLOCK = 64
CONV_COL_BLOCK = 128
LN_ROW_BLOCK = 256


def _conf_kernel(*refs, tm, n_tiles, halo_from_cache, aliased):
    (a_ref, halo_ref, g_ref, dww_ref, dwb_ref, lnw_ref, lnb_ref, w_ref) = refs[:8]
    o_ref, cache_ref, xs_ref, zs_ref, y_ref, s_ref = refs[8 + int(aliased):]
    i = pl.program_id(1)
    hist = CONV_WIDTH - 1
    if halo_from_cache:
        xs_ref[0:CONV_HALO - hist, :] = jnp.zeros((CONV_HALO - hist, CONV_DIM), F32)
        xs_ref[CONV_HALO - hist:CONV_HALO, :] = halo_ref[0]
    else:
        @pl.when(i == 0)
        def _():
            xs_ref[0:CONV_HALO, :] = jnp.zeros((CONV_HALO, CONV_DIM), F32)

        @pl.when(i > 0)
        def _():
            xs_ref[0:CONV_HALO, :] = halo_ref[...].astype(F32)
    xs_ref[CONV_HALO:CONV_HALO + tm, :] = a_ref[...].astype(F32)

    base = CONV_HALO - hist
    rb = min(CONV_ROW_BLOCK, tm)

    def conv_cols(cb, carry):
        cols = pl.ds(pl.multiple_of(cb * CONV_COL_BLOCK, CONV_COL_BLOCK), CONV_COL_BLOCK)
        for s in range(SUBLANES):
            rows = tm + SUBLANES * ((hist - s) // SUBLANES)
            zs_ref[s, 0:rows, :] = xs_ref[base + s:base + s + rows, cols]
        bias = jnp.broadcast_to(dwb_ref[:, cols], (rb, CONV_COL_BLOCK))
        for r0 in range(0, tm, rb):
            acc = bias
            for k in range(CONV_WIDTH):
                s, q = k % SUBLANES, k // SUBLANES
                acc = acc + dww_ref[k:k + 1, cols] * zs_ref[s, r0 + SUBLANES * q:r0 + SUBLANES * q + rb, :]
            y_ref[r0:r0 + rb, cols] = acc
        return carry

    lax.fori_loop(0, CONV_DIM // CONV_COL_BLOCK, conv_cols, 0)

    ln_rows_per_step = min(LN_ROW_BLOCK, tm)

    def ln_rows(rb, carry):
        r0 = pl.multiple_of(rb * ln_rows_per_step, ln_rows_per_step)
        y = y_ref[pl.ds(r0, ln_rows_per_step), :]
        mu = jnp.mean(y, axis=-1, keepdims=True)
        yc = y - mu
        var = jnp.mean(yc * yc, axis=-1, keepdims=True)
        yn = yc * lax.rsqrt(var + EPS) * lnw_ref[...] + lnb_ref[...]
        s_ref[pl.ds(r0, ln_rows_per_step), :] = (yn * _sigmoid(yn)).astype(BF16)
        return carry

    lax.fori_loop(0, tm // ln_rows_per_step, ln_rows, 0)

    ya = jnp.dot(s_ref[...], w_ref[...], preferred_element_type=F32)
    o_ref[...] = (g_ref[...].astype(F32) * ya).astype(o_ref.dtype)

    @pl.when(i == n_tiles - 1)
    def _():
        cache_ref[0] = xs_ref[CONV_HALO + tm - hist:CONV_HALO + tm, :]


def _conf_call(a, gates, cache, prev, dww, dwb, lnw, lnb, w, *, row0, n_seq, seq_len, tm, name):
    assert seq_len % tm == 0 and row0 % tm == 0 and tm % CONV_HALO == 0
    t = a.shape[0]
    n_tiles = seq_len // tm
    halo_from_cache = cache is not None
    hist = CONV_WIDTH - 1
    blk0 = row0 // tm

    def row_blk(b, i):
        return blk0 + b * n_tiles + i

    if halo_from_cache:
        assert n_tiles == 1
        halo_arr = cache
        halo_spec = pl.BlockSpec((1, hist, CONV_DIM), lambda b, i: (b, 0, 0))
        halo_bytes = _nbytes((32, CONV_DIM), F32)
    else:
        per = tm // CONV_HALO
        halo_arr = a
        halo_spec = pl.BlockSpec(
            (CONV_HALO, CONV_DIM), lambda b, i: (jnp.maximum(row_blk(b, i) * per - 1, 0), 0))
        halo_bytes = _nbytes((CONV_HALO, CONV_DIM), BF16)
    blocks = [_nbytes((tm, CONV_DIM), BF16), halo_bytes, _nbytes((tm, D_MODEL), BF16),
              _nbytes((32, CONV_DIM), F32), _nbytes((8, CONV_DIM), F32) * 3,
              _nbytes((CONV_DIM, D_MODEL), BF16), _nbytes((tm, D_MODEL), BF16),
              _nbytes((32, CONV_DIM), F32)]
    scratch_dims = [((CONV_HALO + tm, CONV_DIM), F32),
                    ((SUBLANES, CONV_HALO + tm, CONV_COL_BLOCK), F32),
                    ((tm, CONV_DIM), F32), ((tm, CONV_DIM), BF16)]
    in_specs = [
        pl.BlockSpec((tm, CONV_DIM), lambda b, i: (row_blk(b, i), 0)),
        halo_spec,
        pl.BlockSpec((tm, D_MODEL), lambda b, i: (row_blk(b, i), 0)),
        pl.BlockSpec((CONV_WIDTH, CONV_DIM), lambda b, i: (0, 0)),
        pl.BlockSpec((1, CONV_DIM), lambda b, i: (0, 0)),
        pl.BlockSpec((1, CONV_DIM), lambda b, i: (0, 0)),
        pl.BlockSpec((1, CONV_DIM), lambda b, i: (0, 0)),
        pl.BlockSpec((CONV_DIM, D_MODEL), lambda b, i: (0, 0)),
    ]
    operands = [a, halo_arr, gates, dww, dwb, lnw, lnb, w]
    aliases = {}
    if prev is not None:
        in_specs.append(pl.BlockSpec(memory_space=pl.ANY))
        operands.append(prev)
        aliases = {len(operands) - 1: 0}
    return pl.pallas_call(
        functools.partial(_conf_kernel, tm=tm, n_tiles=n_tiles, halo_from_cache=halo_from_cache,
                          aliased=prev is not None),
        grid=(n_seq, n_tiles),
        in_specs=in_specs,
        out_specs=[
            pl.BlockSpec((tm, D_MODEL), lambda b, i: (row_blk(b, i), 0)),
            pl.BlockSpec((1, hist, CONV_DIM), lambda b, i: (b, 0, 0)),
        ],
        out_shape=[
            jax.ShapeDtypeStruct((t, D_MODEL), BF16),
            jax.ShapeDtypeStruct((n_seq, hist, CONV_DIM), F32),
        ],
        scratch_shapes=[pltpu.VMEM(shape, dt) for shape, dt in scratch_dims],
        input_output_aliases=aliases,
        compiler_params=pltpu.CompilerParams(
            dimension_semantics=("arbitrary", "arbitrary"),
            vmem_limit_bytes=_vmem_limit(blocks, sum(_nbytes(sh, dt) for sh, dt in scratch_dims))),
        name=name,
    )(*operands)


SSM_CONV_COL_BLOCK = 512
GROUP_COLS = HEADS_PER_GROUP * SSM_HEAD_DIM
B_COL0 = SSM_INNER
C_COL0 = SSM_INNER + SSM_GROUPS * SSM_STATE
NEG_BIG = -1e30
LOG2_E = 1.4426950408889634


def _split3(x):
    p1 = x.astype(BF16)
    r1 = x - p1.astype(F32)
    p2 = r1.astype(BF16)
    r2 = r1 - p2.astype(F32)
    return p1, p2, r2.astype(BF16)


def _ssd_kernel(*refs, ql, n_chunks, has_h0, conv_in_kernel, aliased):
    it = iter(refs)
    src_ref = next(it)
    if conv_in_kernel:
        hist_ref = next(it)
        cw_ref = next(it)
        cb_ref = next(it)
    dt_ref = next(it)
    z_ref = next(it)
    alog_ref = next(it)
    dskip_ref = next(it)
    nw_ref = next(it)
    h0_ref = next(it) if has_h0 else None
    if aliased:
        next(it)
    y_ref = next(it)
    cache_ref = next(it) if conv_in_kernel else None
    hout_ref = next(it)
    xs_ref = next(it) if conv_in_kernel else None
    act_ref = next(it)
    acs_ref = next(it)
    rowt_ref = next(it)
    wendt_ref = next(it)
    ht_ref = next(it)

    q = SSD_CHUNK
    c_idx = pl.program_id(1)
    hist = SSM_CONV_WIDTH - 1

    @pl.when(c_idx == 0)
    def _():
        for g in range(SSM_GROUPS):
            if has_h0:
                ht_ref[g] = h0_ref[0, g * GROUP_COLS:(g + 1) * GROUP_COLS, :].T
            else:
                ht_ref[g] = jnp.zeros((SSM_STATE, GROUP_COLS), F32)

    if conv_in_kernel:
        assert n_chunks == 1
        xs_ref[0:SSM_HALO - hist, :] = jnp.zeros((SSM_HALO - hist, SSM_XBC_DIM), F32)
        xs_ref[SSM_HALO - hist:SSM_HALO, :] = hist_ref[0]
        xs_ref[SSM_HALO:SSM_HALO + ql, :] = src_ref[...].astype(F32)
        if ql < q:
            xs_ref[SSM_HALO + ql:SSM_HALO + q, :] = jnp.zeros((q - ql, SSM_XBC_DIM), F32)
        base = SSM_HALO - hist
        row_ok = lax.broadcasted_iota(jnp.int32, (q, SSM_CONV_COL_BLOCK), 0) < ql

        def conv_cols(cb, carry):
            c0 = pl.multiple_of(cb * SSM_CONV_COL_BLOCK, SSM_CONV_COL_BLOCK)
            cols = pl.ds(c0, SSM_CONV_COL_BLOCK)
            acc = jnp.broadcast_to(cb_ref[:, cols], (q, SSM_CONV_COL_BLOCK))
            for k in range(SSM_CONV_WIDTH):
                acc = acc + cw_ref[k:k + 1, cols] * xs_ref[base + k:base + k + q, cols]
            act = acc * _sigmoid(acc)
            if ql < q:
                act = jnp.where(row_ok, act, 0.0)
            act_ref[:, cols] = act
            return carry

        lax.fori_loop(0, SSM_XBC_DIM // SSM_CONV_COL_BLOCK, conv_cols, 0)
    else:
        assert ql == q
        act_ref[...] = src_ref[...].astype(F32)

    dt = dt_ref[...]
    if ql < q:
        dt = jnp.concatenate([dt, jnp.zeros((q - ql, V7X_LANES), F32)], axis=0)
    a_neg = -jnp.exp(alog_ref[...])
    dta = dt * a_neg
    r_i = lax.broadcasted_iota(jnp.int32, (q, q), 0)
    c_i = lax.broadcasted_iota(jnp.int32, (q, q), 1)
    causal = r_i >= c_i
    tri = jnp.where(causal, 1.0, 0.0).astype(BF16)
    p1, p2, p3 = _split3(dta)
    acs = (jnp.dot(tri, p1, preferred_element_type=F32) + jnp.dot(tri, p2, preferred_element_type=F32)
           + jnp.dot(tri, p3, preferred_element_type=F32))
    acs2 = acs * LOG2_E
    acs_ref[...] = acs2
    acs2_t = acs2.T
    rowt = acs2_t - jnp.log2(dt.T)
    rowt_ref[...] = rowt
    wendt_ref[...] = jnp.exp2(acs2_t[:, q - 1:q] - rowt)

    left = c_i < SSM_HEAD_DIM
    left_row = left[0:1, :]

    for g in range(SSM_GROUPS):
        bg = act_ref[:, B_COL0 + g * SSM_STATE:B_COL0 + (g + 1) * SSM_STATE]
        cg = act_ref[:, C_COL0 + g * SSM_STATE:C_COL0 + (g + 1) * SSM_STATE].astype(BF16)
        cbm = lax.dot_general(cg, bg.astype(BF16), (((1,), (1,)), ((), ())),
                              preferred_element_type=F32)
        bg_t = bg.T
        ht_g = ht_ref[g]
        yoff = jnp.dot(cg, ht_g.astype(BF16), preferred_element_type=F32)
        y_parts, inc_parts, dec_parts = [], [], []
        for j in range(HEADS_PER_GROUP // 2):
            h_a = g * HEADS_PER_GROUP + 2 * j
            h_b = h_a + 1
            lo = h_a * SSM_HEAD_DIM
            xp = act_ref[:, lo:lo + V7X_LANES]
            lhs_parts = []
            cols = []
            for h in (h_a, h_b):
                col = jnp.broadcast_to(acs_ref[:, h:h + 1], (q, q))
                cols.append(col)
                seg = jnp.where(causal, col - rowt_ref[h:h + 1, :], NEG_BIG)
                m_h = cbm * jnp.exp2(seg)
                w_h = bg_t * wendt_ref[h:h + 1, :]
                lhs_parts.append(jnp.concatenate([m_h, w_h], axis=0).astype(BF16))
            rhs = jnp.concatenate([jnp.where(left, xp, 0.0).astype(BF16),
                                   jnp.where(left, 0.0, xp).astype(BF16)], axis=0)
            res = jnp.dot(jnp.concatenate(lhs_parts, axis=1), rhs, preferred_element_type=F32)
            colsel = jnp.where(left, cols[0], cols[1])
            y_parts.append(res[0:q] + jnp.exp2(colsel) * yoff[:, j * V7X_LANES:(j + 1) * V7X_LANES]
                           + dskip_ref[:, lo:lo + V7X_LANES] * xp)
            inc_parts.append(res[q:2 * q])
            last_a = jnp.broadcast_to(acs_ref[q - 1:q, h_a:h_a + 1], (1, V7X_LANES))
            last_b = jnp.broadcast_to(acs_ref[q - 1:q, h_b:h_b + 1], (1, V7X_LANES))
            dec_parts.append(jnp.exp2(jnp.where(left_row, last_a, last_b)))
        ht_ref[g] = jnp.concatenate(dec_parts, axis=1) * ht_g + jnp.concatenate(inc_parts, axis=1)
        gc = slice(g * GROUP_COLS, (g + 1) * GROUP_COLS)
        yg = jnp.concatenate(y_parts, axis=1)[0:ql]
        zg = z_ref[:, gc].astype(F32)
        yz = yg * (zg * _sigmoid(zg))
        ms = jnp.mean(yz * yz, axis=-1, keepdims=True)
        y_ref[:, gc] = (yz * lax.rsqrt(ms + EPS) * nw_ref[:, gc]).astype(y_ref.dtype)

    @pl.when(c_idx == n_chunks - 1)
    def _():
        if conv_in_kernel:
            cache_ref[0] = xs_ref[SSM_HALO + ql - hist:SSM_HALO + ql, :]
        for g in range(SSM_GROUPS):
            hout_ref[0, g * GROUP_COLS:(g + 1) * GROUP_COLS, :] = ht_ref[g].T


def _ssd_call(src, dt, z, conv, h0, prev, alog, dskip, nw, *, row0, n_seq, seq_len, ql, name):
    assert seq_len % ql == 0 and row0 % ql == 0 and ql % SSM_HALO == 0 and ql <= SSD_CHUNK
    assert src.shape[0] == n_seq * seq_len
    n_chunks = seq_len // ql
    conv_in_kernel = conv is not None
    has_h0 = h0 is not None
    hist = SSM_CONV_WIDTH - 1
    blk0 = row0 // ql
    q = SSD_CHUNK

    def row_blk(b, c):
        return blk0 + b * n_chunks + c

    in_specs = [pl.BlockSpec((ql, SSM_XBC_DIM), lambda b, c: (b * n_chunks + c, 0))]
    operands = [src]
    if conv_in_kernel:
        in_specs += [
            pl.BlockSpec((1, hist, SSM_XBC_DIM), lambda b, c: (b, 0, 0)),
            pl.BlockSpec((SSM_CONV_WIDTH, SSM_XBC_DIM), lambda b, c: (0, 0)),
            pl.BlockSpec((1, SSM_XBC_DIM), lambda b, c: (0, 0)),
        ]
        operands += list(conv)
    in_specs += [
        pl.BlockSpec((ql, V7X_LANES), lambda b, c: (row_blk(b, c), 0)),
        pl.BlockSpec((ql, SSM_INNER), lambda b, c: (row_blk(b, c), 0)),
        pl.BlockSpec((1, V7X_LANES), lambda b, c: (0, 0)),
        pl.BlockSpec((1, SSM_INNER), lambda b, c: (0, 0)),
        pl.BlockSpec((1, SSM_INNER), lambda b, c: (0, 0)),
    ]
    operands += [dt, z, alog, dskip, nw]
    if has_h0:
        in_specs.append(pl.BlockSpec((1, SSM_INNER, SSM_STATE), lambda b, c: (b, 0, 0)))
        operands.append(h0)
    aliases = {}
    if prev is not None:
        in_specs.append(pl.BlockSpec(memory_space=pl.ANY))
        operands.append(prev)
        aliases = {len(operands) - 1: 0}
    state_bytes = _nbytes((SSM_INNER, SSM_STATE), F32)
    blocks = [_nbytes((ql, SSM_XBC_DIM), BF16), _nbytes((SSM_HALO, SSM_XBC_DIM), F32),
              _nbytes((ql, V7X_LANES), F32), _nbytes((ql, SSM_INNER), BF16),
              _nbytes((8, SSM_XBC_DIM), F32) * 2, _nbytes((8, SSM_INNER), F32) * 2,
              _nbytes((ql, SSM_INNER), BF16), _nbytes((8, SSM_XBC_DIM), F32),
              state_bytes * (2 if has_h0 else 1)]
    scratch_dims = ([((SSM_HALO + q, SSM_XBC_DIM), F32)] if conv_in_kernel else []) + [
        ((q, SSM_XBC_DIM), F32),
        ((q, V7X_LANES), F32),
        ((V7X_LANES, q), F32),
        ((V7X_LANES, q), F32),
        ((SSM_GROUPS, SSM_STATE, GROUP_COLS), F32),
    ]
    out_specs = [pl.BlockSpec((ql, SSM_INNER), lambda b, c: (row_blk(b, c), 0))]
    out_shape = [jax.ShapeDtypeStruct((dt.shape[0], SSM_INNER), BF16)]
    if conv_in_kernel:
        out_specs.append(pl.BlockSpec((1, hist, SSM_XBC_DIM), lambda b, c: (b, 0, 0)))
        out_shape.append(jax.ShapeDtypeStruct((n_seq, hist, SSM_XBC_DIM), F32))
    out_specs.append(pl.BlockSpec((1, SSM_INNER, SSM_STATE), lambda b, c: (b, 0, 0)))
    out_shape.append(jax.ShapeDtypeStruct((n_seq, SSM_INNER, SSM_STATE), F32))
    return pl.pallas_call(
        functools.partial(_ssd_kernel, ql=ql, n_chunks=n_chunks, has_h0=has_h0,
                          conv_in_kernel=conv_in_kernel, aliased=prev is not None),
        grid=(n_seq, n_chunks),
        in_specs=in_specs,
        out_specs=out_specs,
        out_shape=out_shape,
        scratch_shapes=[pltpu.VMEM(shape, dt_) for shape, dt_ in scratch_dims],
        input_output_aliases=aliases,
        compiler_params=pltpu.CompilerParams(
            dimension_semantics=("arbitrary", "arbitrary"),
            vmem_limit_bytes=_vmem_limit(blocks, sum(_nbytes(sh, dt_) for sh, dt_ in scratch_dims))),
        name=name,
    )(*operands)


def _outproj_kernel(m_ref, w_ref, xp_ref, xs_ref, o_ref, *, n_prompt_tiles):
    i = pl.program_id(0)
    acc = jnp.dot(m_ref[...], w_ref[...], preferred_element_type=F32)

    @pl.when(i < n_prompt_tiles)
    def _():
        o_ref[...] = xp_ref[...] + acc

    @pl.when(i >= n_prompt_tiles)
    def _():
        o_ref[...] = xs_ref[...] + acc


def _outproj_call(m, w, xp, xs):
    t, d = m.shape
    npt = xp.shape[0] // ROW_TILE
    blocks = [_nbytes((ROW_TILE, d), BF16), _nbytes((d, d), BF16)] + [_nbytes((ROW_TILE, d), F32)] * 3
    return pl.pallas_call(
        functools.partial(_outproj_kernel, n_prompt_tiles=npt),
        grid=(t // ROW_TILE,),
        in_specs=[
            pl.BlockSpec((ROW_TILE, d), lambda i: (i, 0)),
            pl.BlockSpec((d, d), lambda i: (0, 0)),
            pl.BlockSpec((ROW_TILE, d), lambda i: (jnp.minimum(i, npt - 1), 0)),
            pl.BlockSpec((ROW_TILE, d), lambda i: (0, 0)),
        ],
        out_specs=pl.BlockSpec((ROW_TILE, d), lambda i: (i, 0)),
        out_shape=jax.ShapeDtypeStruct((t, d), F32),
        compiler_params=pltpu.CompilerParams(
            dimension_semantics=("arbitrary",), vmem_limit_bytes=_vmem_limit(blocks)),
        name="out_proj_residual",
    )(m, w, xp, xs)


MLP_FF_TILE = 1024


def _mlp_kernel(x1_ref, nw_ref, wup_ref, wdn_ref, fw_ref, yp_ref, ys_ref, h2_ref, acc_ref,
                *, n_prompt_tiles, n_ff):
    i = pl.program_id(0)
    f = pl.program_id(1)

    @pl.when(f == 0)
    def _():
        h2_ref[...] = _rms(x1_ref[...], nw_ref[...]).astype(BF16)
        acc_ref[...] = jnp.zeros_like(acc_ref)

    hid = jnp.dot(h2_ref[...], wup_ref[...], preferred_element_type=F32)
    hid = jnp.square(jnp.maximum(hid, 0.0))
    acc_ref[...] += jnp.dot(hid.astype(BF16), wdn_ref[...], preferred_element_type=F32)

    @pl.when(f == n_ff - 1)
    def _():
        out = _rms(x1_ref[...] + acc_ref[...], fw_ref[...])

        @pl.when(i < n_prompt_tiles)
        def _():
            yp_ref[...] = out

        @pl.when(i >= n_prompt_tiles)
        def _():
            ys_ref[...] = out


def _mlp_call(x1, nw, wup, wdn, fw, *, n_prompt_rows):
    t, d = x1.shape
    npt = n_prompt_rows // ROW_TILE
    n_ff = D_FF // MLP_FF_TILE
    blocks = [_nbytes((ROW_TILE, d), F32), _nbytes((d, MLP_FF_TILE), BF16),
              _nbytes((MLP_FF_TILE, d), BF16), _nbytes((ROW_TILE, d), F32) * 2]
    scratch_bytes = _nbytes((ROW_TILE, d), BF16) + _nbytes((ROW_TILE, d), F32)
    return pl.pallas_call(
        functools.partial(_mlp_kernel, n_prompt_tiles=npt, n_ff=n_ff),
        grid=(t // ROW_TILE, n_ff),
        in_specs=[
            pl.BlockSpec((ROW_TILE, d), lambda i, f: (i, 0)),
            pl.BlockSpec((1, d), lambda i, f: (0, 0)),
            pl.BlockSpec((d, MLP_FF_TILE), lambda i, f: (0, f)),
            pl.BlockSpec((MLP_FF_TILE, d), lambda i, f: (f, 0)),
            pl.BlockSpec((1, d), lambda i, f: (0, 0)),
        ],
        out_specs=[
            pl.BlockSpec((ROW_TILE, d), lambda i, f: (jnp.minimum(i, npt - 1), 0)),
            pl.BlockSpec((ROW_TILE, d), lambda i, f: (0, 0)),
        ],
        out_shape=[
            jax.ShapeDtypeStruct((n_prompt_rows, d), F32),
            jax.ShapeDtypeStruct((t - n_prompt_rows, d), F32),
        ],
        scratch_shapes=[pltpu.VMEM((ROW_TILE, d), BF16), pltpu.VMEM((ROW_TILE, d), F32)],
        compiler_params=pltpu.CompilerParams(
            dimension_semantics=("arbitrary", "arbitrary"),
            vmem_limit_bytes=_vmem_limit(blocks, scratch_bytes)),
        name="mlp_final_norm",
    )(x1, nw, wup, wdn, fw)


def kernel(x_prompt, x_sample, cache_conv, cache_ssm_conv, state_ssm, mix_norm_w, w_in, conv_dw_w,
           conv_dw_b, conv_ln_w, conv_ln_b, w_conv_out, ssm_conv_w, ssm_conv_b, dt_bias, a_log, d_skip,
           ssm_norm_w, w_ssm_out, w_out, mlp_norm_w, w_up, w_down, final_norm_w):
    bp, lp, d = x_prompt.shape
    bs, ls, _ = x_sample.shape
    tp, ts = bp * lp, bs * ls
    xp = x_prompt.reshape(tp, d)
    xs = x_sample.reshape(ts, d)

    wi = w_in[0]
    o_z = 2 * CONV_DIM
    o_xbc = o_z + SSM_INNER
    o_dt = o_xbc + SSM_XBC_DIM
    o_g = o_dt + SSM_HEADS
    w_val = wi[:, 0:CONV_DIM].astype(BF16)
    w_gate = wi[:, CONV_DIM:o_z].astype(BF16)
    w_z = wi[:, o_z:o_xbc].astype(BF16)
    w_xbc = wi[:, o_xbc:o_dt].astype(BF16)
    w_dt = jnp.pad(wi[:, o_dt:o_g], ((0, 0), (0, V7X_LANES - SSM_HEADS))).astype(BF16)
    w_g = wi[:, o_g:].astype(BF16)
    dt_b = jnp.pad(dt_bias[0], (0, V7X_LANES - SSM_HEADS)).reshape(1, V7X_LANES)
    alog = jnp.pad(a_log[0], (0, V7X_LANES - SSM_HEADS)).reshape(1, V7X_LANES)
    dskip = jnp.repeat(d_skip[0], SSM_HEAD_DIM).reshape(1, SSM_INNER)
    row = lambda v: v.reshape(1, -1)

    h = _norm_call(xp, xs, row(mix_norm_w[0]))

    tm, tn = MM_ROW_TILE, 1024
    a = _mm_call(h, [w_val, w_gate], _ep_glu, BF16, tm=tm, tn=tn // 2, name="in_proj_glu")
    z = _mm_call(h, [w_z], _ep_identity, BF16, tm=tm, tn=tn, name="in_proj_z")
    cw, cb = ssm_conv_w[0], row(ssm_conv_b[0])
    act_p, xbc_tail_p = _xbc_conv_call(h, w_xbc, cw, cb, n_seq=bp, seq_len=lp)
    xbc_s = _mm_call(h[tp:], [w_xbc], _ep_identity, BF16, tm=ts, tn=tn, name="in_proj_xbc_sample")
    dt =_mm_call(h, [w_dt], _ep_softplus_bias, F32, tm=tm, tn=V7X_LANES, params=[dt_b], name="in_proj_dt")
    gates = _mm_call(h, [w_g], _ep_sigmoid, BF16, tm=tm, tn=tn, name="in_proj_gates")

    conf_args = (conv_dw_w[0], row(conv_dw_b[0]), row(conv_ln_w[0]), row(conv_ln_b[0]),
                 w_conv_out[0].astype(BF16))
    m_a, conv_p = _conf_call(a, gates, None, None, *conf_args, row0=0, n_seq=bp, seq_len=lp, tm=ROW_TILE,
                             name="conformer_prompt")
    m_a, conv_s = _conf_call(a, gates, cache_conv[0], m_a, *conf_args, row0=tp, n_seq=bs, seq_len=ls, tm=ls,
                             name="conformer_sample")

    ssd_args = (alog, dskip, row(ssm_norm_w[0]))
    yn, ssm_p = _ssd_call(act_p, dt, z, None, None, None, *ssd_args, row0=0, n_seq=bp, seq_len=lp,
                          ql=SSD_CHUNK, name="ssd_prompt")
    tiles_per_seq = lp // XBC_ROW_TILE
    xbc_p = xbc_tail_p[tiles_per_seq - 1::tiles_per_seq, SUBLANES - (SSM_CONV_WIDTH - 1):, :]
    h0 = state_ssm[0].reshape(bs, SSM_INNER, SSM_STATE)
    yn, xbc_s, ssm_s = _ssd_call(xbc_s, dt, z, (cache_ssm_conv[0], cw, cb), h0, yn, *ssd_args, row0=tp,
                                 n_seq=bs, seq_len=ls, ql=ls, name="ssd_sample")

    m = _mm_call(yn, [w_ssm_out[0].astype(BF16)], _ep_merge, BF16, tm=tm, tn=tn // 2,
                 extras=[(m_a, 0), (gates, D_MODEL)], name="ssm_out_merge")
    x1 = _outproj_call(m, w_out[0].astype(BF16), xp, xs)
    y_p, y_s = _mlp_call(x1, row(mlp_norm_w[0]), w_up[0].astype(BF16), w_down[0].astype(BF16),
                         row(final_norm_w), n_prompt_rows=tp)

    hshape = (SSM_HEADS, SSM_HEAD_DIM, SSM_STATE)
    return (y_p.reshape(bp, lp, d), y_s.reshape(bs, ls, d),
            conv_p[None], xbc_p[None], ssm_p.reshape(1, bp, *hshape),
            conv_s[None], xbc_s[None], ssm_s.reshape(1, bs, *hshape))
```

```python
import functools

import jax
import jax.numpy as jnp
from jax import lax
from jax.experimental import pallas as pl
from jax.experimental.pallas import tpu as pltpu

F32 = jnp.float32
BF16 = jnp.bfloat16

D_MODEL = 2048
CONV_DIM = D_MODEL
CONV_WIDTH = 31
SSM_INNER = 2 * D_MODEL
SSM_HEAD_DIM = 64
SSM_HEADS = SSM_INNER // SSM_HEAD_DIM
SSM_GROUPS = 8
HEADS_PER_GROUP = SSM_HEADS // SSM_GROUPS
SSM_STATE = 128
SSM_CONV_WIDTH = 4
SSM_XBC_DIM = SSM_INNER + 2 * SSM_GROUPS * SSM_STATE
D_FF = 4 * D_MODEL
EPS = 1e-5

V7X_LANES = 128
V7X_VMEM_BYTES = 64 * 1024 * 1024
V7X_VMEM_REQUEST_CAP = 56 * 1024 * 1024
COMPILER_TEMP_BYTES = 12 * 1024 * 1024

SSD_CHUNK = 128
CONV_HALO = 32
SSM_HALO = 16
ROW_TILE = 512
MM_ROW_TILE = 1536


def _vmem_limit(block_bytes, scratch_bytes=0):
    need = 2 * sum(block_bytes) + scratch_bytes + COMPILER_TEMP_BYTES
    return int(min(need, V7X_VMEM_REQUEST_CAP))


def _nbytes(shape, dtype):
    n = 1
    for s in shape:
        n *= s
    return n * jnp.dtype(dtype).itemsize


def _sigmoid(x):
    return 1.0 / (1.0 + jnp.exp(-x))


def _softplus(x):
    return jnp.maximum(x, 0.0) + jnp.log1p(jnp.exp(-jnp.abs(x)))


def _rms(x, w):
    ms = jnp.mean(x * x, axis=-1, keepdims=True)
    return x * lax.rsqrt(ms + EPS) * w


def _norm_kernel(xp_ref, xs_ref, w_ref, o_ref, *, n_prompt_tiles):
    i = pl.program_id(0)

    @pl.when(i < n_prompt_tiles)
    def _():
        o_ref[...] = _rms(xp_ref[...], w_ref[...]).astype(o_ref.dtype)

    @pl.when(i >= n_prompt_tiles)
    def _():
        o_ref[...] = _rms(xs_ref[...], w_ref[...]).astype(o_ref.dtype)


def _norm_call(xp, xs, w):
    tp, d = xp.shape
    ts = xs.shape[0]
    assert ts == ROW_TILE and tp % ROW_TILE == 0
    npt = tp // ROW_TILE
    blocks = [_nbytes((ROW_TILE, d), F32)] * 2 + [_nbytes((ROW_TILE, d), BF16)]
    return pl.pallas_call(
        functools.partial(_norm_kernel, n_prompt_tiles=npt),
        grid=(npt + 1,),
        in_specs=[
            pl.BlockSpec((ROW_TILE, d), lambda i: (jnp.minimum(i, npt - 1), 0)),
            pl.BlockSpec((ROW_TILE, d), lambda i: (0, 0)),
            pl.BlockSpec((1, d), lambda i: (0, 0)),
        ],
        out_specs=pl.BlockSpec((ROW_TILE, d), lambda i: (i, 0)),
        out_shape=jax.ShapeDtypeStruct((tp + ts, d), BF16),
        compiler_params=pltpu.CompilerParams(
            dimension_semantics=("arbitrary",), vmem_limit_bytes=_vmem_limit(blocks)),
        name="mix_rmsnorm",
    )(xp, xs, w)


WPREP_COLS = 512
WCOL_VAL = 0
WCOL_GATE = CONV_DIM
WCOL_Z = 2 * CONV_DIM
WCOL_XBC = WCOL_Z + SSM_INNER
WCOL_G = WCOL_XBC + SSM_XBC_DIM
WCOL_DT = WCOL_G + 2 * D_MODEL
WPREP_TOTAL = WCOL_DT + WPREP_COLS


def _wprep_kernel(a_ref, b_ref, o_ref, *, n_plain, n_shift):
    j = pl.program_id(0)

    @pl.when(j < n_plain)
    def _():
        o_ref[...] = a_ref[0].astype(BF16)

    @pl.when(jnp.logical_and(j >= n_plain, j < n_plain + n_shift))
    def _():
        o_ref[...] = jnp.concatenate([a_ref[0][:, SSM_HEADS:], b_ref[0][:, :SSM_HEADS]], axis=1).astype(BF16)

    @pl.when(j == n_plain + n_shift)
    def _():
        lane = lax.broadcasted_iota(jnp.int32, o_ref.shape, 1)
        o_ref[...] = jnp.where(lane < SSM_HEADS, a_ref[0], 0.0).astype(BF16)


def _wprep_call(w_in):
    _, k, n_src = w_in.shape
    assert n_src == WCOL_G + SSM_HEADS + 2 * D_MODEL
    n_plain = WCOL_G // WPREP_COLS
    n_shift = 2 * D_MODEL // WPREP_COLS
    last_src = (n_src - 1) // WPREP_COLS
    blocks = [_nbytes((k, WPREP_COLS), F32)] * 2 + [_nbytes((k, WPREP_COLS), BF16)]
    return pl.pallas_call(
        functools.partial(_wprep_kernel, n_plain=n_plain, n_shift=n_shift),
        grid=(n_plain + n_shift + 1,),
        in_specs=[
            pl.BlockSpec((1, k, WPREP_COLS), lambda j: (0, 0, jnp.where(j == n_plain + n_shift, n_plain, j))),
            pl.BlockSpec((1, k, WPREP_COLS),
                         lambda j: (0, 0, jnp.where(j < n_plain, n_plain, jnp.minimum(j + 1, last_src)))),
        ],
        out_specs=pl.BlockSpec((k, WPREP_COLS), lambda j: (0, j)),
        out_shape=jax.ShapeDtypeStruct((k, WPREP_TOTAL), BF16),
        compiler_params=pltpu.CompilerParams(
            dimension_semantics=("arbitrary",), vmem_limit_bytes=_vmem_limit(blocks)),
        name="w_in_prep",
    )(w_in, w_in)


def _mm_kernel(*refs, n_w, n_ex, n_p, epilogue):
    lhs_ref = refs[0]
    w_refs = refs[1:1 + n_w]
    ex_refs = refs[1 + n_w:1 + n_w + n_ex]
    p_refs = refs[1 + n_w + n_ex:1 + n_w + n_ex + n_p]
    o_ref = refs[-1]
    lhs = lhs_ref[...]
    accs = [jnp.dot(lhs, w[...], preferred_element_type=F32) for w in w_refs]
    out = epilogue(accs, [e[...].astype(F32) for e in ex_refs], [p[...] for p in p_refs])
    o_ref[...] = out.astype(o_ref.dtype)


def _mm_call(lhs, weights, n, epilogue, out_dtype, *, tm, tn, extras=(), params=(), name):
    t, k = lhs.shape
    assert t % tm == 0 and n % tn == 0
    in_specs = [pl.BlockSpec((tm, k), lambda i, j: (i, 0))]
    operands = [lhs]
    blocks = [_nbytes((tm, k), lhs.dtype)]
    for w, col0 in weights:
        assert col0 % tn == 0
        in_specs.append(pl.BlockSpec((k, tn), lambda i, j, o=col0 // tn: (0, j + o)))
        operands.append(w)
        blocks.append(_nbytes((k, tn), w.dtype))
    for arr, col0 in extras:
        assert col0 % tn == 0
        in_specs.append(pl.BlockSpec((tm, tn), lambda i, j, o=col0 // tn: (i, j + o)))
        operands.append(arr)
        blocks.append(_nbytes((tm, tn), arr.dtype))
    for p in params:
        in_specs.append(pl.BlockSpec((1, tn), lambda i, j: (0, j)))
        operands.append(p)
        blocks.append(_nbytes((8, tn), p.dtype))
    blocks.append(_nbytes((tm, tn), out_dtype))
    return pl.pallas_call(
        functools.partial(_mm_kernel, n_w=len(weights), n_ex=len(extras), n_p=len(params),
                          epilogue=epilogue),
        grid=(t // tm, n // tn),
        in_specs=in_specs,
        out_specs=pl.BlockSpec((tm, tn), lambda i, j: (i, j)),
        out_shape=jax.ShapeDtypeStruct((t, n), out_dtype),
        compiler_params=pltpu.CompilerParams(
            dimension_semantics=("arbitrary", "arbitrary"), vmem_limit_bytes=_vmem_limit(blocks)),
        name=name,
    )(*operands)


def _ep_glu(accs, extras, params):
    return accs[0] * _sigmoid(accs[1])


def _ep_identity(accs, extras, params):
    return accs[0]


def _ep_sigmoid(accs, extras, params):
    return _sigmoid(accs[0])


def _ep_softplus_bias(accs, extras, params):
    return _softplus(accs[0] + params[0])


def _ep_merge(accs, extras, params):
    return extras[0] + extras[1] * accs[0]


XBC_ROW_TILE = 1024
XBC_COL_TILE = 2048
XBC_SUB_COLS = 256
SUBLANES = 8


def _xbc_conv_kernel(h_ref, w_ref, cw_ref, cb_ref, act_ref, tail_ref, carry_ref, stage_ref, *, tiles_per_seq):
    i = pl.program_id(0)
    j = pl.program_id(1)
    tm = h_ref.shape[0]
    hist = SSM_CONV_WIDTH - 1

    @pl.when((i % tiles_per_seq) == 0)
    def _():
        tile_cols = pl.ds(pl.multiple_of(j * XBC_COL_TILE, XBC_COL_TILE), XBC_COL_TILE)
        carry_ref[:, tile_cols] = jnp.zeros((SUBLANES, XBC_COL_TILE), F32)

    lhs = h_ref[...]
    for n in range(XBC_COL_TILE // XBC_SUB_COLS):
        sub = slice(n * XBC_SUB_COLS, (n + 1) * XBC_SUB_COLS)
        gcols = pl.ds(pl.multiple_of(j * XBC_COL_TILE + n * XBC_SUB_COLS, XBC_SUB_COLS), XBC_SUB_COLS)
        st = stage_ref.at[n % 2]
        st[SUBLANES:SUBLANES + tm, :] = jnp.dot(lhs, w_ref[:, sub], preferred_element_type=F32)
        st[0:SUBLANES, :] = carry_ref[:, gcols]
        last = st[tm:tm + SUBLANES, :]
        carry_ref[:, gcols] = last
        tail_ref[0, :, sub] = last
        conv = cb_ref[:, sub] + cw_ref[hist:hist + 1, sub] * st[SUBLANES:SUBLANES + tm, :]
        for k in range(hist):
            conv = conv + cw_ref[k:k + 1, sub] * st[SUBLANES - hist + k:SUBLANES - hist + k + tm, :]
        act_ref[:, sub] = (conv * _sigmoid(conv)).astype(act_ref.dtype)


def _xbc_conv_call(h, w, w_col0, cw, cb, *, n_seq, seq_len):
    k = h.shape[1]
    n = cw.shape[1]
    tm, tn = XBC_ROW_TILE, XBC_COL_TILE
    assert seq_len % tm == 0 and n % tn == 0 and w_col0 % tn == 0
    w_blk0 = w_col0 // tn
    tiles_per_seq = seq_len // tm
    stage_shape = (2, tm + SUBLANES, XBC_SUB_COLS)
    blocks = [_nbytes((tm, k), BF16), _nbytes((k, tn), BF16), _nbytes((8, tn), F32) * 3,
              _nbytes((tm, tn), BF16)]
    return pl.pallas_call(
        functools.partial(_xbc_conv_kernel, tiles_per_seq=tiles_per_seq),
        grid=(n_seq * tiles_per_seq, n // tn),
        in_specs=[
            pl.BlockSpec((tm, k), lambda i, j: (i, 0)),
            pl.BlockSpec((k, tn), lambda i, j: (0, j + w_blk0)),
            pl.BlockSpec((SSM_CONV_WIDTH, tn), lambda i, j: (0, j)),
            pl.BlockSpec((1, tn), lambda i, j: (0, j)),
        ],
        out_specs=[
            pl.BlockSpec((tm, tn), lambda i, j: (i, j)),
            pl.BlockSpec((1, SUBLANES, tn), lambda i, j: (i, 0, j)),
        ],
        out_shape=[
            jax.ShapeDtypeStruct((n_seq * seq_len, n), BF16),
            jax.ShapeDtypeStruct((n_seq * tiles_per_seq, SUBLANES, n), F32),
        ],
        scratch_shapes=[pltpu.VMEM((SUBLANES, n), F32), pltpu.VMEM(stage_shape, F32)],
        compiler_params=pltpu.CompilerParams(
            dimension_semantics=("arbitrary", "arbitrary"),
            vmem_limit_bytes=_vmem_limit(blocks, _nbytes((SUBLANES, n), F32) + _nbytes(stage_shape, F32))),
        name="in_proj_xbc_conv",
    )(h, w, cw, cb)


CONV_ROW_BLOCK = 64
LN_ROW_BLOCK = 256


def _conv31_block(src, zs, dww_ref, dwb_ref, cols, lanes, tm, store):
    hist = CONV_WIDTH - 1
    base = CONV_HALO - hist
    rb = min(CONV_ROW_BLOCK, tm)
    for s in range(SUBLANES):
        rows = tm + SUBLANES * ((hist - s) // SUBLANES)
        zs[s, 0:rows, :] = src[base + s:base + s + rows, lanes]
    bias = jnp.broadcast_to(dwb_ref[:, cols], (rb, V7X_LANES))
    for r0 in range(0, tm, rb):
        acc = bias
        for k in range(CONV_WIDTH):
            s, q = k % SUBLANES, k // SUBLANES
            acc = acc + dww_ref[k:k + 1, cols] * zs[s, r0 + SUBLANES * q:r0 + SUBLANES * q + rb, :]
        store(r0, rb, acc)


def _conf_kernel(*refs, tm, conv_in_kernel, aliased):
    it = iter(refs)
    src_ref = next(it)
    if conv_in_kernel:
        hist_ref = next(it)
        dww_ref = next(it)
        dwb_ref = next(it)
    g_ref = next(it)
    lnw_ref = next(it)
    lnb_ref = next(it)
    w_ref = next(it)
    if aliased:
        next(it)
    o_ref = next(it)
    if conv_in_kernel:
        cache_ref = next(it)
        xs_ref = next(it)
        zs_ref = next(it)
        y_ref = next(it)
    s_ref = next(it)
    hist = CONV_WIDTH - 1

    if conv_in_kernel:
        xs_ref[0:CONV_HALO - hist, :] = jnp.zeros((CONV_HALO - hist, CONV_DIM), F32)
        xs_ref[CONV_HALO - hist:CONV_HALO, :] = hist_ref[0]
        xs_ref[CONV_HALO:CONV_HALO + tm, :] = src_ref[...].astype(F32)
        cache_ref[0] = xs_ref[CONV_HALO + tm - hist:CONV_HALO + tm, :]

        def conv_cols(cb, carry):
            cols = pl.ds(pl.multiple_of(cb * V7X_LANES, V7X_LANES), V7X_LANES)

            def store(r0, rb, acc):
                y_ref[r0:r0 + rb, cols] = acc

            _conv31_block(xs_ref, zs_ref, dww_ref, dwb_ref, cols, cols, tm, store)
            return carry

        lax.fori_loop(0, CONV_DIM // V7X_LANES, conv_cols, 0)

    ln_rows_per_step = min(LN_ROW_BLOCK, tm)

    def ln_rows(rb, carry):
        r0 = pl.multiple_of(rb * ln_rows_per_step, ln_rows_per_step)
        if conv_in_kernel:
            y = y_ref[pl.ds(r0, ln_rows_per_step), :]
        else:
            y = src_ref[pl.ds(r0, ln_rows_per_step), :].astype(F32)
        mu = jnp.mean(y, axis=-1, keepdims=True)
        yc = y - mu
        var = jnp.mean(yc * yc, axis=-1, keepdims=True)
        yn = yc * lax.rsqrt(var + EPS) * lnw_ref[...] + lnb_ref[...]
        s_ref[pl.ds(r0, ln_rows_per_step), :] = (yn * _sigmoid(yn)).astype(BF16)
        return carry

    lax.fori_loop(0, tm // ln_rows_per_step, ln_rows, 0)

    ya = jnp.dot(s_ref[...], w_ref[...], preferred_element_type=F32)
    o_ref[...] = (g_ref[...].astype(F32) * ya).astype(o_ref.dtype)


def _conf_call(src, gates, conv, prev, lnw, lnb, w, *, row0, n_rows, tm, name):
    assert n_rows % tm == 0 and row0 % tm == 0 and src.shape[0] == n_rows
    n_tiles = n_rows // tm
    conv_in_kernel = conv is not None
    hist = CONV_WIDTH - 1
    blk0 = row0 // tm
    in_specs = [pl.BlockSpec((tm, CONV_DIM), lambda i: (i, 0))]
    operands = [src]
    blocks = [_nbytes((tm, CONV_DIM), BF16), _nbytes((tm, D_MODEL), BF16) * 2, _nbytes((8, CONV_DIM), F32) * 2,
              _nbytes((CONV_DIM, D_MODEL), BF16)]
    scratch_dims = [((tm, CONV_DIM), BF16)]
    out_specs = [pl.BlockSpec((tm, D_MODEL), lambda i: (blk0 + i, 0))]
    out_shape = [jax.ShapeDtypeStruct((gates.shape[0], D_MODEL), BF16)]
    if conv_in_kernel:
        assert tm % CONV_HALO == 0
        in_specs += [
            pl.BlockSpec((1, hist, CONV_DIM), lambda i: (i, 0, 0)),
            pl.BlockSpec((CONV_WIDTH, CONV_DIM), lambda i: (0, 0)),
            pl.BlockSpec((1, CONV_DIM), lambda i: (0, 0)),
        ]
        operands += list(conv)
        blocks += [_nbytes((32, CONV_DIM), F32) * 3, _nbytes((8, CONV_DIM), F32)]
        scratch_dims = [((CONV_HALO + tm, CONV_DIM), F32), ((SUBLANES, CONV_HALO + tm, V7X_LANES), F32),
                        ((tm, CONV_DIM), F32)] + scratch_dims
        out_specs.append(pl.BlockSpec((1, hist, CONV_DIM), lambda i: (i, 0, 0)))
        out_shape.append(jax.ShapeDtypeStruct((n_tiles, hist, CONV_DIM), F32))
    in_specs += [
        pl.BlockSpec((tm, D_MODEL), lambda i: (blk0 + i, 0)),
        pl.BlockSpec((1, CONV_DIM), lambda i: (0, 0)),
        pl.BlockSpec((1, CONV_DIM), lambda i: (0, 0)),
        pl.BlockSpec((CONV_DIM, D_MODEL), lambda i: (0, 0)),
    ]
    operands += [gates, lnw, lnb, w]
    aliases = {}
    if prev is not None:
        in_specs.append(pl.BlockSpec(memory_space=pl.ANY))
        operands.append(prev)
        aliases = {len(operands) - 1: 0}
    return pl.pallas_call(
        functools.partial(_conf_kernel, tm=tm, conv_in_kernel=conv_in_kernel, aliased=prev is not None),
        grid=(n_tiles,),
        in_specs=in_specs,
        out_specs=out_specs,
        out_shape=out_shape,
        scratch_shapes=[pltpu.VMEM(shape, dt) for shape, dt in scratch_dims],
        input_output_aliases=aliases,
        compiler_params=pltpu.CompilerParams(
            dimension_semantics=("arbitrary",),
            vmem_limit_bytes=_vmem_limit(blocks, sum(_nbytes(sh, dt) for sh, dt in scratch_dims))),
        name=name,
    )(*operands)


GC_ROW_TILE = 512
GC_COL_TILE = 1024
GC_SUB_COLS = 256


def _glu_conv_kernel(h_ref, wv_ref, wg_ref, dww_ref, dwb_ref, y_ref, tail_ref, carry_ref, xs_ref, zs_ref,
                     *, tiles_per_seq):
    i = pl.program_id(0)
    j = pl.program_id(1)
    tm = h_ref.shape[0]

    @pl.when((i % tiles_per_seq) == 0)
    def _():
        tile_cols = pl.ds(pl.multiple_of(j * GC_COL_TILE, GC_COL_TILE), GC_COL_TILE)
        carry_ref[:, tile_cols] = jnp.zeros((CONV_HALO, GC_COL_TILE), F32)

    lhs = h_ref[...]
    for n in range(GC_COL_TILE // GC_SUB_COLS):
        sub = slice(n * GC_SUB_COLS, (n + 1) * GC_SUB_COLS)
        gcols = pl.ds(pl.multiple_of(j * GC_COL_TILE + n * GC_SUB_COLS, GC_SUB_COLS), GC_SUB_COLS)
        xs = xs_ref.at[n % 2]
        val = jnp.dot(lhs, wv_ref[:, sub], preferred_element_type=F32)
        gate = jnp.dot(lhs, wg_ref[:, sub], preferred_element_type=F32)
        xs[CONV_HALO:CONV_HALO + tm, :] = val * _sigmoid(gate)
        xs[0:CONV_HALO, :] = carry_ref[:, gcols]
        last = xs[tm:tm + CONV_HALO, :]
        carry_ref[:, gcols] = last
        tail_ref[0, :, sub] = last
        for half in range(GC_SUB_COLS // V7X_LANES):
            lanes = slice(half * V7X_LANES, (half + 1) * V7X_LANES)
            cols = slice(n * GC_SUB_COLS + half * V7X_LANES, n * GC_SUB_COLS + (half + 1) * V7X_LANES)

            def store(r0, rb, acc, cols=cols):
                y_ref[r0:r0 + rb, cols] = acc.astype(y_ref.dtype)

            _conv31_block(xs, zs_ref.at[half], dww_ref, dwb_ref, cols, lanes, tm, store)


def _glu_conv_call(h, w, wv_col0, wg_col0, dww, dwb, *, n_seq, seq_len):
    k = h.shape[1]
    n = dww.shape[1]
    tm, tn = GC_ROW_TILE, GC_COL_TILE
    assert seq_len % tm == 0 and n % tn == 0 and wv_col0 % tn == 0 and wg_col0 % tn == 0
    wv_blk0, wg_blk0 = wv_col0 // tn, wg_col0 // tn
    tiles_per_seq = seq_len // tm
    blocks = [_nbytes((tm, k), BF16), _nbytes((k, tn), BF16) * 2, _nbytes((32, tn), F32) * 2,
              _nbytes((tm, tn), BF16), _nbytes((CONV_HALO, tn), F32)]
    scratch_dims = [((CONV_HALO, n), F32), ((2, CONV_HALO + tm, GC_SUB_COLS), F32),
                    ((GC_SUB_COLS // V7X_LANES, SUBLANES, CONV_HALO + tm, V7X_LANES), F32)]
    return pl.pallas_call(
        functools.partial(_glu_conv_kernel, tiles_per_seq=tiles_per_seq),
        grid=(n_seq * tiles_per_seq, n // tn),
        in_specs=[
            pl.BlockSpec((tm, k), lambda i, j: (i, 0)),
            pl.BlockSpec((k, tn), lambda i, j: (0, j + wv_blk0)),
            pl.BlockSpec((k, tn), lambda i, j: (0, j + wg_blk0)),
            pl.BlockSpec((CONV_WIDTH, tn), lambda i, j: (0, j)),
            pl.BlockSpec((1, tn), lambda i, j: (0, j)),
        ],
        out_specs=[
            pl.BlockSpec((tm, tn), lambda i, j: (i, j)),
            pl.BlockSpec((1, CONV_HALO, tn), lambda i, j: (i, 0, j)),
        ],
        out_shape=[
            jax.ShapeDtypeStruct((n_seq * seq_len, n), BF16),
            jax.ShapeDtypeStruct((n_seq * tiles_per_seq, CONV_HALO, n), F32),
        ],
        scratch_shapes=[pltpu.VMEM(shape, dt) for shape, dt in scratch_dims],
        compiler_params=pltpu.CompilerParams(
            dimension_semantics=("arbitrary", "arbitrary"),
            vmem_limit_bytes=_vmem_limit(blocks, sum(_nbytes(sh, dt) for sh, dt in scratch_dims))),
        name="in_proj_glu_conv",
    )(h, w, w, dww, dwb)


SSM_CONV_COL_BLOCK = 512
GROUP_COLS = HEADS_PER_GROUP * SSM_HEAD_DIM
B_COL0 = SSM_INNER
C_COL0 = SSM_INNER + SSM_GROUPS * SSM_STATE
NEG_BIG = -1e30
LOG2_E = 1.4426950408889634


def _split3(x):
    p1 = x.astype(BF16)
    r1 = x - p1.astype(F32)
    p2 = r1.astype(BF16)
    r2 = r1 - p2.astype(F32)
    return p1, p2, r2.astype(BF16)


def _ssd_kernel(*refs, ql, n_chunks, has_h0, conv_in_kernel, aliased):
    it = iter(refs)
    src_ref = next(it)
    if conv_in_kernel:
        hist_ref = next(it)
        cw_ref = next(it)
        cb_ref = next(it)
    dt_ref = next(it)
    z_ref = next(it)
    alog_ref = next(it)
    dskip_ref = next(it)
    nw_ref = next(it)
    h0_ref = next(it) if has_h0 else None
    if aliased:
        next(it)
    y_ref = next(it)
    cache_ref = next(it) if conv_in_kernel else None
    hout_ref = next(it)
    xs_ref = next(it) if conv_in_kernel else None
    act_ref = next(it)
    acs_ref = next(it)
    rowt_ref = next(it)
    wendt_ref = next(it)
    ht_ref = next(it)

    q = SSD_CHUNK
    c_idx = pl.program_id(1)
    hist = SSM_CONV_WIDTH - 1

    @pl.when(c_idx == 0)
    def _():
        for g in range(SSM_GROUPS):
            if has_h0:
                ht_ref[g] = h0_ref[0, g * GROUP_COLS:(g + 1) * GROUP_COLS, :].T
            else:
                ht_ref[g] = jnp.zeros((SSM_STATE, GROUP_COLS), F32)

    if conv_in_kernel:
        assert n_chunks == 1
        xs_ref[0:SSM_HALO - hist, :] = jnp.zeros((SSM_HALO - hist, SSM_XBC_DIM), F32)
        xs_ref[SSM_HALO - hist:SSM_HALO, :] = hist_ref[0]
        xs_ref[SSM_HALO:SSM_HALO + ql, :] = src_ref[...].astype(F32)
        if ql < q:
            xs_ref[SSM_HALO + ql:SSM_HALO + q, :] = jnp.zeros((q - ql, SSM_XBC_DIM), F32)
        base = SSM_HALO - hist
        row_ok = lax.broadcasted_iota(jnp.int32, (q, SSM_CONV_COL_BLOCK), 0) < ql

        def conv_cols(cb, carry):
            c0 = pl.multiple_of(cb * SSM_CONV_COL_BLOCK, SSM_CONV_COL_BLOCK)
            cols = pl.ds(c0, SSM_CONV_COL_BLOCK)
            acc = jnp.broadcast_to(cb_ref[:, cols], (q, SSM_CONV_COL_BLOCK))
            for k in range(SSM_CONV_WIDTH):
                acc = acc + cw_ref[k:k + 1, cols] * xs_ref[base + k:base + k + q, cols]
            act = acc * _sigmoid(acc)
            if ql < q:
                act = jnp.where(row_ok, act, 0.0)
            act_ref[:, cols] = act
            return carry

        lax.fori_loop(0, SSM_XBC_DIM // SSM_CONV_COL_BLOCK, conv_cols, 0)
    else:
        assert ql == q
        act_ref[...] = src_ref[...].astype(F32)

    dt = dt_ref[...]
    if ql < q:
        dt = jnp.concatenate([dt, jnp.zeros((q - ql, V7X_LANES), F32)], axis=0)
    a_neg = -jnp.exp(alog_ref[...])
    dta = dt * a_neg
    r_i = lax.broadcasted_iota(jnp.int32, (q, q), 0)
    c_i = lax.broadcasted_iota(jnp.int32, (q, q), 1)
    causal = r_i >= c_i
    tri = jnp.where(causal, 1.0, 0.0).astype(BF16)
    p1, p2, p3 = _split3(dta)
    acs = (jnp.dot(tri, p1, preferred_element_type=F32) + jnp.dot(tri, p2, preferred_element_type=F32)
           + jnp.dot(tri, p3, preferred_element_type=F32))
    acs2 = acs * LOG2_E
    acs_ref[...] = acs2
    acs2_t = acs2.T
    rowt = acs2_t - jnp.log2(dt.T)
    rowt_ref[...] = rowt
    wendt_ref[...] = jnp.exp2(acs2_t[:, q - 1:q] - rowt)

    left = c_i < SSM_HEAD_DIM
    left_row = left[0:1, :]

    for g in range(SSM_GROUPS):
        bg = act_ref[:, B_COL0 + g * SSM_STATE:B_COL0 + (g + 1) * SSM_STATE]
        cg = act_ref[:, C_COL0 + g * SSM_STATE:C_COL0 + (g + 1) * SSM_STATE].astype(BF16)
        cbm = lax.dot_general(cg, bg.astype(BF16), (((1,), (1,)), ((), ())),
                              preferred_element_type=F32)
        bg_t = bg.T
        ht_g = ht_ref[g]
        yoff = jnp.dot(cg, ht_g.astype(BF16), preferred_element_type=F32)
        y_parts, inc_parts, dec_parts = [], [], []
        for j in range(HEADS_PER_GROUP // 2):
            h_a = g * HEADS_PER_GROUP + 2 * j
            h_b = h_a + 1
            lo = h_a * SSM_HEAD_DIM
            xp = act_ref[:, lo:lo + V7X_LANES]
            lhs_parts = []
            cols = []
            for h in (h_a, h_b):
                col = jnp.broadcast_to(acs_ref[:, h:h + 1], (q, q))
                cols.append(col)
                seg = jnp.where(causal, col - rowt_ref[h:h + 1, :], NEG_BIG)
                m_h = cbm * jnp.exp2(seg)
                w_h = bg_t * wendt_ref[h:h + 1, :]
                lhs_parts.append(jnp.concatenate([m_h, w_h], axis=0).astype(BF16))
            rhs = jnp.concatenate([jnp.where(left, xp, 0.0).astype(BF16),
                                   jnp.where(left, 0.0, xp).astype(BF16)], axis=0)
            res = jnp.dot(jnp.concatenate(lhs_parts, axis=1), rhs, preferred_element_type=F32)
            colsel = jnp.where(left, cols[0], cols[1])
            y_parts.append(res[0:q] + jnp.exp2(colsel) * yoff[:, j * V7X_LANES:(j + 1) * V7X_LANES]
                           + dskip_ref[:, lo:lo + V7X_LANES] * xp)
            inc_parts.append(res[q:2 * q])
            last_a = jnp.broadcast_to(acs_ref[q - 1:q, h_a:h_a + 1], (1, V7X_LANES))
            last_b = jnp.broadcast_to(acs_ref[q - 1:q, h_b:h_b + 1], (1, V7X_LANES))
            dec_parts.append(jnp.exp2(jnp.where(left_row, last_a, last_b)))
        ht_ref[g] = jnp.concatenate(dec_parts, axis=1) * ht_g + jnp.concatenate(inc_parts, axis=1)
        gc = slice(g * GROUP_COLS, (g + 1) * GROUP_COLS)
        yg = jnp.concatenate(y_parts, axis=1)[0:ql]
        zg = z_ref[:, gc].astype(F32)
        yz = yg * (zg * _sigmoid(zg))
        ms = jnp.mean(yz * yz, axis=-1, keepdims=True)
        y_ref[:, gc] = (yz * lax.rsqrt(ms + EPS) * nw_ref[:, gc]).astype(y_ref.dtype)

    @pl.when(c_idx == n_chunks - 1)
    def _():
        if conv_in_kernel:
            cache_ref[0] = xs_ref[SSM_HALO + ql - hist:SSM_HALO + ql, :]
        for g in range(SSM_GROUPS):
            hout_ref[0, g * GROUP_COLS:(g + 1) * GROUP_COLS, :] = ht_ref[g].T


def _ssd_call(src, dt, z, conv, h0, prev, alog, dskip, nw, *, row0, n_seq, seq_len, ql, name):
    assert seq_len % ql == 0 and row0 % ql == 0 and ql % SSM_HALO == 0 and ql <= SSD_CHUNK
    assert src.shape[0] == n_seq * seq_len
    n_chunks = seq_len // ql
    conv_in_kernel = conv is not None
    has_h0 = h0 is not None
    hist = SSM_CONV_WIDTH - 1
    blk0 = row0 // ql
    q = SSD_CHUNK

    def row_blk(b, c):
        return blk0 + b * n_chunks + c

    in_specs = [pl.BlockSpec((ql, SSM_XBC_DIM), lambda b, c: (b * n_chunks + c, 0))]
    operands = [src]
    if conv_in_kernel:
        in_specs += [
            pl.BlockSpec((1, hist, SSM_XBC_DIM), lambda b, c: (b, 0, 0)),
            pl.BlockSpec((SSM_CONV_WIDTH, SSM_XBC_DIM), lambda b, c: (0, 0)),
            pl.BlockSpec((1, SSM_XBC_DIM), lambda b, c: (0, 0)),
        ]
        operands += list(conv)
    in_specs += [
        pl.BlockSpec((ql, V7X_LANES), lambda b, c: (row_blk(b, c), 0)),
        pl.BlockSpec((ql, SSM_INNER), lambda b, c: (row_blk(b, c), 0)),
        pl.BlockSpec((1, V7X_LANES), lambda b, c: (0, 0)),
        pl.BlockSpec((1, SSM_INNER), lambda b, c: (0, 0)),
        pl.BlockSpec((1, SSM_INNER), lambda b, c: (0, 0)),
    ]
    operands += [dt, z, alog, dskip, nw]
    if has_h0:
        in_specs.append(pl.BlockSpec((1, SSM_INNER, SSM_STATE), lambda b, c: (b, 0, 0)))
        operands.append(h0)
    aliases = {}
    if prev is not None:
        in_specs.append(pl.BlockSpec(memory_space=pl.ANY))
        operands.append(prev)
        aliases = {len(operands) - 1: 0}
    state_bytes = _nbytes((SSM_INNER, SSM_STATE), F32)
    blocks = [_nbytes((ql, SSM_XBC_DIM), BF16), _nbytes((SSM_HALO, SSM_XBC_DIM), F32),
              _nbytes((ql, V7X_LANES), F32), _nbytes((ql, SSM_INNER), BF16),
              _nbytes((8, SSM_XBC_DIM), F32) * 2, _nbytes((8, SSM_INNER), F32) * 2,
              _nbytes((ql, SSM_INNER), BF16), _nbytes((8, SSM_XBC_DIM), F32),
              state_bytes * (2 if has_h0 else 1)]
    scratch_dims = ([((SSM_HALO + q, SSM_XBC_DIM), F32)] if conv_in_kernel else []) + [
        ((q, SSM_XBC_DIM), F32),
        ((q, V7X_LANES), F32),
        ((V7X_LANES, q), F32),
        ((V7X_LANES, q), F32),
        ((SSM_GROUPS, SSM_STATE, GROUP_COLS), F32),
    ]
    out_specs = [pl.BlockSpec((ql, SSM_INNER), lambda b, c: (row_blk(b, c), 0))]
    out_shape = [jax.ShapeDtypeStruct((dt.shape[0], SSM_INNER), BF16)]
    if conv_in_kernel:
        out_specs.append(pl.BlockSpec((1, hist, SSM_XBC_DIM), lambda b, c: (b, 0, 0)))
        out_shape.append(jax.ShapeDtypeStruct((n_seq, hist, SSM_XBC_DIM), F32))
    out_specs.append(pl.BlockSpec((1, SSM_INNER, SSM_STATE), lambda b, c: (b, 0, 0)))
    out_shape.append(jax.ShapeDtypeStruct((n_seq, SSM_INNER, SSM_STATE), F32))
    return pl.pallas_call(
        functools.partial(_ssd_kernel, ql=ql, n_chunks=n_chunks, has_h0=has_h0,
                          conv_in_kernel=conv_in_kernel, aliased=prev is not None),
        grid=(n_seq, n_chunks),
        in_specs=in_specs,
        out_specs=out_specs,
        out_shape=out_shape,
        scratch_shapes=[pltpu.VMEM(shape, dt_) for shape, dt_ in scratch_dims],
        input_output_aliases=aliases,
        compiler_params=pltpu.CompilerParams(
            dimension_semantics=("arbitrary", "arbitrary"),
            vmem_limit_bytes=_vmem_limit(blocks, sum(_nbytes(sh, dt_) for sh, dt_ in scratch_dims))),
        name=name,
    )(*operands)


def _outproj_kernel(m_ref, w_ref, xp_ref, xs_ref, o_ref, *, n_prompt_tiles):
    i = pl.program_id(0)
    acc = jnp.dot(m_ref[...], w_ref[...], preferred_element_type=F32)

    @pl.when(i < n_prompt_tiles)
    def _():
        o_ref[...] = xp_ref[...] + acc

    @pl.when(i >= n_prompt_tiles)
    def _():
        o_ref[...] = xs_ref[...] + acc


def _outproj_call(m, w, xp, xs):
    t, d = m.shape
    npt = xp.shape[0] // ROW_TILE
    blocks = [_nbytes((ROW_TILE, d), BF16), _nbytes((d, d), BF16)] + [_nbytes((ROW_TILE, d), F32)] * 3
    return pl.pallas_call(
        functools.partial(_outproj_kernel, n_prompt_tiles=npt),
        grid=(t // ROW_TILE,),
        in_specs=[
            pl.BlockSpec((ROW_TILE, d), lambda i: (i, 0)),
            pl.BlockSpec((d, d), lambda i: (0, 0)),
            pl.BlockSpec((ROW_TILE, d), lambda i: (jnp.minimum(i, npt - 1), 0)),
            pl.BlockSpec((ROW_TILE, d), lambda i: (0, 0)),
        ],
        out_specs=pl.BlockSpec((ROW_TILE, d), lambda i: (i, 0)),
        out_shape=jax.ShapeDtypeStruct((t, d), F32),
        compiler_params=pltpu.CompilerParams(
            dimension_semantics=("arbitrary",), vmem_limit_bytes=_vmem_limit(blocks)),
        name="out_proj_residual",
    )(m, w, xp, xs)


MLP_FF_TILE = 1024


def _mlp_kernel(x1_ref, nw_ref, wup_ref, wdn_ref, fw_ref, yp_ref, ys_ref, h2_ref, acc_ref,
                *, n_prompt_tiles, n_ff):
    i = pl.program_id(0)
    f = pl.program_id(1)

    @pl.when(f == 0)
    def _():
        h2_ref[...] = _rms(x1_ref[...], nw_ref[...]).astype(BF16)
        acc_ref[...] = jnp.zeros_like(acc_ref)

    hid = jnp.dot(h2_ref[...], wup_ref[...], preferred_element_type=F32)
    hid = jnp.square(jnp.maximum(hid, 0.0))
    acc_ref[...] += jnp.dot(hid.astype(BF16), wdn_ref[...], preferred_element_type=F32)

    @pl.when(f == n_ff - 1)
    def _():
        out = _rms(x1_ref[...] + acc_ref[...], fw_ref[...])

        @pl.when(i < n_prompt_tiles)
        def _():
            yp_ref[...] = out

        @pl.when(i >= n_prompt_tiles)
        def _():
            ys_ref[...] = out


def _mlp_call(x1, nw, wup, wdn, fw, *, n_prompt_rows):
    t, d = x1.shape
    npt = n_prompt_rows // ROW_TILE
    n_ff = D_FF // MLP_FF_TILE
    blocks = [_nbytes((ROW_TILE, d), F32), _nbytes((d, MLP_FF_TILE), BF16),
              _nbytes((MLP_FF_TILE, d), BF16), _nbytes((ROW_TILE, d), F32) * 2]
    scratch_bytes = _nbytes((ROW_TILE, d), BF16) + _nbytes((ROW_TILE, d), F32)
    return pl.pallas_call(
        functools.partial(_mlp_kernel, n_prompt_tiles=npt, n_ff=n_ff),
        grid=(t // ROW_TILE, n_ff),
        in_specs=[
            pl.BlockSpec((ROW_TILE, d), lambda i, f: (i, 0)),
            pl.BlockSpec((1, d), lambda i, f: (0, 0)),
            pl.BlockSpec((d, MLP_FF_TILE), lambda i, f: (0, f)),
            pl.BlockSpec((MLP_FF_TILE, d), lambda i, f: (f, 0)),
            pl.BlockSpec((1, d), lambda i, f: (0, 0)),
        ],
        out_specs=[
            pl.BlockSpec((ROW_TILE, d), lambda i, f: (jnp.minimum(i, npt - 1), 0)),
            pl.BlockSpec((ROW_TILE, d), lambda i, f: (0, 0)),
        ],
        out_shape=[
            jax.ShapeDtypeStruct((n_prompt_rows, d), F32),
            jax.ShapeDtypeStruct((t - n_prompt_rows, d), F32),
        ],
        scratch_shapes=[pltpu.VMEM((ROW_TILE, d), BF16), pltpu.VMEM((ROW_TILE, d), F32)],
        compiler_params=pltpu.CompilerParams(
            dimension_semantics=("arbitrary", "arbitrary"),
            vmem_limit_bytes=_vmem_limit(blocks, scratch_bytes)),
        name="mlp_final_norm",
    )(x1, nw, wup, wdn, fw)


def kernel(x_prompt, x_sample, cache_conv, cache_ssm_conv, state_ssm, mix_norm_w, w_in, conv_dw_w,
           conv_dw_b, conv_ln_w, conv_ln_b, w_conv_out, ssm_conv_w, ssm_conv_b, dt_bias, a_log, d_skip,
           ssm_norm_w, w_ssm_out, w_out, mlp_norm_w, w_up, w_down, final_norm_w):
    bp, lp, d = x_prompt.shape
    bs, ls, _ = x_sample.shape
    tp, ts = bp * lp, bs * ls
    xp = x_prompt.reshape(tp, d)
    xs = x_sample.reshape(ts, d)

    wi = _wprep_call(w_in)
    dt_b =jnp.pad(dt_bias[0], (0, V7X_LANES - SSM_HEADS)).reshape(1, V7X_LANES)
    alog = jnp.pad(a_log[0], (0, V7X_LANES - SSM_HEADS)).reshape(1, V7X_LANES)
    dskip = jnp.repeat(d_skip[0], SSM_HEAD_DIM).reshape(1, SSM_INNER)
    row = lambda v: v.reshape(1, -1)

    h = _norm_call(xp, xs, row(mix_norm_w[0]))

    tm, tn = MM_ROW_TILE, 1024
    dww, dwb = conv_dw_w[0], row(conv_dw_b[0])
    yconv_p, a_tail_p = _glu_conv_call(h, wi, WCOL_VAL, WCOL_GATE, dww, dwb, n_seq=bp, seq_len=lp)
    h_s = h[tp:]
    a_s = _mm_call(h_s, [(wi, WCOL_VAL), (wi, WCOL_GATE)], CONV_DIM, _ep_glu, BF16, tm=ts, tn=tn // 2,
                   name="in_proj_glu_sample")
    z = _mm_call(h, [(wi, WCOL_Z)], SSM_INNER, _ep_identity, BF16, tm=tm, tn=tn, name="in_proj_z")
    cw, cb = ssm_conv_w[0], row(ssm_conv_b[0])
    act_p, xbc_tail_p = _xbc_conv_call(h, wi, WCOL_XBC, cw, cb, n_seq=bp, seq_len=lp)
    xbc_s = _mm_call(h_s, [(wi, WCOL_XBC)], SSM_XBC_DIM, _ep_identity, BF16, tm=ts, tn=tn,
                     name="in_proj_xbc_sample")
    dt = _mm_call(h, [(wi, WCOL_DT)], V7X_LANES, _ep_softplus_bias, F32, tm=tm, tn=V7X_LANES, params=[dt_b],
                  name="in_proj_dt")
    gates = _mm_call(h, [(wi, WCOL_G)], 2 * D_MODEL, _ep_sigmoid, BF16, tm=tm, tn=tn, name="in_proj_gates")

    conf_args = (row(conv_ln_w[0]), row(conv_ln_b[0]), w_conv_out[0].astype(BF16))
    (m_a,) = _conf_call(yconv_p, gates, None, None, *conf_args, row0=0, n_rows=tp, tm=ROW_TILE,
                        name="conformer_prompt")
    gc_tiles = lp // GC_ROW_TILE
    conv_p = a_tail_p[gc_tiles - 1::gc_tiles, CONV_HALO - (CONV_WIDTH - 1):, :]
    m_a, conv_s = _conf_call(a_s, gates, (cache_conv[0], dww, dwb), m_a, *conf_args, row0=tp, n_rows=ts,
                             tm=ls, name="conformer_sample")

    ssd_args = (alog, dskip, row(ssm_norm_w[0]))
    yn, ssm_p = _ssd_call(act_p, dt, z, None, None, None, *ssd_args, row0=0, n_seq=bp, seq_len=lp,
                          ql=SSD_CHUNK, name="ssd_prompt")
    tiles_per_seq = lp // XBC_ROW_TILE
    xbc_p = xbc_tail_p[tiles_per_seq - 1::tiles_per_seq, SUBLANES - (SSM_CONV_WIDTH - 1):, :]
    h0 = state_ssm[0].reshape(bs, SSM_INNER, SSM_STATE)
    yn, xbc_s, ssm_s = _ssd_call(xbc_s, dt, z, (cache_ssm_conv[0], cw, cb), h0, yn, *ssd_args, row0=tp,
                                 n_seq=bs, seq_len=ls, ql=ls, name="ssd_sample")

    m = _mm_call(yn, [(w_ssm_out[0].astype(BF16), 0)], D_MODEL, _ep_merge, BF16, tm=tm, tn=tn // 2,
                 extras=[(m_a, 0), (gates, D_MODEL)], name="ssm_out_merge")
    x1 = _outproj_call(m, w_out[0].astype(BF16), xp, xs)
    y_p, y_s = _mlp_call(x1, row(mlp_norm_w[0]), w_up[0].astype(BF16), w_down[0].astype(BF16),
                         row(final_norm_w), n_prompt_rows=tp)

    hshape = (SSM_HEADS, SSM_HEAD_DIM, SSM_STATE)
    return (y_p.reshape(bp, lp, d), y_s.reshape(bs, ls, d),
            conv_p[None], xbc_p[None], ssm_p.reshape(1, bp, *hshape),
            conv_s[None], xbc_s[None], ssm_s.reshape(1, bs, *hshape))
```

```python
import functools

import jax
import jax.numpy as jnp
from jax import lax
from jax.experimental import pallas as pl
from jax.experimental.pallas import tpu as pltpu

F32 = jnp.float32
BF16 = jnp.bfloat16

D_MODEL = 2048
CONV_DIM = D_MODEL
CONV_WIDTH = 31
SSM_INNER = 2 * D_MODEL
SSM_HEAD_DIM = 64
SSM_HEADS = SSM_INNER // SSM_HEAD_DIM
SSM_GROUPS = 8
HEADS_PER_GROUP = SSM_HEADS // SSM_GROUPS
SSM_STATE = 128
SSM_CONV_WIDTH = 4
SSM_XBC_DIM = SSM_INNER + 2 * SSM_GROUPS * SSM_STATE
D_FF = 4 * D_MODEL
EPS = 1e-5

V7X_LANES = 128
V7X_VMEM_BYTES = 64 * 1024 * 1024
V7X_VMEM_REQUEST_CAP = 56 * 1024 * 1024
COMPILER_TEMP_BYTES = 12 * 1024 * 1024

SSD_CHUNK = 128
CONV_HALO = 32
SSM_HALO = 16
ROW_TILE = 512
MM_ROW_TILE = 1536


def _vmem_limit(block_bytes, scratch_bytes=0):
    need = 2 * sum(block_bytes) + scratch_bytes + COMPILER_TEMP_BYTES
    return int(min(need, V7X_VMEM_REQUEST_CAP))


def _nbytes(shape, dtype):
    n = 1
    for s in shape:
        n *= s
    return n * jnp.dtype(dtype).itemsize


def _sigmoid(x):
    return 1.0 / (1.0 + jnp.exp(-x))


def _softplus(x):
    return jnp.maximum(x, 0.0) + jnp.log1p(jnp.exp(-jnp.abs(x)))


def _rms(x, w):
    ms = jnp.mean(x * x, axis=-1, keepdims=True)
    return x * lax.rsqrt(ms + EPS) * w


def _norm_kernel(xp_ref, xs_ref, w_ref, o_ref, *, n_prompt_tiles):
    i = pl.program_id(0)

    @pl.when(i < n_prompt_tiles)
    def _():
        o_ref[...] = _rms(xp_ref[...], w_ref[...]).astype(o_ref.dtype)

    @pl.when(i >= n_prompt_tiles)
    def _():
        o_ref[...] = _rms(xs_ref[...], w_ref[...]).astype(o_ref.dtype)


def _norm_call(xp, xs, w):
    tp, d = xp.shape
    ts = xs.shape[0]
    assert ts == ROW_TILE and tp % ROW_TILE == 0
    npt = tp // ROW_TILE
    blocks = [_nbytes((ROW_TILE, d), F32)] * 2 + [_nbytes((ROW_TILE, d), BF16)]
    return pl.pallas_call(
        functools.partial(_norm_kernel, n_prompt_tiles=npt),
        grid=(npt + 1,),
        in_specs=[
            pl.BlockSpec((ROW_TILE, d), lambda i: (jnp.minimum(i, npt - 1), 0)),
            pl.BlockSpec((ROW_TILE, d), lambda i: (0, 0)),
            pl.BlockSpec((1, d), lambda i: (0, 0)),
        ],
        out_specs=pl.BlockSpec((ROW_TILE, d), lambda i: (i, 0)),
        out_shape=jax.ShapeDtypeStruct((tp + ts, d), BF16),
        compiler_params=pltpu.CompilerParams(
            dimension_semantics=("arbitrary",), vmem_limit_bytes=_vmem_limit(blocks)),
        name="mix_rmsnorm",
    )(xp, xs, w)


WPREP_COLS = 512
WCOL_VAL = 0
WCOL_GATE = CONV_DIM
WCOL_Z = 2 * CONV_DIM
WCOL_XBC = WCOL_Z + SSM_INNER
WCOL_G = WCOL_XBC + SSM_XBC_DIM
WCOL_DT = WCOL_G + 2 * D_MODEL
WPREP_TOTAL = WCOL_DT + WPREP_COLS


def _wprep_kernel(a_ref, b_ref, o_ref, *, n_plain, n_shift):
    j = pl.program_id(0)

    @pl.when(j < n_plain)
    def _():
        o_ref[...] = a_ref[...].astype(BF16)

    @pl.when(jnp.logical_and(j >= n_plain, j < n_plain + n_shift))
    def _():
        o_ref[...] = jnp.concatenate([a_ref[:, SSM_HEADS:], b_ref[:, :SSM_HEADS]], axis=1).astype(BF16)

    @pl.when(j == n_plain + n_shift)
    def _():
        lane = lax.broadcasted_iota(jnp.int32, o_ref.shape, 1)
        o_ref[...] = jnp.where(lane < SSM_HEADS, a_ref[...], 0.0).astype(BF16)


def _wprep_call(w_in):
    k, n_src = w_in.shape
    assert n_src == WCOL_G + SSM_HEADS + 2 * D_MODEL
    n_plain = WCOL_G // WPREP_COLS
    n_shift = 2 * D_MODEL // WPREP_COLS
    last_src = (n_src - 1) // WPREP_COLS
    blocks = [_nbytes((k, WPREP_COLS), F32)] * 2 + [_nbytes((k, WPREP_COLS), BF16)]
    return pl.pallas_call(
        functools.partial(_wprep_kernel, n_plain=n_plain, n_shift=n_shift),
        grid=(n_plain + n_shift + 1,),
        in_specs=[
            pl.BlockSpec((k, WPREP_COLS), lambda j: (0, jnp.where(j == n_plain + n_shift, n_plain, j))),
            pl.BlockSpec((k, WPREP_COLS),
                         lambda j: (0, jnp.where(j < n_plain, n_plain, jnp.minimum(j + 1, last_src)))),
        ],
        out_specs=pl.BlockSpec((k, WPREP_COLS), lambda j: (0, j)),
        out_shape=jax.ShapeDtypeStruct((k, WPREP_TOTAL), BF16),
        compiler_params=pltpu.CompilerParams(
            dimension_semantics=("arbitrary",), vmem_limit_bytes=_vmem_limit(blocks)),
        name="w_in_prep",
    )(w_in, w_in)


def _mm_kernel(*refs, n_w, n_ex, n_p, epilogue):
    lhs_ref = refs[0]
    w_refs = refs[1:1 + n_w]
    ex_refs = refs[1 + n_w:1 + n_w + n_ex]
    p_refs = refs[1 + n_w + n_ex:1 + n_w + n_ex + n_p]
    o_ref = refs[-1]
    lhs = lhs_ref[...]
    accs = [jnp.dot(lhs, w[...], preferred_element_type=F32) for w in w_refs]
    out = epilogue(accs, [e[...].astype(F32) for e in ex_refs], [p[...] for p in p_refs])
    o_ref[...] = out.astype(o_ref.dtype)


def _mm_call(lhs, weights, n, epilogue, out_dtype, *, tm, tn, extras=(), params=(), name):
    t, k = lhs.shape
    assert t % tm == 0 and n % tn == 0
    in_specs = [pl.BlockSpec((tm, k), lambda i, j: (i, 0))]
    operands = [lhs]
    blocks = [_nbytes((tm, k), lhs.dtype)]
    for w, col0 in weights:
        assert col0 % tn == 0
        in_specs.append(pl.BlockSpec((k, tn), lambda i, j, o=col0 // tn: (0, j + o)))
        operands.append(w)
        blocks.append(_nbytes((k, tn), w.dtype))
    for arr, col0 in extras:
        assert col0 % tn == 0
        in_specs.append(pl.BlockSpec((tm, tn), lambda i, j, o=col0 // tn: (i, j + o)))
        operands.append(arr)
        blocks.append(_nbytes((tm, tn), arr.dtype))
    for p in params:
        in_specs.append(pl.BlockSpec((1, tn), lambda i, j: (0, j)))
        operands.append(p)
        blocks.append(_nbytes((8, tn), p.dtype))
    blocks.append(_nbytes((tm, tn), out_dtype))
    return pl.pallas_call(
        functools.partial(_mm_kernel, n_w=len(weights), n_ex=len(extras), n_p=len(params),
                          epilogue=epilogue),
        grid=(t // tm, n // tn),
        in_specs=in_specs,
        out_specs=pl.BlockSpec((tm, tn), lambda i, j: (i, j)),
        out_shape=jax.ShapeDtypeStruct((t, n), out_dtype),
        compiler_params=pltpu.CompilerParams(
            dimension_semantics=("arbitrary", "arbitrary"), vmem_limit_bytes=_vmem_limit(blocks)),
        name=name,
    )(*operands)


def _ep_glu(accs, extras, params):
    return accs[0] * _sigmoid(accs[1])


def _ep_identity(accs, extras, params):
    return accs[0]


def _ep_sigmoid(accs, extras, params):
    return _sigmoid(accs[0])


def _ep_softplus_bias(accs, extras, params):
    return _softplus(accs[0] + params[0])


def _ep_merge(accs, extras, params):
    return extras[0] + extras[1] * accs[0]


XBC_ROW_TILE = 1024
XBC_COL_TILE = 2048
XBC_SUB_COLS = 256
SUBLANES = 8


def _xbc_conv_kernel(h_ref, w_ref, cw_ref, cb_ref, act_ref, tail_ref, carry_ref, stage_ref, *, tiles_per_seq):
    i = pl.program_id(0)
    j = pl.program_id(1)
    tm = h_ref.shape[0]
    hist = SSM_CONV_WIDTH - 1

    @pl.when((i % tiles_per_seq) == 0)
    def _():
        tile_cols = pl.ds(pl.multiple_of(j * XBC_COL_TILE, XBC_COL_TILE), XBC_COL_TILE)
        carry_ref[:, tile_cols] = jnp.zeros((SUBLANES, XBC_COL_TILE), F32)

    lhs = h_ref[...]
    for n in range(XBC_COL_TILE // XBC_SUB_COLS):
        sub = slice(n * XBC_SUB_COLS, (n + 1) * XBC_SUB_COLS)
        gcols = pl.ds(pl.multiple_of(j * XBC_COL_TILE + n * XBC_SUB_COLS, XBC_SUB_COLS), XBC_SUB_COLS)
        st = stage_ref.at[n % 2]
        st[SUBLANES:SUBLANES + tm, :] = jnp.dot(lhs, w_ref[:, sub], preferred_element_type=F32)
        st[0:SUBLANES, :] = carry_ref[:, gcols]
        last = st[tm:tm + SUBLANES, :]
        carry_ref[:, gcols] = last
        tail_ref[0, :, sub] = last
        conv = cb_ref[:, sub] + cw_ref[hist:hist + 1, sub] * st[SUBLANES:SUBLANES + tm, :]
        for k in range(hist):
            conv = conv + cw_ref[k:k + 1, sub] * st[SUBLANES - hist + k:SUBLANES - hist + k + tm, :]
        act_ref[:, sub] = (conv * _sigmoid(conv)).astype(act_ref.dtype)


def _xbc_conv_call(h, w, w_col0, cw, cb, *, n_seq, seq_len):
    k = h.shape[1]
    n = cw.shape[1]
    tm, tn = XBC_ROW_TILE, XBC_COL_TILE
    assert seq_len % tm == 0 and n % tn == 0 and w_col0 % tn == 0
    w_blk0 = w_col0 // tn
    tiles_per_seq = seq_len // tm
    stage_shape = (2, tm + SUBLANES, XBC_SUB_COLS)
    blocks = [_nbytes((tm, k), BF16), _nbytes((k, tn), BF16), _nbytes((8, tn), F32) * 3,
              _nbytes((tm, tn), BF16)]
    return pl.pallas_call(
        functools.partial(_xbc_conv_kernel, tiles_per_seq=tiles_per_seq),
        grid=(n_seq * tiles_per_seq, n // tn),
        in_specs=[
            pl.BlockSpec((tm, k), lambda i, j: (i, 0)),
            pl.BlockSpec((k, tn), lambda i, j: (0, j + w_blk0)),
            pl.BlockSpec((SSM_CONV_WIDTH, tn), lambda i, j: (0, j)),
            pl.BlockSpec((1, tn), lambda i, j: (0, j)),
        ],
        out_specs=[
            pl.BlockSpec((tm, tn), lambda i, j: (i, j)),
            pl.BlockSpec((1, SUBLANES, tn), lambda i, j: (i, 0, j)),
        ],
        out_shape=[
            jax.ShapeDtypeStruct((n_seq * seq_len, n), BF16),
            jax.ShapeDtypeStruct((n_seq * tiles_per_seq, SUBLANES, n), F32),
        ],
        scratch_shapes=[pltpu.VMEM((SUBLANES, n), F32), pltpu.VMEM(stage_shape, F32)],
        compiler_params=pltpu.CompilerParams(
            dimension_semantics=("arbitrary", "arbitrary"),
            vmem_limit_bytes=_vmem_limit(blocks, _nbytes((SUBLANES, n), F32) + _nbytes(stage_shape, F32))),
        name="in_proj_xbc_conv",
    )(h, w, cw, cb)


CONV_ROW_BLOCK = 64
LN_ROW_BLOCK = 256


def _conv31_block(src, zs, dww_ref, dwb_ref, cols, lanes, tm, store):
    hist = CONV_WIDTH - 1
    base = CONV_HALO - hist
    rb = min(CONV_ROW_BLOCK, tm)
    for s in range(SUBLANES):
        rows = tm + SUBLANES * ((hist - s) // SUBLANES)
        zs[s, 0:rows, :] = src[base + s:base + s + rows, lanes]
    bias = jnp.broadcast_to(dwb_ref[:, cols], (rb, V7X_LANES))
    for r0 in range(0, tm, rb):
        acc = bias
        for k in range(CONV_WIDTH):
            s, q = k % SUBLANES, k // SUBLANES
            acc = acc + dww_ref[k:k + 1, cols] * zs[s, r0 + SUBLANES * q:r0 + SUBLANES * q + rb, :]
        store(r0, rb, acc)


def _conf_kernel(*refs, tm, conv_in_kernel, aliased):
    it = iter(refs)
    src_ref = next(it)
    if conv_in_kernel:
        hist_ref = next(it)
        dww_ref = next(it)
        dwb_ref = next(it)
    g_ref = next(it)
    lnw_ref = next(it)
    lnb_ref = next(it)
    w_ref = next(it)
    if aliased:
        next(it)
    o_ref = next(it)
    if conv_in_kernel:
        cache_ref = next(it)
        xs_ref = next(it)
        zs_ref = next(it)
        y_ref = next(it)
    s_ref = next(it)
    hist = CONV_WIDTH - 1

    if conv_in_kernel:
        xs_ref[0:CONV_HALO - hist, :] = jnp.zeros((CONV_HALO - hist, CONV_DIM), F32)
        xs_ref[CONV_HALO - hist:CONV_HALO, :] = hist_ref[0]
        xs_ref[CONV_HALO:CONV_HALO + tm, :] = src_ref[...].astype(F32)
        cache_ref[0] = xs_ref[CONV_HALO + tm - hist:CONV_HALO + tm, :]

        def conv_cols(cb, carry):
            cols = pl.ds(pl.multiple_of(cb * V7X_LANES, V7X_LANES), V7X_LANES)

            def store(r0, rb, acc):
                y_ref[r0:r0 + rb, cols] = acc

            _conv31_block(xs_ref, zs_ref, dww_ref, dwb_ref, cols, cols, tm, store)
            return carry

        lax.fori_loop(0, CONV_DIM // V7X_LANES, conv_cols, 0)

    ln_rows_per_step = min(LN_ROW_BLOCK, tm)

    def ln_rows(rb, carry):
        r0 = pl.multiple_of(rb * ln_rows_per_step, ln_rows_per_step)
        if conv_in_kernel:
            y = y_ref[pl.ds(r0, ln_rows_per_step), :]
        else:
            y = src_ref[pl.ds(r0, ln_rows_per_step), :].astype(F32)
        mu = jnp.mean(y, axis=-1, keepdims=True)
        yc = y - mu
        var = jnp.mean(yc * yc, axis=-1, keepdims=True)
        yn = yc * lax.rsqrt(var + EPS) * lnw_ref[...] + lnb_ref[...]
        s_ref[pl.ds(r0, ln_rows_per_step), :] = (yn * _sigmoid(yn)).astype(BF16)
        return carry

    lax.fori_loop(0, tm // ln_rows_per_step, ln_rows, 0)

    ya = jnp.dot(s_ref[...], w_ref[...], preferred_element_type=F32)
    o_ref[...] = (g_ref[...].astype(F32) * ya).astype(o_ref.dtype)


def _conf_call(src, gates, conv, prev, lnw, lnb, w, *, row0, n_rows, tm, name):
    assert n_rows % tm == 0 and row0 % tm == 0 and src.shape[0] == n_rows
    n_tiles = n_rows // tm
    conv_in_kernel = conv is not None
    hist = CONV_WIDTH - 1
    blk0 = row0 // tm
    in_specs = [pl.BlockSpec((tm, CONV_DIM), lambda i: (i, 0))]
    operands = [src]
    blocks = [_nbytes((tm, CONV_DIM), BF16), _nbytes((tm, D_MODEL), BF16) * 2, _nbytes((8, CONV_DIM), F32) * 2,
              _nbytes((CONV_DIM, D_MODEL), BF16)]
    scratch_dims = [((tm, CONV_DIM), BF16)]
    out_specs = [pl.BlockSpec((tm, D_MODEL), lambda i: (blk0 + i, 0))]
    out_shape = [jax.ShapeDtypeStruct((gates.shape[0], D_MODEL), BF16)]
    if conv_in_kernel:
        assert tm % CONV_HALO == 0
        in_specs += [
            pl.BlockSpec((1, hist, CONV_DIM), lambda i: (i, 0, 0)),
            pl.BlockSpec((CONV_WIDTH, CONV_DIM), lambda i: (0, 0)),
            pl.BlockSpec((1, CONV_DIM), lambda i: (0, 0)),
        ]
        operands += list(conv)
        blocks += [_nbytes((32, CONV_DIM), F32) * 3, _nbytes((8, CONV_DIM), F32)]
        scratch_dims = [((CONV_HALO + tm, CONV_DIM), F32), ((SUBLANES, CONV_HALO + tm, V7X_LANES), F32),
                        ((tm, CONV_DIM), F32)] + scratch_dims
        out_specs.append(pl.BlockSpec((1, hist, CONV_DIM), lambda i: (i, 0, 0)))
        out_shape.append(jax.ShapeDtypeStruct((n_tiles, hist, CONV_DIM), F32))
    in_specs += [
        pl.BlockSpec((tm, D_MODEL), lambda i: (blk0 + i, 0)),
        pl.BlockSpec((1, CONV_DIM), lambda i: (0, 0)),
        pl.BlockSpec((1, CONV_DIM), lambda i: (0, 0)),
        pl.BlockSpec((CONV_DIM, D_MODEL), lambda i: (0, 0)),
    ]
    operands += [gates, lnw, lnb, w]
    aliases = {}
    if prev is not None:
        in_specs.append(pl.BlockSpec(memory_space=pl.ANY))
        operands.append(prev)
        aliases = {len(operands) - 1: 0}
    return pl.pallas_call(
        functools.partial(_conf_kernel, tm=tm, conv_in_kernel=conv_in_kernel, aliased=prev is not None),
        grid=(n_tiles,),
        in_specs=in_specs,
        out_specs=out_specs,
        out_shape=out_shape,
        scratch_shapes=[pltpu.VMEM(shape, dt) for shape, dt in scratch_dims],
        input_output_aliases=aliases,
        compiler_params=pltpu.CompilerParams(
            dimension_semantics=("arbitrary",),
            vmem_limit_bytes=_vmem_limit(blocks, sum(_nbytes(sh, dt) for sh, dt in scratch_dims))),
        name=name,
    )(*operands)


GC_ROW_TILE = 512
GC_COL_TILE = 2048
GC_SUB_COLS = 256


def _glu_conv_kernel(h_ref, wv_ref, wg_ref, dww_ref, dwb_ref, y_ref, tail_ref, carry_ref, xs_ref, zs_ref,
                     *, tiles_per_seq):
    i = pl.program_id(0)
    j = pl.program_id(1)
    tm = h_ref.shape[0]

    @pl.when((i % tiles_per_seq) == 0)
    def _():
        tile_cols = pl.ds(pl.multiple_of(j * GC_COL_TILE, GC_COL_TILE), GC_COL_TILE)
        carry_ref[:, tile_cols] = jnp.zeros((CONV_HALO, GC_COL_TILE), F32)

    lhs = h_ref[...]
    for n in range(GC_COL_TILE // GC_SUB_COLS):
        sub = slice(n * GC_SUB_COLS, (n + 1) * GC_SUB_COLS)
        gcols = pl.ds(pl.multiple_of(j * GC_COL_TILE + n * GC_SUB_COLS, GC_SUB_COLS), GC_SUB_COLS)
        xs = xs_ref.at[n % 2]
        val = jnp.dot(lhs, wv_ref[:, sub], preferred_element_type=F32)
        gate = jnp.dot(lhs, wg_ref[:, sub], preferred_element_type=F32)
        xs[CONV_HALO:CONV_HALO + tm, :] = val * _sigmoid(gate)
        xs[0:CONV_HALO, :] = carry_ref[:, gcols]
        last = xs[tm:tm + CONV_HALO, :]
        carry_ref[:, gcols] = last
        tail_ref[0, :, sub] = last
        for half in range(GC_SUB_COLS // V7X_LANES):
            lanes = slice(half * V7X_LANES, (half + 1) * V7X_LANES)
            cols = slice(n * GC_SUB_COLS + half * V7X_LANES, n * GC_SUB_COLS + (half + 1) * V7X_LANES)

            def store(r0, rb, acc, cols=cols):
                y_ref[r0:r0 + rb, cols] = acc.astype(y_ref.dtype)

            _conv31_block(xs, zs_ref.at[half], dww_ref, dwb_ref, cols, lanes, tm, store)


def _glu_conv_call(h, w, wv_col0, wg_col0, dww, dwb, *, n_seq, seq_len):
    k = h.shape[1]
    n = dww.shape[1]
    tm, tn = GC_ROW_TILE, GC_COL_TILE
    assert seq_len % tm == 0 and n % tn == 0 and wv_col0 % tn == 0 and wg_col0 % tn == 0
    wv_blk0, wg_blk0 = wv_col0 // tn, wg_col0 // tn
    tiles_per_seq = seq_len // tm
    blocks = [_nbytes((tm, k), BF16), _nbytes((k, tn), BF16) * 2, _nbytes((32, tn), F32) * 2,
              _nbytes((tm, tn), BF16), _nbytes((CONV_HALO, tn), F32)]
    scratch_dims = [((CONV_HALO, n), F32), ((2, CONV_HALO + tm, GC_SUB_COLS), F32),
                    ((GC_SUB_COLS // V7X_LANES, SUBLANES, CONV_HALO + tm, V7X_LANES), F32)]
    return pl.pallas_call(
        functools.partial(_glu_conv_kernel, tiles_per_seq=tiles_per_seq),
        grid=(n_seq * tiles_per_seq, n // tn),
        in_specs=[
            pl.BlockSpec((tm, k), lambda i, j: (i, 0)),
            pl.BlockSpec((k, tn), lambda i, j: (0, j + wv_blk0)),
            pl.BlockSpec((k, tn), lambda i, j: (0, j + wg_blk0)),
            pl.BlockSpec((CONV_WIDTH, tn), lambda i, j: (0, j)),
            pl.BlockSpec((1, tn), lambda i, j: (0, j)),
        ],
        out_specs=[
            pl.BlockSpec((tm, tn), lambda i, j: (i, j)),
            pl.BlockSpec((1, CONV_HALO, tn), lambda i, j: (i, 0, j)),
        ],
        out_shape=[
            jax.ShapeDtypeStruct((n_seq * seq_len, n), BF16),
            jax.ShapeDtypeStruct((n_seq * tiles_per_seq, CONV_HALO, n), F32),
        ],
        scratch_shapes=[pltpu.VMEM(shape, dt) for shape, dt in scratch_dims],
        compiler_params=pltpu.CompilerParams(
            dimension_semantics=("arbitrary", "arbitrary"),
            vmem_limit_bytes=_vmem_limit(blocks, sum(_nbytes(sh, dt) for sh, dt in scratch_dims))),
        name="in_proj_glu_conv",
    )(h, w, w, dww, dwb)


SSM_CONV_COL_BLOCK = 512
GROUP_COLS = HEADS_PER_GROUP * SSM_HEAD_DIM
B_COL0 = SSM_INNER
C_COL0 = SSM_INNER + SSM_GROUPS * SSM_STATE
NEG_BIG = -1e30
LOG2_E = 1.4426950408889634


def _split3(x):
    p1 = x.astype(BF16)
    r1 = x - p1.astype(F32)
    p2 = r1.astype(BF16)
    r2 = r1 - p2.astype(F32)
    return p1, p2, r2.astype(BF16)


def _ssd_kernel(*refs, ql, n_chunks, has_h0, conv_in_kernel, aliased):
    it = iter(refs)
    src_ref = next(it)
    if conv_in_kernel:
        hist_ref = next(it)
        cw_ref = next(it)
        cb_ref = next(it)
    dt_ref = next(it)
    z_ref = next(it)
    alog_ref = next(it)
    dskip_ref = next(it)
    nw_ref = next(it)
    h0_ref = next(it) if has_h0 else None
    if aliased:
        next(it)
    y_ref = next(it)
    cache_ref = next(it) if conv_in_kernel else None
    hout_ref = next(it)
    xs_ref = next(it) if conv_in_kernel else None
    act_ref = next(it)
    acs_ref = next(it)
    rowt_ref = next(it)
    wendt_ref = next(it)
    ht_ref = next(it)

    q = SSD_CHUNK
    c_idx = pl.program_id(1)
    hist = SSM_CONV_WIDTH - 1

    @pl.when(c_idx == 0)
    def _():
        for g in range(SSM_GROUPS):
            if has_h0:
                ht_ref[g] = h0_ref[0, g * GROUP_COLS:(g + 1) * GROUP_COLS, :].T
            else:
                ht_ref[g] = jnp.zeros((SSM_STATE, GROUP_COLS), F32)

    if conv_in_kernel:
        assert n_chunks == 1
        xs_ref[0:SSM_HALO - hist, :] = jnp.zeros((SSM_HALO - hist, SSM_XBC_DIM), F32)
        xs_ref[SSM_HALO - hist:SSM_HALO, :] = hist_ref[0]
        xs_ref[SSM_HALO:SSM_HALO + ql, :] = src_ref[...].astype(F32)
        if ql < q:
            xs_ref[SSM_HALO + ql:SSM_HALO + q, :] = jnp.zeros((q - ql, SSM_XBC_DIM), F32)
        base = SSM_HALO - hist
        row_ok = lax.broadcasted_iota(jnp.int32, (q, SSM_CONV_COL_BLOCK), 0) < ql

        def conv_cols(cb, carry):
            c0 = pl.multiple_of(cb * SSM_CONV_COL_BLOCK, SSM_CONV_COL_BLOCK)
            cols = pl.ds(c0, SSM_CONV_COL_BLOCK)
            acc = jnp.broadcast_to(cb_ref[:, cols], (q, SSM_CONV_COL_BLOCK))
            for k in range(SSM_CONV_WIDTH):
                acc = acc + cw_ref[k:k + 1, cols] * xs_ref[base + k:base + k + q, cols]
            act = acc * _sigmoid(acc)
            if ql < q:
                act = jnp.where(row_ok, act, 0.0)
            act_ref[:, cols] = act
            return carry

        lax.fori_loop(0, SSM_XBC_DIM // SSM_CONV_COL_BLOCK, conv_cols, 0)
    else:
        assert ql == q
        act_ref[...] = src_ref[...].astype(F32)

    dt = dt_ref[...]
    if ql < q:
        dt = jnp.concatenate([dt, jnp.zeros((q - ql, V7X_LANES), F32)], axis=0)
    a_neg = -jnp.exp(alog_ref[...])
    dta = dt * a_neg
    r_i = lax.broadcasted_iota(jnp.int32, (q, q), 0)
    c_i = lax.broadcasted_iota(jnp.int32, (q, q), 1)
    causal = r_i >= c_i
    tri = jnp.where(causal, 1.0, 0.0).astype(BF16)
    p1, p2, p3 = _split3(dta)
    acs = (jnp.dot(tri, p1, preferred_element_type=F32) + jnp.dot(tri, p2, preferred_element_type=F32)
           + jnp.dot(tri, p3, preferred_element_type=F32))
    acs2 = acs * LOG2_E
    acs_ref[...] = acs2
    acs2_t = acs2.T
    rowt = acs2_t - jnp.log2(dt.T)
    rowt_ref[...] = rowt
    wendt_ref[...] = jnp.exp2(acs2_t[:, q - 1:q] - rowt)

    left = c_i < SSM_HEAD_DIM
    left_row = left[0:1, :]

    for g in range(SSM_GROUPS):
        bg = act_ref[:, B_COL0 + g * SSM_STATE:B_COL0 + (g + 1) * SSM_STATE]
        cg = act_ref[:, C_COL0 + g * SSM_STATE:C_COL0 + (g + 1) * SSM_STATE].astype(BF16)
        cbm = lax.dot_general(cg, bg.astype(BF16), (((1,), (1,)), ((), ())),
                              preferred_element_type=F32)
        bg_t = bg.T
        ht_g = ht_ref[g]
        yoff = jnp.dot(cg, ht_g.astype(BF16), preferred_element_type=F32)
        y_parts, inc_parts, dec_parts = [], [], []
        for j in range(HEADS_PER_GROUP // 2):
            h_a = g * HEADS_PER_GROUP + 2 * j
            h_b = h_a + 1
            lo = h_a * SSM_HEAD_DIM
            xp = act_ref[:, lo:lo + V7X_LANES]
            lhs_parts = []
            cols = []
            for h in (h_a, h_b):
                col = jnp.broadcast_to(acs_ref[:, h:h + 1], (q, q))
                cols.append(col)
                seg = jnp.where(causal, col - rowt_ref[h:h + 1, :], NEG_BIG)
                m_h = cbm * jnp.exp2(seg)
                w_h = bg_t * wendt_ref[h:h + 1, :]
                lhs_parts.append(jnp.concatenate([m_h, w_h], axis=0).astype(BF16))
            rhs = jnp.concatenate([jnp.where(left, xp, 0.0).astype(BF16),
                                   jnp.where(left, 0.0, xp).astype(BF16)], axis=0)
            res = jnp.dot(jnp.concatenate(lhs_parts, axis=1), rhs, preferred_element_type=F32)
            colsel = jnp.where(left, cols[0], cols[1])
            y_parts.append(res[0:q] + jnp.exp2(colsel) * yoff[:, j * V7X_LANES:(j + 1) * V7X_LANES]
                           + dskip_ref[:, lo:lo + V7X_LANES] * xp)
            inc_parts.append(res[q:2 * q])
            last_a = jnp.broadcast_to(acs_ref[q - 1:q, h_a:h_a + 1], (1, V7X_LANES))
            last_b = jnp.broadcast_to(acs_ref[q - 1:q, h_b:h_b + 1], (1, V7X_LANES))
            dec_parts.append(jnp.exp2(jnp.where(left_row, last_a, last_b)))
        ht_ref[g] = jnp.concatenate(dec_parts, axis=1) * ht_g + jnp.concatenate(inc_parts, axis=1)
        gc = slice(g * GROUP_COLS, (g + 1) * GROUP_COLS)
        yg = jnp.concatenate(y_parts, axis=1)[0:ql]
        zg = z_ref[:, gc].astype(F32)
        yz = yg * (zg * _sigmoid(zg))
        ms = jnp.mean(yz * yz, axis=-1, keepdims=True)
        y_ref[:, gc] = (yz * lax.rsqrt(ms + EPS) * nw_ref[:, gc]).astype(y_ref.dtype)

    @pl.when(c_idx == n_chunks - 1)
    def _():
        if conv_in_kernel:
            cache_ref[0] = xs_ref[SSM_HALO + ql - hist:SSM_HALO + ql, :]
        for g in range(SSM_GROUPS):
            hout_ref[0, g * GROUP_COLS:(g + 1) * GROUP_COLS, :] = ht_ref[g].T


def _ssd_call(src, dt, z, conv, h0, prev, alog, dskip, nw, *, row0, n_seq, seq_len, ql, name):
    assert seq_len % ql == 0 and row0 % ql == 0 and ql % SSM_HALO == 0 and ql <= SSD_CHUNK
    assert src.shape[0] == n_seq * seq_len
    n_chunks = seq_len // ql
    conv_in_kernel = conv is not None
    has_h0 = h0 is not None
    hist = SSM_CONV_WIDTH - 1
    blk0 = row0 // ql
    q = SSD_CHUNK

    def row_blk(b, c):
        return blk0 + b * n_chunks + c

    in_specs = [pl.BlockSpec((ql, SSM_XBC_DIM), lambda b, c: (b * n_chunks + c, 0))]
    operands = [src]
    if conv_in_kernel:
        in_specs += [
            pl.BlockSpec((1, hist, SSM_XBC_DIM), lambda b, c: (b, 0, 0)),
            pl.BlockSpec((SSM_CONV_WIDTH, SSM_XBC_DIM), lambda b, c: (0, 0)),
            pl.BlockSpec((1, SSM_XBC_DIM), lambda b, c: (0, 0)),
        ]
        operands += list(conv)
    in_specs += [
        pl.BlockSpec((ql, V7X_LANES), lambda b, c: (row_blk(b, c), 0)),
        pl.BlockSpec((ql, SSM_INNER), lambda b, c: (row_blk(b, c), 0)),
        pl.BlockSpec((1, V7X_LANES), lambda b, c: (0, 0)),
        pl.BlockSpec((1, SSM_INNER), lambda b, c: (0, 0)),
        pl.BlockSpec((1, SSM_INNER), lambda b, c: (0, 0)),
    ]
    operands += [dt, z, alog, dskip, nw]
    if has_h0:
        in_specs.append(pl.BlockSpec((1, SSM_INNER, SSM_STATE), lambda b, c: (b, 0, 0)))
        operands.append(h0)
    aliases = {}
    if prev is not None:
        in_specs.append(pl.BlockSpec(memory_space=pl.ANY))
        operands.append(prev)
        aliases = {len(operands) - 1: 0}
    state_bytes = _nbytes((SSM_INNER, SSM_STATE), F32)
    blocks = [_nbytes((ql, SSM_XBC_DIM), BF16), _nbytes((SSM_HALO, SSM_XBC_DIM), F32),
              _nbytes((ql, V7X_LANES), F32), _nbytes((ql, SSM_INNER), BF16),
              _nbytes((8, SSM_XBC_DIM), F32) * 2, _nbytes((8, SSM_INNER), F32) * 2,
              _nbytes((ql, SSM_INNER), BF16), _nbytes((8, SSM_XBC_DIM), F32),
              state_bytes * (2 if has_h0 else 1)]
    scratch_dims = ([((SSM_HALO + q, SSM_XBC_DIM), F32)] if conv_in_kernel else []) + [
        ((q, SSM_XBC_DIM), F32),
        ((q, V7X_LANES), F32),
        ((V7X_LANES, q), F32),
        ((V7X_LANES, q), F32),
        ((SSM_GROUPS, SSM_STATE, GROUP_COLS), F32),
    ]
    out_specs = [pl.BlockSpec((ql, SSM_INNER), lambda b, c: (row_blk(b, c), 0))]
    out_shape = [jax.ShapeDtypeStruct((dt.shape[0], SSM_INNER), BF16)]
    if conv_in_kernel:
        out_specs.append(pl.BlockSpec((1, hist, SSM_XBC_DIM), lambda b, c: (b, 0, 0)))
        out_shape.append(jax.ShapeDtypeStruct((n_seq, hist, SSM_XBC_DIM), F32))
    out_specs.append(pl.BlockSpec((1, SSM_INNER, SSM_STATE), lambda b, c: (b, 0, 0)))
    out_shape.append(jax.ShapeDtypeStruct((n_seq, SSM_INNER, SSM_STATE), F32))
    return pl.pallas_call(
        functools.partial(_ssd_kernel, ql=ql, n_chunks=n_chunks, has_h0=has_h0,
                          conv_in_kernel=conv_in_kernel, aliased=prev is not None),
        grid=(n_seq, n_chunks),
        in_specs=in_specs,
        out_specs=out_specs,
        out_shape=out_shape,
        scratch_shapes=[pltpu.VMEM(shape, dt_) for shape, dt_ in scratch_dims],
        input_output_aliases=aliases,
        compiler_params=pltpu.CompilerParams(
            dimension_semantics=("arbitrary", "arbitrary"),
            vmem_limit_bytes=_vmem_limit(blocks, sum(_nbytes(sh, dt_) for sh, dt_ in scratch_dims))),
        name=name,
    )(*operands)


def _outproj_kernel(m_ref, w_ref, xp_ref, xs_ref, o_ref, *, n_prompt_tiles):
    i = pl.program_id(0)
    acc = jnp.dot(m_ref[...], w_ref[...], preferred_element_type=F32)

    @pl.when(i < n_prompt_tiles)
    def _():
        o_ref[...] = xp_ref[...] + acc

    @pl.when(i >= n_prompt_tiles)
    def _():
        o_ref[...] = xs_ref[...] + acc


def _outproj_call(m, w, xp, xs):
    t, d = m.shape
    npt = xp.shape[0] // ROW_TILE
    blocks = [_nbytes((ROW_TILE, d), BF16), _nbytes((d, d), BF16)] + [_nbytes((ROW_TILE, d), F32)] * 3
    return pl.pallas_call(
        functools.partial(_outproj_kernel, n_prompt_tiles=npt),
        grid=(t // ROW_TILE,),
        in_specs=[
            pl.BlockSpec((ROW_TILE, d), lambda i: (i, 0)),
            pl.BlockSpec((d, d), lambda i: (0, 0)),
            pl.BlockSpec((ROW_TILE, d), lambda i: (jnp.minimum(i, npt - 1), 0)),
            pl.BlockSpec((ROW_TILE, d), lambda i: (0, 0)),
        ],
        out_specs=pl.BlockSpec((ROW_TILE, d), lambda i: (i, 0)),
        out_shape=jax.ShapeDtypeStruct((t, d), F32),
        compiler_params=pltpu.CompilerParams(
            dimension_semantics=("arbitrary",), vmem_limit_bytes=_vmem_limit(blocks)),
        name="out_proj_residual",
    )(m, w, xp, xs)


MLP_FF_TILE = 1024


def _mlp_kernel(x1_ref, nw_ref, wup_ref, wdn_ref, fw_ref, yp_ref, ys_ref, h2_ref, acc_ref,
                *, n_prompt_tiles, n_ff):
    i = pl.program_id(0)
    f = pl.program_id(1)

    @pl.when(f == 0)
    def _():
        h2_ref[...] = _rms(x1_ref[...], nw_ref[...]).astype(BF16)
        acc_ref[...] = jnp.zeros_like(acc_ref)

    hid = jnp.dot(h2_ref[...], wup_ref[...], preferred_element_type=F32)
    hid = jnp.square(jnp.maximum(hid, 0.0))
    acc_ref[...] += jnp.dot(hid.astype(BF16), wdn_ref[...], preferred_element_type=F32)

    @pl.when(f == n_ff - 1)
    def _():
        out = _rms(x1_ref[...] + acc_ref[...], fw_ref[...])

        @pl.when(i < n_prompt_tiles)
        def _():
            yp_ref[...] = out

        @pl.when(i >= n_prompt_tiles)
        def _():
            ys_ref[...] = out


def _mlp_call(x1, nw, wup, wdn, fw, *, n_prompt_rows):
    t, d = x1.shape
    npt = n_prompt_rows // ROW_TILE
    n_ff = D_FF // MLP_FF_TILE
    blocks = [_nbytes((ROW_TILE, d), F32), _nbytes((d, MLP_FF_TILE), BF16),
              _nbytes((MLP_FF_TILE, d), BF16), _nbytes((ROW_TILE, d), F32) * 2]
    scratch_bytes = _nbytes((ROW_TILE, d), BF16) + _nbytes((ROW_TILE, d), F32)
    return pl.pallas_call(
        functools.partial(_mlp_kernel, n_prompt_tiles=npt, n_ff=n_ff),
        grid=(t // ROW_TILE, n_ff),
        in_specs=[
            pl.BlockSpec((ROW_TILE, d), lambda i, f: (i, 0)),
            pl.BlockSpec((1, d), lambda i, f: (0, 0)),
            pl.BlockSpec((d, MLP_FF_TILE), lambda i, f: (0, f)),
            pl.BlockSpec((MLP_FF_TILE, d), lambda i, f: (f, 0)),
            pl.BlockSpec((1, d), lambda i, f: (0, 0)),
        ],
        out_specs=[
            pl.BlockSpec((ROW_TILE, d), lambda i, f: (jnp.minimum(i, npt - 1), 0)),
            pl.BlockSpec((ROW_TILE, d), lambda i, f: (0, 0)),
        ],
        out_shape=[
            jax.ShapeDtypeStruct((n_prompt_rows, d), F32),
            jax.ShapeDtypeStruct((t - n_prompt_rows, d), F32),
        ],
        scratch_shapes=[pltpu.VMEM((ROW_TILE, d), BF16), pltpu.VMEM((ROW_TILE, d), F32)],
        compiler_params=pltpu.CompilerParams(
            dimension_semantics=("arbitrary", "arbitrary"),
            vmem_limit_bytes=_vmem_limit(blocks, scratch_bytes)),
        name="mlp_final_norm",
    )(x1, nw, wup, wdn, fw)


def kernel(x_prompt, x_sample, cache_conv, cache_ssm_conv, state_ssm, mix_norm_w, w_in, conv_dw_w,
           conv_dw_b, conv_ln_w, conv_ln_b, w_conv_out, ssm_conv_w, ssm_conv_b, dt_bias, a_log, d_skip,
           ssm_norm_w, w_ssm_out, w_out, mlp_norm_w, w_up, w_down, final_norm_w):
    bp, lp, d = x_prompt.shape
    bs, ls, _ = x_sample.shape
    tp, ts = bp * lp, bs * ls
    xp = x_prompt.reshape(tp, d)
    xs = x_sample.reshape(ts, d)

    wi = _wprep_call(w_in.reshape(w_in.shape[1:]))
    dt_b =jnp.pad(dt_bias[0], (0, V7X_LANES - SSM_HEADS)).reshape(1, V7X_LANES)
    alog = jnp.pad(a_log[0], (0, V7X_LANES - SSM_HEADS)).reshape(1, V7X_LANES)
    dskip = jnp.repeat(d_skip[0], SSM_HEAD_DIM).reshape(1, SSM_INNER)
    row = lambda v: v.reshape(1, -1)

    h = _norm_call(xp, xs, row(mix_norm_w[0]))

    tm, tn = MM_ROW_TILE, 1024
    dww, dwb = conv_dw_w[0], row(conv_dw_b[0])
    yconv_p, a_tail_p = _glu_conv_call(h, wi, WCOL_VAL, WCOL_GATE, dww, dwb, n_seq=bp, seq_len=lp)
    h_s = h[tp:]
    a_s = _mm_call(h_s, [(wi, WCOL_VAL), (wi, WCOL_GATE)], CONV_DIM, _ep_glu, BF16, tm=ts, tn=tn // 2,
                   name="in_proj_glu_sample")
    z = _mm_call(h, [(wi, WCOL_Z)], SSM_INNER, _ep_identity, BF16, tm=tm, tn=tn, name="in_proj_z")
    cw, cb = ssm_conv_w[0], row(ssm_conv_b[0])
    act_p, xbc_tail_p = _xbc_conv_call(h, wi, WCOL_XBC, cw, cb, n_seq=bp, seq_len=lp)
    xbc_s = _mm_call(h_s, [(wi, WCOL_XBC)], SSM_XBC_DIM, _ep_identity, BF16, tm=ts, tn=tn,
                     name="in_proj_xbc_sample")
    dt = _mm_call(h, [(wi, WCOL_DT)], V7X_LANES, _ep_softplus_bias, F32, tm=tm, tn=V7X_LANES, params=[dt_b],
                  name="in_proj_dt")
    gates = _mm_call(h, [(wi, WCOL_G)], 2 * D_MODEL, _ep_sigmoid, BF16, tm=tm, tn=tn, name="in_proj_gates")

    conf_args = (row(conv_ln_w[0]), row(conv_ln_b[0]), w_conv_out[0].astype(BF16))
    (m_a,) = _conf_call(yconv_p, gates, None, None, *conf_args, row0=0, n_rows=tp, tm=ROW_TILE,
                        name="conformer_prompt")
    gc_tiles = lp // GC_ROW_TILE
    conv_p = a_tail_p[gc_tiles - 1::gc_tiles, CONV_HALO - (CONV_WIDTH - 1):, :]
    m_a, conv_s = _conf_call(a_s, gates, (cache_conv[0], dww, dwb), m_a, *conf_args, row0=tp, n_rows=ts,
                             tm=ls, name="conformer_sample")

    ssd_args = (alog, dskip, row(ssm_norm_w[0]))
    yn, ssm_p = _ssd_call(act_p, dt, z, None, None, None, *ssd_args, row0=0, n_seq=bp, seq_len=lp,
                          ql=SSD_CHUNK, name="ssd_prompt")
    tiles_per_seq = lp // XBC_ROW_TILE
    xbc_p = xbc_tail_p[tiles_per_seq - 1::tiles_per_seq, SUBLANES - (SSM_CONV_WIDTH - 1):, :]
    h0 = state_ssm[0].reshape(bs, SSM_INNER, SSM_STATE)
    yn, xbc_s, ssm_s = _ssd_call(xbc_s, dt, z, (cache_ssm_conv[0], cw, cb), h0, yn, *ssd_args, row0=tp,
                                 n_seq=bs, seq_len=ls, ql=ls, name="ssd_sample")

    m = _mm_call(yn, [(w_ssm_out[0].astype(BF16), 0)], D_MODEL, _ep_merge, BF16, tm=tm, tn=tn // 2,
                 extras=[(m_a, 0), (gates, D_MODEL)], name="ssm_out_merge")
    x1 = _outproj_call(m, w_out[0].astype(BF16), xp, xs)
    y_p, y_s = _mlp_call(x1, row(mlp_norm_w[0]), w_up[0].astype(BF16), w_down[0].astype(BF16),
                         row(final_norm_w), n_prompt_rows=tp)

    hshape = (SSM_HEADS, SSM_HEAD_DIM, SSM_STATE)
    return (y_p.reshape(bp, lp, d), y_s.reshape(bs, ls, d),
            conv_p[None], xbc_p[None], ssm_p.reshape(1, bp, *hshape),
            conv_s[None], xbc_s[None], ssm_s.reshape(1, bs, *hshape))
```

```python
import functools

import jax
import jax.numpy as jnp
from jax import lax
from jax.experimental import pallas as pl
from jax.experimental.pallas import tpu as pltpu

F32 = jnp.float32
BF16 = jnp.bfloat16

D_MODEL = 2048
CONV_DIM = D_MODEL
CONV_WIDTH = 31
SSM_INNER = 2 * D_MODEL
SSM_HEAD_DIM = 64
SSM_HEADS = SSM_INNER // SSM_HEAD_DIM
SSM_GROUPS = 8
HEADS_PER_GROUP = SSM_HEADS // SSM_GROUPS
SSM_STATE = 128
SSM_CONV_WIDTH = 4
SSM_XBC_DIM = SSM_INNER + 2 * SSM_GROUPS * SSM_STATE
D_FF = 4 * D_MODEL
EPS = 1e-5

V7X_LANES = 128
V7X_VMEM_BYTES = 64 * 1024 * 1024
V7X_VMEM_REQUEST_CAP = 56 * 1024 * 1024
COMPILER_TEMP_BYTES = 12 * 1024 * 1024

SSD_CHUNK = 128
CONV_HALO = 32
SSM_HALO = 16
ROW_TILE = 512
MM_ROW_TILE = 1536


def _vmem_limit(block_bytes, scratch_bytes=0):
    need = 2 * sum(block_bytes) + scratch_bytes + COMPILER_TEMP_BYTES
    return int(min(need, V7X_VMEM_REQUEST_CAP))


def _nbytes(shape, dtype):
    n = 1
    for s in shape:
        n *= s
    return n * jnp.dtype(dtype).itemsize


def _sigmoid(x):
    return 1.0 / (1.0 + jnp.exp(-x))


def _softplus(x):
    return jnp.maximum(x, 0.0) + jnp.log1p(jnp.exp(-jnp.abs(x)))


def _rms(x, w):
    ms = jnp.mean(x * x, axis=-1, keepdims=True)
    return x * lax.rsqrt(ms + EPS) * w


def _norm_kernel(xp_ref, xs_ref, w_ref, o_ref, *, n_prompt_tiles):
    i = pl.program_id(0)

    @pl.when(i < n_prompt_tiles)
    def _():
        o_ref[...] = _rms(xp_ref[...], w_ref[...]).astype(o_ref.dtype)

    @pl.when(i >= n_prompt_tiles)
    def _():
        o_ref[...] = _rms(xs_ref[...], w_ref[...]).astype(o_ref.dtype)


def _norm_call(xp, xs, w):
    tp, d = xp.shape
    ts = xs.shape[0]
    assert ts == ROW_TILE and tp % ROW_TILE == 0
    npt = tp // ROW_TILE
    blocks = [_nbytes((ROW_TILE, d), F32)] * 2 + [_nbytes((ROW_TILE, d), BF16)]
    return pl.pallas_call(
        functools.partial(_norm_kernel, n_prompt_tiles=npt),
        grid=(npt + 1,),
        in_specs=[
            pl.BlockSpec((ROW_TILE, d), lambda i: (jnp.minimum(i, npt - 1), 0)),
            pl.BlockSpec((ROW_TILE, d), lambda i: (0, 0)),
            pl.BlockSpec((1, d), lambda i: (0, 0)),
        ],
        out_specs=pl.BlockSpec((ROW_TILE, d), lambda i: (i, 0)),
        out_shape=jax.ShapeDtypeStruct((tp + ts, d), BF16),
        compiler_params=pltpu.CompilerParams(
            dimension_semantics=("arbitrary",), vmem_limit_bytes=_vmem_limit(blocks)),
        name="mix_rmsnorm",
    )(xp, xs, w)


WPREP_COLS = 512
WCOL_VAL = 0
WCOL_GATE = CONV_DIM
WCOL_Z = 2 * CONV_DIM
WCOL_XBC = WCOL_Z + SSM_INNER
WCOL_G = WCOL_XBC + SSM_XBC_DIM
WCOL_DT = WCOL_G + 2 * D_MODEL
WPREP_TOTAL = WCOL_DT + WPREP_COLS


def _wprep_kernel(a_ref, b_ref, o_ref, *, n_plain, n_shift):
    j = pl.program_id(0)

    @pl.when(j < n_plain)
    def _():
        o_ref[...] = a_ref[...].T.astype(BF16)

    @pl.when(jnp.logical_and(j >= n_plain, j < n_plain + n_shift))
    def _():
        src = jnp.concatenate([a_ref[SSM_HEADS:, :], b_ref[:SSM_HEADS, :]], axis=0)
        o_ref[...] = src.T.astype(BF16)

    @pl.when(j == n_plain + n_shift)
    def _():
        r = lax.broadcasted_iota(jnp.int32, a_ref.shape, 0)
        o_ref[...] = jnp.where(r < SSM_HEADS, a_ref[...], 0.0).T.astype(BF16)


def _wprep_call(w_in_t):
    n_src, k = w_in_t.shape
    assert n_src == WCOL_G + SSM_HEADS + 2 * D_MODEL
    n_plain = WCOL_G // WPREP_COLS
    n_shift = 2 * D_MODEL // WPREP_COLS
    last_src = (n_src - 1) // WPREP_COLS
    blocks = [_nbytes((WPREP_COLS, k), F32)] * 3 + [_nbytes((k, WPREP_COLS), BF16)]
    return pl.pallas_call(
        functools.partial(_wprep_kernel, n_plain=n_plain, n_shift=n_shift),
        grid=(n_plain + n_shift + 1,),
        in_specs=[
            pl.BlockSpec((WPREP_COLS, k), lambda j: (jnp.where(j == n_plain + n_shift, n_plain, j), 0)),
            pl.BlockSpec((WPREP_COLS, k),
                         lambda j: (jnp.where(j < n_plain, n_plain, jnp.minimum(j + 1, last_src)), 0)),
        ],
        out_specs=pl.BlockSpec((k, WPREP_COLS), lambda j: (0, j)),
        out_shape=jax.ShapeDtypeStruct((k, WPREP_TOTAL), BF16),
        compiler_params=pltpu.CompilerParams(
            dimension_semantics=("arbitrary",), vmem_limit_bytes=_vmem_limit(blocks)),
        name="w_in_prep",
    )(w_in_t, w_in_t)


def _mm_kernel(*refs, n_w, n_ex, n_p, epilogue):
    lhs_ref = refs[0]
    w_refs = refs[1:1 + n_w]
    ex_refs = refs[1 + n_w:1 + n_w + n_ex]
    p_refs = refs[1 + n_w + n_ex:1 + n_w + n_ex + n_p]
    o_ref = refs[-1]
    lhs = lhs_ref[...]
    accs = [jnp.dot(lhs, w[...], preferred_element_type=F32) for w in w_refs]
    out = epilogue(accs, [e[...].astype(F32) for e in ex_refs], [p[...] for p in p_refs])
    o_ref[...] = out.astype(o_ref.dtype)


def _mm_call(lhs, weights, n, epilogue, out_dtype, *, tm, tn, extras=(), params=(), name):
    t, k = lhs.shape
    assert t % tm == 0 and n % tn == 0
    in_specs = [pl.BlockSpec((tm, k), lambda i, j: (i, 0))]
    operands = [lhs]
    blocks = [_nbytes((tm, k), lhs.dtype)]
    for w, col0 in weights:
        assert col0 % tn == 0
        in_specs.append(pl.BlockSpec((k, tn), lambda i, j, o=col0 // tn: (0, j + o)))
        operands.append(w)
        blocks.append(_nbytes((k, tn), w.dtype))
    for arr, col0 in extras:
        assert col0 % tn == 0
        in_specs.append(pl.BlockSpec((tm, tn), lambda i, j, o=col0 // tn: (i, j + o)))
        operands.append(arr)
        blocks.append(_nbytes((tm, tn), arr.dtype))
    for p in params:
        in_specs.append(pl.BlockSpec((1, tn), lambda i, j: (0, j)))
        operands.append(p)
        blocks.append(_nbytes((8, tn), p.dtype))
    blocks.append(_nbytes((tm, tn), out_dtype))
    return pl.pallas_call(
        functools.partial(_mm_kernel, n_w=len(weights), n_ex=len(extras), n_p=len(params),
                          epilogue=epilogue),
        grid=(t // tm, n // tn),
        in_specs=in_specs,
        out_specs=pl.BlockSpec((tm, tn), lambda i, j: (i, j)),
        out_shape=jax.ShapeDtypeStruct((t, n), out_dtype),
        compiler_params=pltpu.CompilerParams(
            dimension_semantics=("arbitrary", "arbitrary"), vmem_limit_bytes=_vmem_limit(blocks)),
        name=name,
    )(*operands)


def _ep_glu(accs, extras, params):
    return accs[0] * _sigmoid(accs[1])


def _ep_identity(accs, extras, params):
    return accs[0]


def _ep_sigmoid(accs, extras, params):
    return _sigmoid(accs[0])


def _ep_softplus_bias(accs, extras, params):
    return _softplus(accs[0] + params[0])


def _ep_merge(accs, extras, params):
    return extras[0] + extras[1] * accs[0]


XBC_ROW_TILE = 1024
XBC_COL_TILE = 2048
XBC_SUB_COLS = 256
SUBLANES = 8


def _xbc_conv_kernel(h_ref, w_ref, cw_ref, cb_ref, act_ref, tail_ref, carry_ref, stage_ref, *, tiles_per_seq):
    i = pl.program_id(0)
    j = pl.program_id(1)
    tm = h_ref.shape[0]
    hist = SSM_CONV_WIDTH - 1

    @pl.when((i % tiles_per_seq) == 0)
    def _():
        tile_cols = pl.ds(pl.multiple_of(j * XBC_COL_TILE, XBC_COL_TILE), XBC_COL_TILE)
        carry_ref[:, tile_cols] = jnp.zeros((SUBLANES, XBC_COL_TILE), F32)

    lhs = h_ref[...]
    for n in range(XBC_COL_TILE // XBC_SUB_COLS):
        sub = slice(n * XBC_SUB_COLS, (n + 1) * XBC_SUB_COLS)
        gcols = pl.ds(pl.multiple_of(j * XBC_COL_TILE + n * XBC_SUB_COLS, XBC_SUB_COLS), XBC_SUB_COLS)
        st = stage_ref.at[n % 2]
        st[SUBLANES:SUBLANES + tm, :] = jnp.dot(lhs, w_ref[:, sub], preferred_element_type=F32)
        st[0:SUBLANES, :] = carry_ref[:, gcols]
        last = st[tm:tm + SUBLANES, :]
        carry_ref[:, gcols] = last
        tail_ref[0, :, sub] = last
        conv = cb_ref[:, sub] + cw_ref[hist:hist + 1, sub] * st[SUBLANES:SUBLANES + tm, :]
        for k in range(hist):
            conv = conv + cw_ref[k:k + 1, sub] * st[SUBLANES - hist + k:SUBLANES - hist + k + tm, :]
        act_ref[:, sub] = (conv * _sigmoid(conv)).astype(act_ref.dtype)


def _xbc_conv_call(h, w, w_col0, cw, cb, *, n_seq, seq_len):
    k = h.shape[1]
    n = cw.shape[1]
    tm, tn = XBC_ROW_TILE, XBC_COL_TILE
    assert seq_len % tm == 0 and n % tn == 0 and w_col0 % tn == 0
    w_blk0 = w_col0 // tn
    tiles_per_seq = seq_len // tm
    stage_shape = (2, tm + SUBLANES, XBC_SUB_COLS)
    blocks = [_nbytes((tm, k), BF16), _nbytes((k, tn), BF16), _nbytes((8, tn), F32) * 3,
              _nbytes((tm, tn), BF16)]
    return pl.pallas_call(
        functools.partial(_xbc_conv_kernel, tiles_per_seq=tiles_per_seq),
        grid=(n_seq * tiles_per_seq, n // tn),
        in_specs=[
            pl.BlockSpec((tm, k), lambda i, j: (i, 0)),
            pl.BlockSpec((k, tn), lambda i, j: (0, j + w_blk0)),
            pl.BlockSpec((SSM_CONV_WIDTH, tn), lambda i, j: (0, j)),
            pl.BlockSpec((1, tn), lambda i, j: (0, j)),
        ],
        out_specs=[
            pl.BlockSpec((tm, tn), lambda i, j: (i, j)),
            pl.BlockSpec((1, SUBLANES, tn), lambda i, j: (i, 0, j)),
        ],
        out_shape=[
            jax.ShapeDtypeStruct((n_seq * seq_len, n), BF16),
            jax.ShapeDtypeStruct((n_seq * tiles_per_seq, SUBLANES, n), F32),
        ],
        scratch_shapes=[pltpu.VMEM((SUBLANES, n), F32), pltpu.VMEM(stage_shape, F32)],
        compiler_params=pltpu.CompilerParams(
            dimension_semantics=("arbitrary", "arbitrary"),
            vmem_limit_bytes=_vmem_limit(blocks, _nbytes((SUBLANES, n), F32) + _nbytes(stage_shape, F32))),
        name="in_proj_xbc_conv",
    )(h, w, cw, cb)


CONV_ROW_BLOCK = 64
LN_ROW_BLOCK = 256


def _conv31_block(src, zs, dww_ref, dwb_ref, cols, lanes, tm, store):
    hist = CONV_WIDTH - 1
    base = CONV_HALO - hist
    rb = min(CONV_ROW_BLOCK, tm)
    for s in range(SUBLANES):
        rows = tm + SUBLANES * ((hist - s) // SUBLANES)
        zs[s, 0:rows, :] = src[base + s:base + s + rows, lanes]
    bias = jnp.broadcast_to(dwb_ref[:, cols], (rb, V7X_LANES))
    for r0 in range(0, tm, rb):
        acc = bias
        for k in range(CONV_WIDTH):
            s, q = k % SUBLANES, k // SUBLANES
            acc = acc + dww_ref[k:k + 1, cols] * zs[s, r0 + SUBLANES * q:r0 + SUBLANES * q + rb, :]
        store(r0, rb, acc)


def _conf_kernel(*refs, tm, n_tiles, n_fill, conv_in_kernel, aliased):
    if n_fill == 0:
        _conf_body(refs, tm, conv_in_kernel, aliased)
        return
    i = pl.program_id(0)
    o_ref = refs[5 + 3 * int(conv_in_kernel) + int(aliased)]

    @pl.when(i < n_tiles)
    def _():
        _conf_body(refs, tm, conv_in_kernel, aliased)

    @pl.when(i >= n_tiles)
    def _():
        o_ref[...] = jnp.zeros_like(o_ref)


def _conf_body(refs, tm, conv_in_kernel, aliased):
    it = iter(refs)
    src_ref = next(it)
    if conv_in_kernel:
        hist_ref = next(it)
        dww_ref = next(it)
        dwb_ref = next(it)
    g_ref = next(it)
    lnw_ref = next(it)
    lnb_ref = next(it)
    w_ref = next(it)
    if aliased:
        next(it)
    o_ref = next(it)
    if conv_in_kernel:
        cache_ref = next(it)
        xs_ref = next(it)
        zs_ref = next(it)
        y_ref = next(it)
    s_ref = next(it)
    hist = CONV_WIDTH - 1

    if conv_in_kernel:
        xs_ref[0:CONV_HALO - hist, :] = jnp.zeros((CONV_HALO - hist, CONV_DIM), F32)
        xs_ref[CONV_HALO - hist:CONV_HALO, :] = hist_ref[0]
        xs_ref[CONV_HALO:CONV_HALO + tm, :] = src_ref[...].astype(F32)
        cache_ref[0] = xs_ref[CONV_HALO + tm - hist:CONV_HALO + tm, :]

        def conv_cols(cb, carry):
            cols = pl.ds(pl.multiple_of(cb * V7X_LANES, V7X_LANES), V7X_LANES)

            def store(r0, rb, acc):
                y_ref[r0:r0 + rb, cols] = acc

            _conv31_block(xs_ref, zs_ref, dww_ref, dwb_ref, cols, cols, tm, store)
            return carry

        lax.fori_loop(0, CONV_DIM // V7X_LANES, conv_cols, 0)

    ln_rows_per_step = min(LN_ROW_BLOCK, tm)

    def ln_rows(rb, carry):
        r0 = pl.multiple_of(rb * ln_rows_per_step, ln_rows_per_step)
        if conv_in_kernel:
            y = y_ref[pl.ds(r0, ln_rows_per_step), :]
        else:
            y = src_ref[pl.ds(r0, ln_rows_per_step), :].astype(F32)
        mu = jnp.mean(y, axis=-1, keepdims=True)
        yc = y - mu
        var = jnp.mean(yc * yc, axis=-1, keepdims=True)
        yn = yc * lax.rsqrt(var + EPS) * lnw_ref[...] + lnb_ref[...]
        s_ref[pl.ds(r0, ln_rows_per_step), :] = (yn * _sigmoid(yn)).astype(BF16)
        return carry

    lax.fori_loop(0, tm // ln_rows_per_step, ln_rows, 0)

    ya = jnp.dot(s_ref[...], w_ref[...], preferred_element_type=F32)
    o_ref[...] = (g_ref[...].astype(F32) * ya).astype(o_ref.dtype)


def _conf_call(src, gates, conv, prev, lnw, lnb, w, *, row0, n_rows, tm, name, n_fill=0):
    assert n_rows % tm == 0 and row0 % tm == 0 and src.shape[0] == n_rows
    n_tiles = n_rows // tm
    conv_in_kernel = conv is not None
    assert not (conv_in_kernel and n_fill)
    hist = CONV_WIDTH - 1
    blk0 = row0 // tm

    def tile(i):
        return jnp.minimum(i, n_tiles - 1)

    in_specs = [pl.BlockSpec((tm, CONV_DIM), lambda i: (tile(i), 0))]
    operands = [src]
    blocks = [_nbytes((tm, CONV_DIM), BF16), _nbytes((tm, D_MODEL), BF16) * 2, _nbytes((8, CONV_DIM), F32) * 2,
              _nbytes((CONV_DIM, D_MODEL), BF16)]
    scratch_dims = [((tm, CONV_DIM), BF16)]
    out_specs = [pl.BlockSpec((tm, D_MODEL), lambda i: (blk0 + i, 0))]
    out_shape = [jax.ShapeDtypeStruct((gates.shape[0], D_MODEL), BF16)]
    if conv_in_kernel:
        assert tm % CONV_HALO == 0
        in_specs += [
            pl.BlockSpec((1, hist, CONV_DIM), lambda i: (i, 0, 0)),
            pl.BlockSpec((CONV_WIDTH, CONV_DIM), lambda i: (0, 0)),
            pl.BlockSpec((1, CONV_DIM), lambda i: (0, 0)),
        ]
        operands += list(conv)
        blocks += [_nbytes((32, CONV_DIM), F32) * 3, _nbytes((8, CONV_DIM), F32)]
        scratch_dims = [((CONV_HALO + tm, CONV_DIM), F32), ((SUBLANES, CONV_HALO + tm, V7X_LANES), F32),
                        ((tm, CONV_DIM), F32)] + scratch_dims
        out_specs.append(pl.BlockSpec((1, hist, CONV_DIM), lambda i: (i, 0, 0)))
        out_shape.append(jax.ShapeDtypeStruct((n_tiles, hist, CONV_DIM), F32))
    in_specs += [
        pl.BlockSpec((tm, D_MODEL), lambda i: (blk0 + tile(i), 0)),
        pl.BlockSpec((1, CONV_DIM), lambda i: (0, 0)),
        pl.BlockSpec((1, CONV_DIM), lambda i: (0, 0)),
        pl.BlockSpec((CONV_DIM, D_MODEL), lambda i: (0, 0)),
    ]
    operands += [gates, lnw, lnb, w]
    aliases = {}
    if prev is not None:
        in_specs.append(pl.BlockSpec(memory_space=pl.ANY))
        operands.append(prev)
        aliases = {len(operands) - 1: 0}
    return pl.pallas_call(
        functools.partial(_conf_kernel, tm=tm, n_tiles=n_tiles, n_fill=n_fill, conv_in_kernel=conv_in_kernel,
                          aliased=prev is not None),
        grid=(n_tiles + n_fill,),
        in_specs=in_specs,
        out_specs=out_specs,
        out_shape=out_shape,
        scratch_shapes=[pltpu.VMEM(shape, dt) for shape, dt in scratch_dims],
        input_output_aliases=aliases,
        compiler_params=pltpu.CompilerParams(
            dimension_semantics=("arbitrary",),
            vmem_limit_bytes=_vmem_limit(blocks, sum(_nbytes(sh, dt) for sh, dt in scratch_dims))),
        name=name,
    )(*operands)


GC_ROW_TILE = 512
GC_COL_TILE = 2048
GC_SUB_COLS = 256


def _glu_conv_kernel(h_ref, wv_ref, wg_ref, dww_ref, dwb_ref, y_ref, tail_ref, carry_ref, xs_ref, zs_ref,
                     *, tiles_per_seq):
    i = pl.program_id(0)
    j = pl.program_id(1)
    tm = h_ref.shape[0]

    @pl.when((i % tiles_per_seq) == 0)
    def _():
        tile_cols = pl.ds(pl.multiple_of(j * GC_COL_TILE, GC_COL_TILE), GC_COL_TILE)
        carry_ref[:, tile_cols] = jnp.zeros((CONV_HALO, GC_COL_TILE), F32)

    lhs = h_ref[...]
    for n in range(GC_COL_TILE // GC_SUB_COLS):
        sub = slice(n * GC_SUB_COLS, (n + 1) * GC_SUB_COLS)
        gcols = pl.ds(pl.multiple_of(j * GC_COL_TILE + n * GC_SUB_COLS, GC_SUB_COLS), GC_SUB_COLS)
        xs = xs_ref.at[n % 2]
        val = jnp.dot(lhs, wv_ref[:, sub], preferred_element_type=F32)
        gate = jnp.dot(lhs, wg_ref[:, sub], preferred_element_type=F32)
        xs[CONV_HALO:CONV_HALO + tm, :] = val * _sigmoid(gate)
        xs[0:CONV_HALO, :] = carry_ref[:, gcols]
        last = xs[tm:tm + CONV_HALO, :]
        carry_ref[:, gcols] = last
        tail_ref[0, :, sub] = last
        for half in range(GC_SUB_COLS // V7X_LANES):
            lanes = slice(half * V7X_LANES, (half + 1) * V7X_LANES)
            cols = slice(n * GC_SUB_COLS + half * V7X_LANES, n * GC_SUB_COLS + (half + 1) * V7X_LANES)

            def store(r0, rb, acc, cols=cols):
                y_ref[r0:r0 + rb, cols] = acc.astype(y_ref.dtype)

            _conv31_block(xs, zs_ref.at[half], dww_ref, dwb_ref, cols, lanes, tm, store)


def _glu_conv_call(h, w, wv_col0, wg_col0, dww, dwb, *, n_seq, seq_len):
    k = h.shape[1]
    n = dww.shape[1]
    tm, tn = GC_ROW_TILE, GC_COL_TILE
    assert seq_len % tm == 0 and n % tn == 0 and wv_col0 % tn == 0 and wg_col0 % tn == 0
    wv_blk0, wg_blk0 = wv_col0 // tn, wg_col0 // tn
    tiles_per_seq = seq_len // tm
    blocks = [_nbytes((tm, k), BF16), _nbytes((k, tn), BF16) * 2, _nbytes((32, tn), F32) * 2,
              _nbytes((tm, tn), BF16), _nbytes((CONV_HALO, tn), F32)]
    scratch_dims = [((CONV_HALO, n), F32), ((2, CONV_HALO + tm, GC_SUB_COLS), F32),
                    ((GC_SUB_COLS // V7X_LANES, SUBLANES, CONV_HALO + tm, V7X_LANES), F32)]
    return pl.pallas_call(
        functools.partial(_glu_conv_kernel, tiles_per_seq=tiles_per_seq),
        grid=(n_seq * tiles_per_seq, n // tn),
        in_specs=[
            pl.BlockSpec((tm, k), lambda i, j: (i, 0)),
            pl.BlockSpec((k, tn), lambda i, j: (0, j + wv_blk0)),
            pl.BlockSpec((k, tn), lambda i, j: (0, j + wg_blk0)),
            pl.BlockSpec((CONV_WIDTH, tn), lambda i, j: (0, j)),
            pl.BlockSpec((1, tn), lambda i, j: (0, j)),
        ],
        out_specs=[
            pl.BlockSpec((tm, tn), lambda i, j: (i, j)),
            pl.BlockSpec((1, CONV_HALO, tn), lambda i, j: (i, 0, j)),
        ],
        out_shape=[
            jax.ShapeDtypeStruct((n_seq * seq_len, n), BF16),
            jax.ShapeDtypeStruct((n_seq * tiles_per_seq, CONV_HALO, n), F32),
        ],
        scratch_shapes=[pltpu.VMEM(shape, dt) for shape, dt in scratch_dims],
        compiler_params=pltpu.CompilerParams(
            dimension_semantics=("arbitrary", "arbitrary"),
            vmem_limit_bytes=_vmem_limit(blocks, sum(_nbytes(sh, dt) for sh, dt in scratch_dims))),
        name="in_proj_glu_conv",
    )(h, w, w, dww, dwb)


SSM_CONV_COL_BLOCK = 512
GROUP_COLS = HEADS_PER_GROUP * SSM_HEAD_DIM
B_COL0 = SSM_INNER
C_COL0 = SSM_INNER + SSM_GROUPS * SSM_STATE
NEG_BIG = -1e30
LOG2_E = 1.4426950408889634


def _split3(x):
    p1 = x.astype(BF16)
    r1 = x - p1.astype(F32)
    p2 = r1.astype(BF16)
    r2 = r1 - p2.astype(F32)
    return p1, p2, r2.astype(BF16)


def _ssd_kernel(*refs, ql, n_chunks, fill_per_seq, has_h0, conv_in_kernel, aliased):
    if fill_per_seq == 0:
        _ssd_body(refs, ql, n_chunks, has_h0, conv_in_kernel, aliased)
        return
    c_idx = pl.program_id(1)
    y_ref = refs[6 + 3 * int(conv_in_kernel) + int(has_h0) + int(aliased)]

    @pl.when(c_idx < n_chunks)
    def _():
        _ssd_body(refs, ql, n_chunks, has_h0, conv_in_kernel, aliased)

    @pl.when(c_idx >= n_chunks)
    def _():
        y_ref[...] = jnp.zeros_like(y_ref)


def _ssd_body(refs, ql, n_chunks, has_h0, conv_in_kernel, aliased):
    it = iter(refs)
    src_ref = next(it)
    if conv_in_kernel:
        hist_ref = next(it)
        cw_ref = next(it)
        cb_ref = next(it)
    dt_ref = next(it)
    z_ref = next(it)
    alog_ref = next(it)
    dskip_ref = next(it)
    nw_ref = next(it)
    h0_ref = next(it) if has_h0 else None
    if aliased:
        next(it)
    y_ref = next(it)
    cache_ref = next(it) if conv_in_kernel else None
    hout_ref = next(it)
    xs_ref = next(it) if conv_in_kernel else None
    act_ref = next(it)
    acs_ref = next(it)
    rowt_ref = next(it)
    wendt_ref = next(it)
    ht_ref = next(it)

    q = SSD_CHUNK
    c_idx = pl.program_id(1)
    hist = SSM_CONV_WIDTH - 1

    @pl.when(c_idx == 0)
    def _():
        for g in range(SSM_GROUPS):
            if has_h0:
                ht_ref[g] = h0_ref[0, g * GROUP_COLS:(g + 1) * GROUP_COLS, :].T
            else:
                ht_ref[g] = jnp.zeros((SSM_STATE, GROUP_COLS), F32)

    if conv_in_kernel:
        assert n_chunks == 1
        xs_ref[0:SSM_HALO - hist, :] = jnp.zeros((SSM_HALO - hist, SSM_XBC_DIM), F32)
        xs_ref[SSM_HALO - hist:SSM_HALO, :] = hist_ref[0]
        xs_ref[SSM_HALO:SSM_HALO + ql, :] = src_ref[...].astype(F32)
        if ql < q:
            xs_ref[SSM_HALO + ql:SSM_HALO + q, :] = jnp.zeros((q - ql, SSM_XBC_DIM), F32)
        base = SSM_HALO - hist
        row_ok = lax.broadcasted_iota(jnp.int32, (q, SSM_CONV_COL_BLOCK), 0) < ql

        def conv_cols(cb, carry):
            c0 = pl.multiple_of(cb * SSM_CONV_COL_BLOCK, SSM_CONV_COL_BLOCK)
            cols = pl.ds(c0, SSM_CONV_COL_BLOCK)
            acc = jnp.broadcast_to(cb_ref[:, cols], (q, SSM_CONV_COL_BLOCK))
            for k in range(SSM_CONV_WIDTH):
                acc = acc + cw_ref[k:k + 1, cols] * xs_ref[base + k:base + k + q, cols]
            act = acc * _sigmoid(acc)
            if ql < q:
                act = jnp.where(row_ok, act, 0.0)
            act_ref[:, cols] = act
            return carry

        lax.fori_loop(0, SSM_XBC_DIM // SSM_CONV_COL_BLOCK, conv_cols, 0)
    else:
        assert ql == q
        act_ref[...] = src_ref[...].astype(F32)

    dt = dt_ref[...]
    if ql < q:
        dt = jnp.concatenate([dt, jnp.zeros((q - ql, V7X_LANES), F32)], axis=0)
    a_neg = -jnp.exp(alog_ref[...])
    dta = dt * a_neg
    r_i = lax.broadcasted_iota(jnp.int32, (q, q), 0)
    c_i = lax.broadcasted_iota(jnp.int32, (q, q), 1)
    causal = r_i >= c_i
    tri = jnp.where(causal, 1.0, 0.0).astype(BF16)
    p1, p2, p3 = _split3(dta)
    acs = (jnp.dot(tri, p1, preferred_element_type=F32) + jnp.dot(tri, p2, preferred_element_type=F32)
           + jnp.dot(tri, p3, preferred_element_type=F32))
    acs2 = acs * LOG2_E
    acs_ref[...] = acs2
    acs2_t = acs2.T
    rowt = acs2_t - jnp.log2(dt.T)
    rowt_ref[...] = rowt
    wendt_ref[...] = jnp.exp2(acs2_t[:, q - 1:q] - rowt)

    left = c_i < SSM_HEAD_DIM
    left_row = left[0:1, :]

    for g in range(SSM_GROUPS):
        bg = act_ref[:, B_COL0 + g * SSM_STATE:B_COL0 + (g + 1) * SSM_STATE]
        cg = act_ref[:, C_COL0 + g * SSM_STATE:C_COL0 + (g + 1) * SSM_STATE].astype(BF16)
        cbm = lax.dot_general(cg, bg.astype(BF16), (((1,), (1,)), ((), ())),
                              preferred_element_type=F32)
        bg_t = bg.T
        ht_g = ht_ref[g]
        yoff = jnp.dot(cg, ht_g.astype(BF16), preferred_element_type=F32)
        y_parts, inc_parts, dec_parts = [], [], []
        for j in range(HEADS_PER_GROUP // 2):
            h_a = g * HEADS_PER_GROUP + 2 * j
            h_b = h_a + 1
            lo = h_a * SSM_HEAD_DIM
            xp = act_ref[:, lo:lo + V7X_LANES]
            lhs_parts = []
            cols = []
            for h in (h_a, h_b):
                col = jnp.broadcast_to(acs_ref[:, h:h + 1], (q, q))
                cols.append(col)
                seg = jnp.where(causal, col - rowt_ref[h:h + 1, :], NEG_BIG)
                m_h = cbm * jnp.exp2(seg)
                w_h = bg_t * wendt_ref[h:h + 1, :]
                lhs_parts.append(jnp.concatenate([m_h, w_h], axis=0).astype(BF16))
            rhs = jnp.concatenate([jnp.where(left, xp, 0.0).astype(BF16),
                                   jnp.where(left, 0.0, xp).astype(BF16)], axis=0)
            res = jnp.dot(jnp.concatenate(lhs_parts, axis=1), rhs, preferred_element_type=F32)
            colsel = jnp.where(left, cols[0], cols[1])
            y_parts.append(res[0:q] + jnp.exp2(colsel) * yoff[:, j * V7X_LANES:(j + 1) * V7X_LANES]
                           + dskip_ref[:, lo:lo + V7X_LANES] * xp)
            inc_parts.append(res[q:2 * q])
            last_a = jnp.broadcast_to(acs_ref[q - 1:q, h_a:h_a + 1], (1, V7X_LANES))
            last_b = jnp.broadcast_to(acs_ref[q - 1:q, h_b:h_b + 1], (1, V7X_LANES))
            dec_parts.append(jnp.exp2(jnp.where(left_row, last_a, last_b)))
        ht_ref[g] = jnp.concatenate(dec_parts, axis=1) * ht_g + jnp.concatenate(inc_parts, axis=1)
        gc = slice(g * GROUP_COLS, (g + 1) * GROUP_COLS)
        yg = jnp.concatenate(y_parts, axis=1)[0:ql]
        zg = z_ref[:, gc].astype(F32)
        yz = yg * (zg * _sigmoid(zg))
        ms = jnp.mean(yz * yz, axis=-1, keepdims=True)
        y_ref[:, gc] = (yz * lax.rsqrt(ms + EPS) * nw_ref[:, gc]).astype(y_ref.dtype)

    @pl.when(c_idx == n_chunks - 1)
    def _():
        if conv_in_kernel:
            cache_ref[0] = xs_ref[SSM_HALO + ql - hist:SSM_HALO + ql, :]
        for g in range(SSM_GROUPS):
            hout_ref[0, g * GROUP_COLS:(g + 1) * GROUP_COLS, :] = ht_ref[g].T


def _ssd_call(src, dt, z, conv, h0, prev, alog, dskip, nw, *, row0, n_seq, seq_len, ql, name, n_fill=0):
    assert seq_len % ql == 0 and row0 % ql == 0 and ql % SSM_HALO == 0 and ql <= SSD_CHUNK
    assert src.shape[0] == n_seq * seq_len
    n_chunks = seq_len // ql
    conv_in_kernel = conv is not None
    has_h0 = h0 is not None
    hist = SSM_CONV_WIDTH - 1
    blk0 = row0 // ql
    q = SSD_CHUNK

    assert n_fill % n_seq == 0
    fill_per_seq = n_fill // n_seq

    def chunk(c):
        return jnp.minimum(c, n_chunks - 1)

    def row_blk(b, c):
        return blk0 + b * n_chunks + chunk(c)

    def out_blk(b, c):
        fill_blk = blk0 + n_seq * n_chunks + b * fill_per_seq + (c - n_chunks)
        return jnp.where(c < n_chunks, row_blk(b, c), fill_blk)

    in_specs = [pl.BlockSpec((ql, SSM_XBC_DIM), lambda b, c: (b * n_chunks + chunk(c), 0))]
    operands = [src]
    if conv_in_kernel:
        in_specs += [
            pl.BlockSpec((1, hist, SSM_XBC_DIM), lambda b, c: (b, 0, 0)),
            pl.BlockSpec((SSM_CONV_WIDTH, SSM_XBC_DIM), lambda b, c: (0, 0)),
            pl.BlockSpec((1, SSM_XBC_DIM), lambda b, c: (0, 0)),
        ]
        operands += list(conv)
    in_specs += [
        pl.BlockSpec((ql, V7X_LANES), lambda b, c: (row_blk(b, c), 0)),
        pl.BlockSpec((ql, SSM_INNER), lambda b, c: (row_blk(b, c), 0)),
        pl.BlockSpec((1, V7X_LANES), lambda b, c: (0, 0)),
        pl.BlockSpec((1, SSM_INNER), lambda b, c: (0, 0)),
        pl.BlockSpec((1, SSM_INNER), lambda b, c: (0, 0)),
    ]
    operands += [dt, z, alog, dskip, nw]
    if has_h0:
        in_specs.append(pl.BlockSpec((1, SSM_INNER, SSM_STATE), lambda b, c: (b, 0, 0)))
        operands.append(h0)
    aliases = {}
    if prev is not None:
        in_specs.append(pl.BlockSpec(memory_space=pl.ANY))
        operands.append(prev)
        aliases = {len(operands) - 1: 0}
    state_bytes = _nbytes((SSM_INNER, SSM_STATE), F32)
    blocks = [_nbytes((ql, SSM_XBC_DIM), BF16), _nbytes((SSM_HALO, SSM_XBC_DIM), F32),
              _nbytes((ql, V7X_LANES), F32), _nbytes((ql, SSM_INNER), BF16),
              _nbytes((8, SSM_XBC_DIM), F32) * 2, _nbytes((8, SSM_INNER), F32) * 2,
              _nbytes((ql, SSM_INNER), BF16), _nbytes((8, SSM_XBC_DIM), F32),
              state_bytes * (2 if has_h0 else 1)]
    scratch_dims = ([((SSM_HALO + q, SSM_XBC_DIM), F32)] if conv_in_kernel else []) + [
        ((q, SSM_XBC_DIM), F32),
        ((q, V7X_LANES), F32),
        ((V7X_LANES, q), F32),
        ((V7X_LANES, q), F32),
        ((SSM_GROUPS, SSM_STATE, GROUP_COLS), F32),
    ]
    out_specs = [pl.BlockSpec((ql, SSM_INNER), lambda b, c: (out_blk(b, c), 0))]
    out_shape = [jax.ShapeDtypeStruct((dt.shape[0], SSM_INNER), BF16)]
    if conv_in_kernel:
        out_specs.append(pl.BlockSpec((1, hist, SSM_XBC_DIM), lambda b, c: (b, 0, 0)))
        out_shape.append(jax.ShapeDtypeStruct((n_seq, hist, SSM_XBC_DIM), F32))
    out_specs.append(pl.BlockSpec((1, SSM_INNER, SSM_STATE), lambda b, c: (b, 0, 0)))
    out_shape.append(jax.ShapeDtypeStruct((n_seq, SSM_INNER, SSM_STATE), F32))
    return pl.pallas_call(
        functools.partial(_ssd_kernel, ql=ql, n_chunks=n_chunks, fill_per_seq=fill_per_seq, has_h0=has_h0,
                          conv_in_kernel=conv_in_kernel, aliased=prev is not None),
        grid=(n_seq, n_chunks + fill_per_seq),
        in_specs=in_specs,
        out_specs=out_specs,
        out_shape=out_shape,
        scratch_shapes=[pltpu.VMEM(shape, dt_) for shape, dt_ in scratch_dims],
        input_output_aliases=aliases,
        compiler_params=pltpu.CompilerParams(
            dimension_semantics=("arbitrary", "arbitrary"),
            vmem_limit_bytes=_vmem_limit(blocks, sum(_nbytes(sh, dt_) for sh, dt_ in scratch_dims))),
        name=name,
    )(*operands)


def _outproj_kernel(m_ref, w_ref, xp_ref, xs_ref, o_ref, *, n_prompt_tiles):
    i = pl.program_id(0)
    acc = jnp.dot(m_ref[...], w_ref[...], preferred_element_type=F32)

    @pl.when(i < n_prompt_tiles)
    def _():
        o_ref[...] = xp_ref[...] + acc

    @pl.when(i >= n_prompt_tiles)
    def _():
        o_ref[...] = xs_ref[...] + acc


def _outproj_call(m, w, xp, xs):
    t, d = m.shape
    npt = xp.shape[0] // ROW_TILE
    blocks = [_nbytes((ROW_TILE, d), BF16), _nbytes((d, d), BF16)] + [_nbytes((ROW_TILE, d), F32)] * 3
    return pl.pallas_call(
        functools.partial(_outproj_kernel, n_prompt_tiles=npt),
        grid=(t // ROW_TILE,),
        in_specs=[
            pl.BlockSpec((ROW_TILE, d), lambda i: (i, 0)),
            pl.BlockSpec((d, d), lambda i: (0, 0)),
            pl.BlockSpec((ROW_TILE, d), lambda i: (jnp.minimum(i, npt - 1), 0)),
            pl.BlockSpec((ROW_TILE, d), lambda i: (0, 0)),
        ],
        out_specs=pl.BlockSpec((ROW_TILE, d), lambda i: (i, 0)),
        out_shape=jax.ShapeDtypeStruct((t, d), F32),
        compiler_params=pltpu.CompilerParams(
            dimension_semantics=("arbitrary",), vmem_limit_bytes=_vmem_limit(blocks)),
        name="out_proj_residual",
    )(m, w, xp, xs)


MLP_FF_TILE = 1024


def _mlp_kernel(x1_ref, nw_ref, wup_ref, wdn_ref, fw_ref, yp_ref, ys_ref, h2_ref, acc_ref,
                *, n_prompt_tiles, n_ff):
    i = pl.program_id(0)
    f = pl.program_id(1)

    @pl.when(f == 0)
    def _():
        h2_ref[...] = _rms(x1_ref[...], nw_ref[...]).astype(BF16)
        acc_ref[...] = jnp.zeros_like(acc_ref)

    hid = jnp.dot(h2_ref[...], wup_ref[...], preferred_element_type=F32)
    hid = jnp.square(jnp.maximum(hid, 0.0))
    acc_ref[...] += jnp.dot(hid.astype(BF16), wdn_ref[...], preferred_element_type=F32)

    @pl.when(f == n_ff - 1)
    def _():
        out = _rms(x1_ref[...] + acc_ref[...], fw_ref[...])

        @pl.when(i < n_prompt_tiles)
        def _():
            yp_ref[...] = out

        @pl.when(i >= n_prompt_tiles)
        def _():
            ys_ref[...] = out


def _mlp_call(x1, nw, wup, wdn, fw, *, n_prompt_rows):
    t, d = x1.shape
    npt = n_prompt_rows // ROW_TILE
    n_ff = D_FF // MLP_FF_TILE
    blocks = [_nbytes((ROW_TILE, d), F32), _nbytes((d, MLP_FF_TILE), BF16),
              _nbytes((MLP_FF_TILE, d), BF16), _nbytes((ROW_TILE, d), F32) * 2]
    scratch_bytes = _nbytes((ROW_TILE, d), BF16) + _nbytes((ROW_TILE, d), F32)
    return pl.pallas_call(
        functools.partial(_mlp_kernel, n_prompt_tiles=npt, n_ff=n_ff),
        grid=(t // ROW_TILE, n_ff),
        in_specs=[
            pl.BlockSpec((ROW_TILE, d), lambda i, f: (i, 0)),
            pl.BlockSpec((1, d), lambda i, f: (0, 0)),
            pl.BlockSpec((d, MLP_FF_TILE), lambda i, f: (0, f)),
            pl.BlockSpec((MLP_FF_TILE, d), lambda i, f: (f, 0)),
            pl.BlockSpec((1, d), lambda i, f: (0, 0)),
        ],
        out_specs=[
            pl.BlockSpec((ROW_TILE, d), lambda i, f: (jnp.minimum(i, npt - 1), 0)),
            pl.BlockSpec((ROW_TILE, d), lambda i, f: (0, 0)),
        ],
        out_shape=[
            jax.ShapeDtypeStruct((n_prompt_rows, d), F32),
            jax.ShapeDtypeStruct((t - n_prompt_rows, d), F32),
        ],
        scratch_shapes=[pltpu.VMEM((ROW_TILE, d), BF16), pltpu.VMEM((ROW_TILE, d), F32)],
        compiler_params=pltpu.CompilerParams(
            dimension_semantics=("arbitrary", "arbitrary"),
            vmem_limit_bytes=_vmem_limit(blocks, scratch_bytes)),
        name="mlp_final_norm",
    )(x1, nw, wup, wdn, fw)


def kernel(x_prompt, x_sample, cache_conv, cache_ssm_conv, state_ssm, mix_norm_w, w_in, conv_dw_w,
           conv_dw_b, conv_ln_w, conv_ln_b, w_conv_out, ssm_conv_w, ssm_conv_b, dt_bias, a_log, d_skip,
           ssm_norm_w, w_ssm_out, w_out, mlp_norm_w, w_up, w_down, final_norm_w):
    bp, lp, d = x_prompt.shape
    bs, ls, _ = x_sample.shape
    tp, ts = bp * lp, bs * ls
    xp = x_prompt.reshape(tp, d)
    xs = x_sample.reshape(ts, d)

    wi = _wprep_call(w_in[0].T)
    dt_b =jnp.pad(dt_bias[0], (0, V7X_LANES - SSM_HEADS)).reshape(1, V7X_LANES)
    alog = jnp.pad(a_log[0], (0, V7X_LANES - SSM_HEADS)).reshape(1, V7X_LANES)
    dskip = jnp.repeat(d_skip[0], SSM_HEAD_DIM).reshape(1, SSM_INNER)
    row = lambda v: v.reshape(1, -1)

    h = _norm_call(xp, xs, row(mix_norm_w[0]))

    tm, tn = MM_ROW_TILE, 1024
    dww, dwb = conv_dw_w[0], row(conv_dw_b[0])
    yconv_p, a_tail_p = _glu_conv_call(h, wi, WCOL_VAL, WCOL_GATE, dww, dwb, n_seq=bp, seq_len=lp)
    h_s = h[tp:]
    a_s = _mm_call(h_s, [(wi, WCOL_VAL), (wi, WCOL_GATE)], CONV_DIM, _ep_glu, BF16, tm=ts, tn=tn // 2,
                   name="in_proj_glu_sample")
    z = _mm_call(h, [(wi, WCOL_Z)], SSM_INNER, _ep_identity, BF16, tm=tm, tn=tn, name="in_proj_z")
    cw, cb = ssm_conv_w[0], row(ssm_conv_b[0])
    act_p, xbc_tail_p = _xbc_conv_call(h, wi, WCOL_XBC, cw, cb, n_seq=bp, seq_len=lp)
    xbc_s = _mm_call(h_s, [(wi, WCOL_XBC)], SSM_XBC_DIM, _ep_identity, BF16, tm=ts, tn=tn,
                     name="in_proj_xbc_sample")
    dt = _mm_call(h, [(wi, WCOL_DT)], V7X_LANES, _ep_softplus_bias, F32, tm=tm, tn=V7X_LANES, params=[dt_b],
                  name="in_proj_dt")
    gates = _mm_call(h, [(wi, WCOL_G)], 2 * D_MODEL, _ep_sigmoid, BF16, tm=tm, tn=tn, name="in_proj_gates")

    conf_args = (row(conv_ln_w[0]), row(conv_ln_b[0]), w_conv_out[0].astype(BF16))
    (m_a,) = _conf_call(yconv_p, gates, None, None, *conf_args, row0=0, n_rows=tp, tm=ROW_TILE,
                        n_fill=ts // ROW_TILE, name="conformer_prompt")
    gc_tiles = lp // GC_ROW_TILE
    conv_p = a_tail_p[gc_tiles - 1::gc_tiles, CONV_HALO - (CONV_WIDTH - 1):, :]
    m_a, conv_s = _conf_call(a_s, gates, (cache_conv[0], dww, dwb), m_a, *conf_args, row0=tp, n_rows=ts,
                             tm=ls, name="conformer_sample")

    ssd_args = (alog, dskip, row(ssm_norm_w[0]))
    yn, ssm_p = _ssd_call(act_p, dt, z, None, None, None, *ssd_args, row0=0, n_seq=bp, seq_len=lp,
                          ql=SSD_CHUNK, n_fill=ts // SSD_CHUNK, name="ssd_prompt")
    tiles_per_seq = lp // XBC_ROW_TILE
    xbc_p = xbc_tail_p[tiles_per_seq - 1::tiles_per_seq, SUBLANES - (SSM_CONV_WIDTH - 1):, :]
    h0 = state_ssm[0].reshape(bs, SSM_INNER, SSM_STATE)
    yn, xbc_s, ssm_s = _ssd_call(xbc_s, dt, z, (cache_ssm_conv[0], cw, cb), h0, yn, *ssd_args, row0=tp,
                                 n_seq=bs, seq_len=ls, ql=ls, name="ssd_sample")

    m = _mm_call(yn, [(w_ssm_out[0].astype(BF16), 0)], D_MODEL, _ep_merge, BF16, tm=tm, tn=tn // 2,
                 extras=[(m_a, 0), (gates, D_MODEL)], name="ssm_out_merge")
    x1 = _outproj_call(m, w_out[0].astype(BF16), xp, xs)
    y_p, y_s = _mlp_call(x1, row(mlp_norm_w[0]), w_up[0].astype(BF16), w_down[0].astype(BF16),
                         row(final_norm_w), n_prompt_rows=tp)

    hshape = (SSM_HEADS, SSM_HEAD_DIM, SSM_STATE)
    return (y_p.reshape(bp, lp, d), y_s.reshape(bs, ls, d),
            conv_p[None], xbc_p[None], ssm_p.reshape(1, bp, *hshape),
            conv_s[None], xbc_s[None], ssm_s.reshape(1, bs, *hshape))
```

```python
import functools

import jax
import jax.numpy as jnp
from jax import lax
from jax.experimental import pallas as pl
from jax.experimental.pallas import tpu as pltpu

F32 = jnp.float32
BF16 = jnp.bfloat16

D_MODEL = 2048
CONV_DIM = D_MODEL
CONV_WIDTH = 31
SSM_INNER = 2 * D_MODEL
SSM_HEAD_DIM = 64
SSM_HEADS = SSM_INNER // SSM_HEAD_DIM
SSM_GROUPS = 8
HEADS_PER_GROUP = SSM_HEADS // SSM_GROUPS
SSM_STATE = 128
SSM_CONV_WIDTH = 4
SSM_XBC_DIM = SSM_INNER + 2 * SSM_GROUPS * SSM_STATE
D_FF = 4 * D_MODEL
EPS = 1e-5

V7X_LANES = 128
V7X_VMEM_BYTES = 64 * 1024 * 1024
V7X_VMEM_REQUEST_CAP = 56 * 1024 * 1024
COMPILER_TEMP_BYTES = 12 * 1024 * 1024

SSD_CHUNK = 128
CONV_HALO = 32
SSM_HALO = 16
ROW_TILE = 512
MM_ROW_TILE = 1536


def _vmem_limit(block_bytes, scratch_bytes=0):
    need = 2 * sum(block_bytes) + scratch_bytes + COMPILER_TEMP_BYTES
    return int(min(need, V7X_VMEM_REQUEST_CAP))


def _nbytes(shape, dtype):
    n = 1
    for s in shape:
        n *= s
    return n * jnp.dtype(dtype).itemsize


def _sigmoid(x):
    return 1.0 / (1.0 + jnp.exp(-x))


def _softplus(x):
    return jnp.maximum(x, 0.0) + jnp.log1p(jnp.exp(-jnp.abs(x)))


def _rms(x, w):
    ms = jnp.mean(x * x, axis=-1, keepdims=True)
    return x * lax.rsqrt(ms + EPS) * w


def _norm_kernel(xp_ref, xs_ref, w_ref, o_ref, *, n_prompt_tiles):
    i = pl.program_id(0)

    @pl.when(i < n_prompt_tiles)
    def _():
        o_ref[...] = _rms(xp_ref[...], w_ref[...]).astype(o_ref.dtype)

    @pl.when(i >= n_prompt_tiles)
    def _():
        o_ref[...] = _rms(xs_ref[...], w_ref[...]).astype(o_ref.dtype)


def _norm_call(xp, xs, w):
    tp, d = xp.shape
    ts = xs.shape[0]
    assert ts == ROW_TILE and tp % ROW_TILE == 0
    npt = tp // ROW_TILE
    blocks = [_nbytes((ROW_TILE, d), F32)] * 2 + [_nbytes((ROW_TILE, d), BF16)]
    return pl.pallas_call(
        functools.partial(_norm_kernel, n_prompt_tiles=npt),
        grid=(npt + 1,),
        in_specs=[
            pl.BlockSpec((ROW_TILE, d), lambda i: (jnp.minimum(i, npt - 1), 0)),
            pl.BlockSpec((ROW_TILE, d), lambda i: (0, 0)),
            pl.BlockSpec((1, d), lambda i: (0, 0)),
        ],
        out_specs=pl.BlockSpec((ROW_TILE, d), lambda i: (i, 0)),
        out_shape=jax.ShapeDtypeStruct((tp + ts, d), BF16),
        compiler_params=pltpu.CompilerParams(
            dimension_semantics=("arbitrary",), vmem_limit_bytes=_vmem_limit(blocks)),
        name="mix_rmsnorm",
    )(xp, xs, w)


WPREP_COLS = 512
WCOL_VAL = 0
WCOL_GATE = CONV_DIM
WCOL_Z = 2 * CONV_DIM
WCOL_XBC = WCOL_Z + SSM_INNER
WCOL_G = WCOL_XBC + SSM_XBC_DIM
WCOL_DT = WCOL_G + 2 * D_MODEL
WPREP_TOTAL = WCOL_DT + WPREP_COLS


def _wprep_kernel(a_ref, b_ref, o_ref, *, n_plain, n_shift):
    j = pl.program_id(0)

    @pl.when(j < n_plain)
    def _():
        o_ref[...] = a_ref[...].T.astype(BF16)

    @pl.when(jnp.logical_and(j >= n_plain, j < n_plain + n_shift))
    def _():
        src = jnp.concatenate([a_ref[SSM_HEADS:, :], b_ref[:SSM_HEADS, :]], axis=0)
        o_ref[...] = src.T.astype(BF16)

    @pl.when(j == n_plain + n_shift)
    def _():
        r = lax.broadcasted_iota(jnp.int32, a_ref.shape, 0)
        o_ref[...] = jnp.where(r < SSM_HEADS, a_ref[...], 0.0).T.astype(BF16)


def _wprep_call(w_in_t):
    n_src, k = w_in_t.shape
    assert n_src == WCOL_G + SSM_HEADS + 2 * D_MODEL
    n_plain = WCOL_G // WPREP_COLS
    n_shift = 2 * D_MODEL // WPREP_COLS
    last_src = (n_src - 1) // WPREP_COLS
    blocks = [_nbytes((WPREP_COLS, k), F32)] * 3 + [_nbytes((k, WPREP_COLS), BF16)]
    return pl.pallas_call(
        functools.partial(_wprep_kernel, n_plain=n_plain, n_shift=n_shift),
        grid=(n_plain + n_shift + 1,),
        in_specs=[
            pl.BlockSpec((WPREP_COLS, k), lambda j: (jnp.where(j == n_plain + n_shift, n_plain, j), 0)),
            pl.BlockSpec((WPREP_COLS, k),
                         lambda j: (jnp.where(j < n_plain, n_plain, jnp.minimum(j + 1, last_src)), 0)),
        ],
        out_specs=pl.BlockSpec((k, WPREP_COLS), lambda j: (0, j)),
        out_shape=jax.ShapeDtypeStruct((k, WPREP_TOTAL), BF16),
        compiler_params=pltpu.CompilerParams(
            dimension_semantics=("arbitrary",), vmem_limit_bytes=_vmem_limit(blocks)),
        name="w_in_prep",
    )(w_in_t, w_in_t)


def _mm_kernel(*refs, n_w, n_ex, n_p, epilogue):
    lhs_ref = refs[0]
    w_refs = refs[1:1 + n_w]
    ex_refs = refs[1 + n_w:1 + n_w + n_ex]
    p_refs = refs[1 + n_w + n_ex:1 + n_w + n_ex + n_p]
    o_ref = refs[-1]
    lhs = lhs_ref[...]
    accs = [jnp.dot(lhs, w[...], preferred_element_type=F32) for w in w_refs]
    out = epilogue(accs, [e[...].astype(F32) for e in ex_refs], [p[...] for p in p_refs])
    o_ref[...] = out.astype(o_ref.dtype)


def _mm_call(lhs, weights, n, epilogue, out_dtype, *, tm, tn, extras=(), params=(), name):
    t, k = lhs.shape
    assert t % tm == 0 and n % tn == 0
    in_specs = [pl.BlockSpec((tm, k), lambda i, j: (i, 0))]
    operands = [lhs]
    blocks = [_nbytes((tm, k), lhs.dtype)]
    for w, col0 in weights:
        assert col0 % tn == 0
        in_specs.append(pl.BlockSpec((k, tn), lambda i, j, o=col0 // tn: (0, j + o)))
        operands.append(w)
        blocks.append(_nbytes((k, tn), w.dtype))
    for arr, col0 in extras:
        assert col0 % tn == 0
        in_specs.append(pl.BlockSpec((tm, tn), lambda i, j, o=col0 // tn: (i, j + o)))
        operands.append(arr)
        blocks.append(_nbytes((tm, tn), arr.dtype))
    for p in params:
        in_specs.append(pl.BlockSpec((1, tn), lambda i, j: (0, j)))
        operands.append(p)
        blocks.append(_nbytes((8, tn), p.dtype))
    blocks.append(_nbytes((tm, tn), out_dtype))
    return pl.pallas_call(
        functools.partial(_mm_kernel, n_w=len(weights), n_ex=len(extras), n_p=len(params),
                          epilogue=epilogue),
        grid=(t // tm, n // tn),
        in_specs=in_specs,
        out_specs=pl.BlockSpec((tm, tn), lambda i, j: (i, j)),
        out_shape=jax.ShapeDtypeStruct((t, n), out_dtype),
        compiler_params=pltpu.CompilerParams(
            dimension_semantics=("arbitrary", "arbitrary"), vmem_limit_bytes=_vmem_limit(blocks)),
        name=name,
    )(*operands)


def _ep_glu(accs, extras, params):
    return accs[0] * _sigmoid(accs[1])


def _ep_identity(accs, extras, params):
    return accs[0]


def _ep_sigmoid(accs, extras, params):
    return _sigmoid(accs[0])


def _ep_softplus_bias(accs, extras, params):
    return _softplus(accs[0] + params[0])


def _ep_merge(accs, extras, params):
    return extras[0] + extras[1] * accs[0]


XBC_ROW_TILE = 1024
XBC_COL_TILE = 2048
XBC_SUB_COLS = 256
SUBLANES = 8


def _xbc_conv_kernel(h_ref, w_ref, cw_ref, cb_ref, act_ref, tail_ref, carry_ref, stage_ref, *, tiles_per_seq):
    i = pl.program_id(0)
    j = pl.program_id(1)
    tm = h_ref.shape[0]
    hist = SSM_CONV_WIDTH - 1

    @pl.when((i % tiles_per_seq) == 0)
    def _():
        tile_cols = pl.ds(pl.multiple_of(j * XBC_COL_TILE, XBC_COL_TILE), XBC_COL_TILE)
        carry_ref[:, tile_cols] = jnp.zeros((SUBLANES, XBC_COL_TILE), F32)

    lhs = h_ref[...]
    for n in range(XBC_COL_TILE // XBC_SUB_COLS):
        sub = slice(n * XBC_SUB_COLS, (n + 1) * XBC_SUB_COLS)
        gcols = pl.ds(pl.multiple_of(j * XBC_COL_TILE + n * XBC_SUB_COLS, XBC_SUB_COLS), XBC_SUB_COLS)
        st = stage_ref.at[n % 2]
        st[SUBLANES:SUBLANES + tm, :] = jnp.dot(lhs, w_ref[:, sub], preferred_element_type=F32)
        st[0:SUBLANES, :] = carry_ref[:, gcols]
        last = st[tm:tm + SUBLANES, :]
        carry_ref[:, gcols] = last
        tail_ref[0, :, sub] = last
        conv = cb_ref[:, sub] + cw_ref[hist:hist + 1, sub] * st[SUBLANES:SUBLANES + tm, :]
        for k in range(hist):
            conv = conv + cw_ref[k:k + 1, sub] * st[SUBLANES - hist + k:SUBLANES - hist + k + tm, :]
        act_ref[:, sub] = (conv * _sigmoid(conv)).astype(act_ref.dtype)


def _xbc_conv_call(h, w, w_col0, cw, cb, *, n_seq, seq_len):
    k = h.shape[1]
    n = cw.shape[1]
    tm, tn = XBC_ROW_TILE, XBC_COL_TILE
    assert seq_len % tm == 0 and n % tn == 0 and w_col0 % tn == 0
    w_blk0 = w_col0 // tn
    tiles_per_seq = seq_len // tm
    stage_shape = (2, tm + SUBLANES, XBC_SUB_COLS)
    blocks = [_nbytes((tm, k), BF16), _nbytes((k, tn), BF16), _nbytes((8, tn), F32) * 3,
              _nbytes((tm, tn), BF16)]
    return pl.pallas_call(
        functools.partial(_xbc_conv_kernel, tiles_per_seq=tiles_per_seq),
        grid=(n_seq * tiles_per_seq, n // tn),
        in_specs=[
            pl.BlockSpec((tm, k), lambda i, j: (i, 0)),
            pl.BlockSpec((k, tn), lambda i, j: (0, j + w_blk0)),
            pl.BlockSpec((SSM_CONV_WIDTH, tn), lambda i, j: (0, j)),
            pl.BlockSpec((1, tn), lambda i, j: (0, j)),
        ],
        out_specs=[
            pl.BlockSpec((tm, tn), lambda i, j: (i, j)),
            pl.BlockSpec((1, SUBLANES, tn), lambda i, j: (i, 0, j)),
        ],
        out_shape=[
            jax.ShapeDtypeStruct((n_seq * seq_len, n), BF16),
            jax.ShapeDtypeStruct((n_seq * tiles_per_seq, SUBLANES, n), F32),
        ],
        scratch_shapes=[pltpu.VMEM((SUBLANES, n), F32), pltpu.VMEM(stage_shape, F32)],
        compiler_params=pltpu.CompilerParams(
            dimension_semantics=("arbitrary", "arbitrary"),
            vmem_limit_bytes=_vmem_limit(blocks, _nbytes((SUBLANES, n), F32) + _nbytes(stage_shape, F32))),
        name="in_proj_xbc_conv",
    )(h, w, cw, cb)


CONV_ROW_BLOCK = 64
LN_ROW_BLOCK = 256


def _conv31_block(src, zs, dww_ref, dwb_ref, cols, lanes, tm, store):
    hist = CONV_WIDTH - 1
    base = CONV_HALO - hist
    rb = min(CONV_ROW_BLOCK, tm)
    for s in range(SUBLANES):
        rows = tm + SUBLANES * ((hist - s) // SUBLANES)
        zs[s, 0:rows, :] = src[base + s:base + s + rows, lanes]
    bias = jnp.broadcast_to(dwb_ref[:, cols], (rb, V7X_LANES))
    for r0 in range(0, tm, rb):
        acc = bias
        for k in range(CONV_WIDTH):
            s, q = k % SUBLANES, k // SUBLANES
            acc = acc + dww_ref[k:k + 1, cols] * zs[s, r0 + SUBLANES * q:r0 + SUBLANES * q + rb, :]
        store(r0, rb, acc)


def _conf_kernel(*refs, tm, n_tiles, n_fill, conv_in_kernel, aliased):
    if n_fill == 0:
        _conf_body(refs, tm, conv_in_kernel, aliased)
        return
    i = pl.program_id(0)
    o_ref = refs[5 + 3 * int(conv_in_kernel) + int(aliased)]

    @pl.when(i < n_tiles)
    def _():
        _conf_body(refs, tm, conv_in_kernel, aliased)

    @pl.when(i >= n_tiles)
    def _():
        o_ref[...] = jnp.zeros_like(o_ref)


def _conf_body(refs, tm, conv_in_kernel, aliased):
    it = iter(refs)
    src_ref = next(it)
    if conv_in_kernel:
        hist_ref = next(it)
        dww_ref = next(it)
        dwb_ref = next(it)
    g_ref = next(it)
    lnw_ref = next(it)
    lnb_ref = next(it)
    w_ref = next(it)
    if aliased:
        next(it)
    o_ref = next(it)
    if conv_in_kernel:
        cache_ref = next(it)
        xs_ref = next(it)
        zs_ref = next(it)
        y_ref = next(it)
    s_ref = next(it)
    hist = CONV_WIDTH - 1

    if conv_in_kernel:
        xs_ref[0:CONV_HALO - hist, :] = jnp.zeros((CONV_HALO - hist, CONV_DIM), F32)
        xs_ref[CONV_HALO - hist:CONV_HALO, :] = hist_ref[0]
        xs_ref[CONV_HALO:CONV_HALO + tm, :] = src_ref[...].astype(F32)
        cache_ref[0] = xs_ref[CONV_HALO + tm - hist:CONV_HALO + tm, :]

        def conv_cols(cb, carry):
            cols = pl.ds(pl.multiple_of(cb * V7X_LANES, V7X_LANES), V7X_LANES)

            def store(r0, rb, acc):
                y_ref[r0:r0 + rb, cols] = acc

            _conv31_block(xs_ref, zs_ref, dww_ref, dwb_ref, cols, cols, tm, store)
            return carry

        lax.fori_loop(0, CONV_DIM // V7X_LANES, conv_cols, 0)

    ln_rows_per_step = min(LN_ROW_BLOCK, tm)

    def ln_rows(rb, carry):
        r0 = pl.multiple_of(rb * ln_rows_per_step, ln_rows_per_step)
        if conv_in_kernel:
            y = y_ref[pl.ds(r0, ln_rows_per_step), :]
        else:
            y = src_ref[pl.ds(r0, ln_rows_per_step), :].astype(F32)
        mu = jnp.mean(y, axis=-1, keepdims=True)
        yc = y - mu
        var = jnp.mean(yc * yc, axis=-1, keepdims=True)
        yn = yc * lax.rsqrt(var + EPS) * lnw_ref[...] + lnb_ref[...]
        s_ref[pl.ds(r0, ln_rows_per_step), :] = (yn * _sigmoid(yn)).astype(BF16)
        return carry

    lax.fori_loop(0, tm // ln_rows_per_step, ln_rows, 0)

    ya = jnp.dot(s_ref[...], w_ref[...], preferred_element_type=F32)
    o_ref[...] = (g_ref[...].astype(F32) * ya).astype(o_ref.dtype)


def _conf_call(src, gates, conv, prev, lnw, lnb, w, *, row0, n_rows, tm, name, n_fill=0):
    assert n_rows % tm == 0 and row0 % tm == 0 and src.shape[0] == n_rows
    n_tiles = n_rows // tm
    conv_in_kernel = conv is not None
    assert not (conv_in_kernel and n_fill)
    hist = CONV_WIDTH - 1
    blk0 = row0 // tm

    def tile(i):
        return jnp.minimum(i, n_tiles - 1)

    in_specs = [pl.BlockSpec((tm, CONV_DIM), lambda i: (tile(i), 0))]
    operands = [src]
    blocks = [_nbytes((tm, CONV_DIM), BF16), _nbytes((tm, D_MODEL), BF16) * 2, _nbytes((8, CONV_DIM), F32) * 2,
              _nbytes((CONV_DIM, D_MODEL), BF16)]
    scratch_dims = [((tm, CONV_DIM), BF16)]
    out_specs = [pl.BlockSpec((tm, D_MODEL), lambda i: (blk0 + i, 0))]
    out_shape = [jax.ShapeDtypeStruct((gates.shape[0], D_MODEL), BF16)]
    if conv_in_kernel:
        assert tm % CONV_HALO == 0
        in_specs += [
            pl.BlockSpec((1, hist, CONV_DIM), lambda i: (i, 0, 0)),
            pl.BlockSpec((CONV_WIDTH, CONV_DIM), lambda i: (0, 0)),
            pl.BlockSpec((1, CONV_DIM), lambda i: (0, 0)),
        ]
        operands += list(conv)
        blocks += [_nbytes((32, CONV_DIM), F32) * 3, _nbytes((8, CONV_DIM), F32)]
        scratch_dims = [((CONV_HALO + tm, CONV_DIM), F32), ((SUBLANES, CONV_HALO + tm, V7X_LANES), F32),
                        ((tm, CONV_DIM), F32)] + scratch_dims
        out_specs.append(pl.BlockSpec((1, hist, CONV_DIM), lambda i: (i, 0, 0)))
        out_shape.append(jax.ShapeDtypeStruct((n_tiles, hist, CONV_DIM), F32))
    in_specs += [
        pl.BlockSpec((tm, D_MODEL), lambda i: (blk0 + tile(i), 0)),
        pl.BlockSpec((1, CONV_DIM), lambda i: (0, 0)),
        pl.BlockSpec((1, CONV_DIM), lambda i: (0, 0)),
        pl.BlockSpec((CONV_DIM, D_MODEL), lambda i: (0, 0)),
    ]
    operands += [gates, lnw, lnb, w]
    aliases = {}
    if prev is not None:
        in_specs.append(pl.BlockSpec(memory_space=pl.ANY))
        operands.append(prev)
        aliases = {len(operands) - 1: 0}
    return pl.pallas_call(
        functools.partial(_conf_kernel, tm=tm, n_tiles=n_tiles, n_fill=n_fill, conv_in_kernel=conv_in_kernel,
                          aliased=prev is not None),
        grid=(n_tiles + n_fill,),
        in_specs=in_specs,
        out_specs=out_specs,
        out_shape=out_shape,
        scratch_shapes=[pltpu.VMEM(shape, dt) for shape, dt in scratch_dims],
        input_output_aliases=aliases,
        compiler_params=pltpu.CompilerParams(
            dimension_semantics=("arbitrary",),
            vmem_limit_bytes=_vmem_limit(blocks, sum(_nbytes(sh, dt) for sh, dt in scratch_dims))),
        name=name,
    )(*operands)


GC_ROW_TILE = 512
GC_COL_TILE = 2048
GC_SUB_COLS = 256


def _glu_conv_kernel(h_ref, wv_ref, wg_ref, dww_ref, dwb_ref, y_ref, tail_ref, carry_ref, xs_ref, zs_ref,
                     *, tiles_per_seq):
    i = pl.program_id(0)
    j = pl.program_id(1)
    tm = h_ref.shape[0]

    @pl.when((i % tiles_per_seq) == 0)
    def _():
        tile_cols = pl.ds(pl.multiple_of(j * GC_COL_TILE, GC_COL_TILE), GC_COL_TILE)
        carry_ref[:, tile_cols] = jnp.zeros((CONV_HALO, GC_COL_TILE), F32)

    lhs = h_ref[...]
    for n in range(GC_COL_TILE // GC_SUB_COLS):
        sub = slice(n * GC_SUB_COLS, (n + 1) * GC_SUB_COLS)
        gcols = pl.ds(pl.multiple_of(j * GC_COL_TILE + n * GC_SUB_COLS, GC_SUB_COLS), GC_SUB_COLS)
        xs = xs_ref.at[n % 2]
        val = jnp.dot(lhs, wv_ref[:, sub], preferred_element_type=F32)
        gate = jnp.dot(lhs, wg_ref[:, sub], preferred_element_type=F32)
        xs[CONV_HALO:CONV_HALO + tm, :] = val * _sigmoid(gate)
        xs[0:CONV_HALO, :] = carry_ref[:, gcols]
        last = xs[tm:tm + CONV_HALO, :]
        carry_ref[:, gcols] = last
        tail_ref[0, :, sub] = last
        for half in range(GC_SUB_COLS // V7X_LANES):
            lanes = slice(half * V7X_LANES, (half + 1) * V7X_LANES)
            cols = slice(n * GC_SUB_COLS + half * V7X_LANES, n * GC_SUB_COLS + (half + 1) * V7X_LANES)

            def store(r0, rb, acc, cols=cols):
                y_ref[r0:r0 + rb, cols] = acc.astype(y_ref.dtype)

            _conv31_block(xs, zs_ref.at[half], dww_ref, dwb_ref, cols, lanes, tm, store)


def _glu_conv_call(h, w, wv_col0, wg_col0, dww, dwb, *, n_seq, seq_len):
    k = h.shape[1]
    n = dww.shape[1]
    tm, tn = GC_ROW_TILE, GC_COL_TILE
    assert seq_len % tm == 0 and n % tn == 0 and wv_col0 % tn == 0 and wg_col0 % tn == 0
    wv_blk0, wg_blk0 = wv_col0 // tn, wg_col0 // tn
    tiles_per_seq = seq_len // tm
    blocks = [_nbytes((tm, k), BF16), _nbytes((k, tn), BF16) * 2, _nbytes((32, tn), F32) * 2,
              _nbytes((tm, tn), BF16), _nbytes((CONV_HALO, tn), F32)]
    scratch_dims = [((CONV_HALO, n), F32), ((2, CONV_HALO + tm, GC_SUB_COLS), F32),
                    ((GC_SUB_COLS // V7X_LANES, SUBLANES, CONV_HALO + tm, V7X_LANES), F32)]
    return pl.pallas_call(
        functools.partial(_glu_conv_kernel, tiles_per_seq=tiles_per_seq),
        grid=(n_seq * tiles_per_seq, n // tn),
        in_specs=[
            pl.BlockSpec((tm, k), lambda i, j: (i, 0)),
            pl.BlockSpec((k, tn), lambda i, j: (0, j + wv_blk0)),
            pl.BlockSpec((k, tn), lambda i, j: (0, j + wg_blk0)),
            pl.BlockSpec((CONV_WIDTH, tn), lambda i, j: (0, j)),
            pl.BlockSpec((1, tn), lambda i, j: (0, j)),
        ],
        out_specs=[
            pl.BlockSpec((tm, tn), lambda i, j: (i, j)),
            pl.BlockSpec((1, CONV_HALO, tn), lambda i, j: (i, 0, j)),
        ],
        out_shape=[
            jax.ShapeDtypeStruct((n_seq * seq_len, n), BF16),
            jax.ShapeDtypeStruct((n_seq * tiles_per_seq, CONV_HALO, n), F32),
        ],
        scratch_shapes=[pltpu.VMEM(shape, dt) for shape, dt in scratch_dims],
        compiler_params=pltpu.CompilerParams(
            dimension_semantics=("arbitrary", "arbitrary"),
            vmem_limit_bytes=_vmem_limit(blocks, sum(_nbytes(sh, dt) for sh, dt in scratch_dims))),
        name="in_proj_glu_conv",
    )(h, w, w, dww, dwb)


SSM_CONV_COL_BLOCK = 512
GROUP_COLS = HEADS_PER_GROUP * SSM_HEAD_DIM
B_COL0 = SSM_INNER
C_COL0 = SSM_INNER + SSM_GROUPS * SSM_STATE
NEG_BIG = -1e30
LOG2_E = 1.4426950408889634


def _split3(x):
    p1 = x.astype(BF16)
    r1 = x - p1.astype(F32)
    p2 = r1.astype(BF16)
    r2 = r1 - p2.astype(F32)
    return p1, p2, r2.astype(BF16)


def _ssd_kernel(*refs, ql, n_chunks, fill_per_seq, has_h0, conv_in_kernel, aliased):
    if fill_per_seq == 0:
        _ssd_body(refs, ql, n_chunks, has_h0, conv_in_kernel, aliased)
        return
    c_idx = pl.program_id(1)
    y_ref = refs[6 + 3 * int(conv_in_kernel) + int(has_h0) + int(aliased)]

    @pl.when(c_idx < n_chunks)
    def _():
        _ssd_body(refs, ql, n_chunks, has_h0, conv_in_kernel, aliased)

    @pl.when(c_idx >= n_chunks)
    def _():
        y_ref[...] = jnp.zeros_like(y_ref)


def _ssd_body(refs, ql, n_chunks, has_h0, conv_in_kernel, aliased):
    it = iter(refs)
    src_ref = next(it)
    if conv_in_kernel:
        hist_ref = next(it)
        cw_ref = next(it)
        cb_ref = next(it)
    dt_ref = next(it)
    z_ref = next(it)
    alog_ref = next(it)
    dskip_ref = next(it)
    nw_ref = next(it)
    h0_ref = next(it) if has_h0 else None
    if aliased:
        next(it)
    y_ref = next(it)
    cache_ref = next(it) if conv_in_kernel else None
    hout_ref = next(it)
    xs_ref = next(it) if conv_in_kernel else None
    act_ref = next(it)
    acs_ref = next(it)
    rowt_ref = next(it)
    wendt_ref = next(it)
    ht_ref = next(it)

    q = SSD_CHUNK
    c_idx = pl.program_id(1)
    hist = SSM_CONV_WIDTH - 1

    @pl.when(c_idx == 0)
    def _():
        for g in range(SSM_GROUPS):
            if has_h0:
                ht_ref[g] = h0_ref[0, g * GROUP_COLS:(g + 1) * GROUP_COLS, :].T
            else:
                ht_ref[g] = jnp.zeros((SSM_STATE, GROUP_COLS), F32)

    if conv_in_kernel:
        assert n_chunks == 1 and ql % SUBLANES == 0
        xs_ref[0:SSM_HALO - hist, :] = jnp.zeros((SSM_HALO - hist, SSM_XBC_DIM), F32)
        xs_ref[SSM_HALO - hist:SSM_HALO, :] = hist_ref[0]
        xs_ref[SSM_HALO:SSM_HALO + ql, :] = src_ref[...].astype(F32)
        if ql < q:
            act_ref[ql:q, :] = jnp.zeros((q - ql, SSM_XBC_DIM), F32)
        base = SSM_HALO - hist

        def conv_cols(cb, carry):
            c0 = pl.multiple_of(cb * SSM_CONV_COL_BLOCK, SSM_CONV_COL_BLOCK)
            cols = pl.ds(c0, SSM_CONV_COL_BLOCK)
            acc = jnp.broadcast_to(cb_ref[:, cols], (ql, SSM_CONV_COL_BLOCK))
            for k in range(SSM_CONV_WIDTH):
                acc = acc + cw_ref[k:k + 1, cols] * xs_ref[base + k:base + k + ql, cols]
            act_ref[0:ql, cols] = acc * _sigmoid(acc)
            return carry

        lax.fori_loop(0, SSM_XBC_DIM // SSM_CONV_COL_BLOCK, conv_cols, 0)
    else:
        assert ql == q
        act_ref[...] = src_ref[...].astype(F32)

    dt = dt_ref[...]
    if ql < q:
        dt = jnp.concatenate([dt, jnp.zeros((q - ql, V7X_LANES), F32)], axis=0)
    a_neg = -jnp.exp(alog_ref[...])
    dta = dt * a_neg
    r_i = lax.broadcasted_iota(jnp.int32, (q, q), 0)
    c_i = lax.broadcasted_iota(jnp.int32, (q, q), 1)
    causal = r_i >= c_i
    tri = jnp.where(causal, 1.0, 0.0).astype(BF16)
    p1, p2, p3 = _split3(dta)
    acs = (jnp.dot(tri, p1, preferred_element_type=F32) + jnp.dot(tri, p2, preferred_element_type=F32)
           + jnp.dot(tri, p3, preferred_element_type=F32))
    acs2 = acs * LOG2_E
    acs_ref[...] = acs2
    acs2_t = acs2.T
    rowt = acs2_t - jnp.log2(dt.T)
    rowt_ref[...] = rowt
    wendt_ref[...] = jnp.exp2(acs2_t[:, q - 1:q] - rowt)

    left = c_i < SSM_HEAD_DIM
    left_row = left[0:1, :]
    c_l = lax.broadcasted_iota(jnp.int32, (ql, q), 1)
    causal_l = lax.broadcasted_iota(jnp.int32, (ql, q), 0) >= c_l
    left_l = c_l < SSM_HEAD_DIM

    for g in range(SSM_GROUPS):
        bg = act_ref[:, B_COL0 + g * SSM_STATE:B_COL0 + (g + 1) * SSM_STATE]
        cg = act_ref[0:ql, C_COL0 + g * SSM_STATE:C_COL0 + (g + 1) * SSM_STATE].astype(BF16)
        cbm = lax.dot_general(cg, bg.astype(BF16), (((1,), (1,)), ((), ())),
                              preferred_element_type=F32)
        bg_t = bg.T
        ht_g = ht_ref[g]
        yoff = jnp.dot(cg, ht_g.astype(BF16), preferred_element_type=F32)
        y_parts, inc_parts, dec_parts = [], [], []
        for j in range(HEADS_PER_GROUP // 2):
            h_a = g * HEADS_PER_GROUP + 2 * j
            h_b = h_a + 1
            lo = h_a * SSM_HEAD_DIM
            xp = act_ref[:, lo:lo + V7X_LANES]
            lhs_parts = []
            cols = []
            for h in (h_a, h_b):
                col = jnp.broadcast_to(acs_ref[0:ql, h:h + 1], (ql, q))
                cols.append(col)
                seg = jnp.where(causal_l, col - rowt_ref[h:h + 1, :], NEG_BIG)
                m_h = cbm * jnp.exp2(seg)
                w_h = bg_t * wendt_ref[h:h + 1, :]
                lhs_parts.append(jnp.concatenate([m_h, w_h], axis=0).astype(BF16))
            rhs = jnp.concatenate([jnp.where(left, xp, 0.0).astype(BF16),
                                   jnp.where(left, 0.0, xp).astype(BF16)], axis=0)
            res = jnp.dot(jnp.concatenate(lhs_parts, axis=1), rhs, preferred_element_type=F32)
            colsel = jnp.where(left_l, cols[0], cols[1])
            y_parts.append(res[0:ql] + jnp.exp2(colsel) * yoff[:, j * V7X_LANES:(j + 1) * V7X_LANES]
                           + dskip_ref[:, lo:lo + V7X_LANES] * xp[0:ql])
            inc_parts.append(res[ql:ql + q])
            last_a = jnp.broadcast_to(acs_ref[q - 1:q, h_a:h_a + 1], (1, V7X_LANES))
            last_b = jnp.broadcast_to(acs_ref[q - 1:q, h_b:h_b + 1], (1, V7X_LANES))
            dec_parts.append(jnp.exp2(jnp.where(left_row, last_a, last_b)))
        ht_ref[g] = jnp.concatenate(dec_parts, axis=1) * ht_g + jnp.concatenate(inc_parts, axis=1)
        gc = slice(g * GROUP_COLS, (g + 1) * GROUP_COLS)
        yg = jnp.concatenate(y_parts, axis=1)
        zg = z_ref[:, gc].astype(F32)
        yz = yg * (zg * _sigmoid(zg))
        ms = jnp.mean(yz * yz, axis=-1, keepdims=True)
        y_ref[:, gc] = (yz * lax.rsqrt(ms + EPS) * nw_ref[:, gc]).astype(y_ref.dtype)

    @pl.when(c_idx == n_chunks - 1)
    def _():
        if conv_in_kernel:
            cache_ref[0] = xs_ref[SSM_HALO + ql - hist:SSM_HALO + ql, :]
        for g in range(SSM_GROUPS):
            hout_ref[0, g * GROUP_COLS:(g + 1) * GROUP_COLS, :] = ht_ref[g].T


def _ssd_call(src, dt, z, conv, h0, prev, alog, dskip, nw, *, row0, n_seq, seq_len, ql, name, n_fill=0):
    assert seq_len % ql == 0 and row0 % ql == 0 and ql % SSM_HALO == 0 and ql <= SSD_CHUNK
    assert src.shape[0] == n_seq * seq_len
    n_chunks = seq_len // ql
    conv_in_kernel = conv is not None
    has_h0 = h0 is not None
    hist = SSM_CONV_WIDTH - 1
    blk0 = row0 // ql
    q = SSD_CHUNK

    assert n_fill % n_seq == 0
    fill_per_seq = n_fill // n_seq

    def chunk(c):
        return jnp.minimum(c, n_chunks - 1)

    def row_blk(b, c):
        return blk0 + b * n_chunks + chunk(c)

    def out_blk(b, c):
        fill_blk = blk0 + n_seq * n_chunks + b * fill_per_seq + (c - n_chunks)
        return jnp.where(c < n_chunks, row_blk(b, c), fill_blk)

    in_specs = [pl.BlockSpec((ql, SSM_XBC_DIM), lambda b, c: (b * n_chunks + chunk(c), 0))]
    operands = [src]
    if conv_in_kernel:
        in_specs += [
            pl.BlockSpec((1, hist, SSM_XBC_DIM), lambda b, c: (b, 0, 0)),
            pl.BlockSpec((SSM_CONV_WIDTH, SSM_XBC_DIM), lambda b, c: (0, 0)),
            pl.BlockSpec((1, SSM_XBC_DIM), lambda b, c: (0, 0)),
        ]
        operands += list(conv)
    in_specs += [
        pl.BlockSpec((ql, V7X_LANES), lambda b, c: (row_blk(b, c), 0)),
        pl.BlockSpec((ql, SSM_INNER), lambda b, c: (row_blk(b, c), 0)),
        pl.BlockSpec((1, V7X_LANES), lambda b, c: (0, 0)),
        pl.BlockSpec((1, SSM_INNER), lambda b, c: (0, 0)),
        pl.BlockSpec((1, SSM_INNER), lambda b, c: (0, 0)),
    ]
    operands += [dt, z, alog, dskip, nw]
    if has_h0:
        in_specs.append(pl.BlockSpec((1, SSM_INNER, SSM_STATE), lambda b, c: (b, 0, 0)))
        operands.append(h0)
    aliases = {}
    if prev is not None:
        in_specs.append(pl.BlockSpec(memory_space=pl.ANY))
        operands.append(prev)
        aliases = {len(operands) - 1: 0}
    state_bytes = _nbytes((SSM_INNER, SSM_STATE), F32)
    blocks = [_nbytes((ql, SSM_XBC_DIM), BF16), _nbytes((SSM_HALO, SSM_XBC_DIM), F32),
              _nbytes((ql, V7X_LANES), F32), _nbytes((ql, SSM_INNER), BF16),
              _nbytes((8, SSM_XBC_DIM), F32) * 2, _nbytes((8, SSM_INNER), F32) * 2,
              _nbytes((ql, SSM_INNER), BF16), _nbytes((8, SSM_XBC_DIM), F32),
              state_bytes * (2 if has_h0 else 1)]
    scratch_dims = ([((SSM_HALO + q, SSM_XBC_DIM), F32)] if conv_in_kernel else []) + [
        ((q, SSM_XBC_DIM), F32),
        ((q, V7X_LANES), F32),
        ((V7X_LANES, q), F32),
        ((V7X_LANES, q), F32),
        ((SSM_GROUPS, SSM_STATE, GROUP_COLS), F32),
    ]
    out_specs = [pl.BlockSpec((ql, SSM_INNER), lambda b, c: (out_blk(b, c), 0))]
    out_shape = [jax.ShapeDtypeStruct((dt.shape[0], SSM_INNER), BF16)]
    if conv_in_kernel:
        out_specs.append(pl.BlockSpec((1, hist, SSM_XBC_DIM), lambda b, c: (b, 0, 0)))
        out_shape.append(jax.ShapeDtypeStruct((n_seq, hist, SSM_XBC_DIM), F32))
    out_specs.append(pl.BlockSpec((1, SSM_INNER, SSM_STATE), lambda b, c: (b, 0, 0)))
    out_shape.append(jax.ShapeDtypeStruct((n_seq, SSM_INNER, SSM_STATE), F32))
    return pl.pallas_call(
        functools.partial(_ssd_kernel, ql=ql, n_chunks=n_chunks, fill_per_seq=fill_per_seq, has_h0=has_h0,
                          conv_in_kernel=conv_in_kernel, aliased=prev is not None),
        grid=(n_seq, n_chunks + fill_per_seq),
        in_specs=in_specs,
        out_specs=out_specs,
        out_shape=out_shape,
        scratch_shapes=[pltpu.VMEM(shape, dt_) for shape, dt_ in scratch_dims],
        input_output_aliases=aliases,
        compiler_params=pltpu.CompilerParams(
            dimension_semantics=("arbitrary", "arbitrary"),
            vmem_limit_bytes=_vmem_limit(blocks, sum(_nbytes(sh, dt_) for sh, dt_ in scratch_dims))),
        name=name,
    )(*operands)


def _outproj_kernel(m_ref, w_ref, xp_ref, xs_ref, o_ref, *, n_prompt_tiles):
    i = pl.program_id(0)
    acc = jnp.dot(m_ref[...], w_ref[...], preferred_element_type=F32)

    @pl.when(i < n_prompt_tiles)
    def _():
        o_ref[...] = xp_ref[...] + acc

    @pl.when(i >= n_prompt_tiles)
    def _():
        o_ref[...] = xs_ref[...] + acc


def _outproj_call(m, w, xp, xs):
    t, d = m.shape
    npt = xp.shape[0] // ROW_TILE
    blocks = [_nbytes((ROW_TILE, d), BF16), _nbytes((d, d), BF16)] + [_nbytes((ROW_TILE, d), F32)] * 3
    return pl.pallas_call(
        functools.partial(_outproj_kernel, n_prompt_tiles=npt),
        grid=(t // ROW_TILE,),
        in_specs=[
            pl.BlockSpec((ROW_TILE, d), lambda i: (i, 0)),
            pl.BlockSpec((d, d), lambda i: (0, 0)),
            pl.BlockSpec((ROW_TILE, d), lambda i: (jnp.minimum(i, npt - 1), 0)),
            pl.BlockSpec((ROW_TILE, d), lambda i: (0, 0)),
        ],
        out_specs=pl.BlockSpec((ROW_TILE, d), lambda i: (i, 0)),
        out_shape=jax.ShapeDtypeStruct((t, d), F32),
        compiler_params=pltpu.CompilerParams(
            dimension_semantics=("arbitrary",), vmem_limit_bytes=_vmem_limit(blocks)),
        name="out_proj_residual",
    )(m, w, xp, xs)


MLP_FF_TILE = 1024


def _mlp_kernel(x1_ref, nw_ref, wup_ref, wdn_ref, fw_ref, yp_ref, ys_ref, h2_ref, acc_ref,
                *, n_prompt_tiles, n_ff):
    i = pl.program_id(0)
    f = pl.program_id(1)

    @pl.when(f == 0)
    def _():
        h2_ref[...] = _rms(x1_ref[...], nw_ref[...]).astype(BF16)
        acc_ref[...] = jnp.zeros_like(acc_ref)

    hid = jnp.dot(h2_ref[...], wup_ref[...], preferred_element_type=F32)
    hid = jnp.square(jnp.maximum(hid, 0.0))
    acc_ref[...] += jnp.dot(hid.astype(BF16), wdn_ref[...], preferred_element_type=F32)

    @pl.when(f == n_ff - 1)
    def _():
        out = _rms(x1_ref[...] + acc_ref[...], fw_ref[...])

        @pl.when(i < n_prompt_tiles)
        def _():
            yp_ref[...] = out

        @pl.when(i >= n_prompt_tiles)
        def _():
            ys_ref[...] = out


def _mlp_call(x1, nw, wup, wdn, fw, *, n_prompt_rows):
    t, d = x1.shape
    npt = n_prompt_rows // ROW_TILE
    n_ff = D_FF // MLP_FF_TILE
    blocks = [_nbytes((ROW_TILE, d), F32), _nbytes((d, MLP_FF_TILE), BF16),
              _nbytes((MLP_FF_TILE, d), BF16), _nbytes((ROW_TILE, d), F32) * 2]
    scratch_bytes = _nbytes((ROW_TILE, d), BF16) + _nbytes((ROW_TILE, d), F32)
    return pl.pallas_call(
        functools.partial(_mlp_kernel, n_prompt_tiles=npt, n_ff=n_ff),
        grid=(t // ROW_TILE, n_ff),
        in_specs=[
            pl.BlockSpec((ROW_TILE, d), lambda i, f: (i, 0)),
            pl.BlockSpec((1, d), lambda i, f: (0, 0)),
            pl.BlockSpec((d, MLP_FF_TILE), lambda i, f: (0, f)),
            pl.BlockSpec((MLP_FF_TILE, d), lambda i, f: (f, 0)),
            pl.BlockSpec((1, d), lambda i, f: (0, 0)),
        ],
        out_specs=[
            pl.BlockSpec((ROW_TILE, d), lambda i, f: (jnp.minimum(i, npt - 1), 0)),
            pl.BlockSpec((ROW_TILE, d), lambda i, f: (0, 0)),
        ],
        out_shape=[
            jax.ShapeDtypeStruct((n_prompt_rows, d), F32),
            jax.ShapeDtypeStruct((t - n_prompt_rows, d), F32),
        ],
        scratch_shapes=[pltpu.VMEM((ROW_TILE, d), BF16), pltpu.VMEM((ROW_TILE, d), F32)],
        compiler_params=pltpu.CompilerParams(
            dimension_semantics=("arbitrary", "arbitrary"),
            vmem_limit_bytes=_vmem_limit(blocks, scratch_bytes)),
        name="mlp_final_norm",
    )(x1, nw, wup, wdn, fw)


def kernel(x_prompt, x_sample, cache_conv, cache_ssm_conv, state_ssm, mix_norm_w, w_in, conv_dw_w,
           conv_dw_b, conv_ln_w, conv_ln_b, w_conv_out, ssm_conv_w, ssm_conv_b, dt_bias, a_log, d_skip,
           ssm_norm_w, w_ssm_out, w_out, mlp_norm_w, w_up, w_down, final_norm_w):
    bp, lp, d = x_prompt.shape
    bs, ls, _ = x_sample.shape
    tp, ts = bp * lp, bs * ls
    xp = x_prompt.reshape(tp, d)
    xs = x_sample.reshape(ts, d)

    wi = _wprep_call(w_in[0].T)
    dt_b =jnp.pad(dt_bias[0], (0, V7X_LANES - SSM_HEADS)).reshape(1, V7X_LANES)
    alog = jnp.pad(a_log[0], (0, V7X_LANES - SSM_HEADS)).reshape(1, V7X_LANES)
    dskip = jnp.repeat(d_skip[0], SSM_HEAD_DIM).reshape(1, SSM_INNER)
    row = lambda v: v.reshape(1, -1)

    h = _norm_call(xp, xs, row(mix_norm_w[0]))

    tm, tn = MM_ROW_TILE, 1024
    dww, dwb = conv_dw_w[0], row(conv_dw_b[0])
    yconv_p, a_tail_p = _glu_conv_call(h, wi, WCOL_VAL, WCOL_GATE, dww, dwb, n_seq=bp, seq_len=lp)
    h_s = h[tp:]
    a_s = _mm_call(h_s, [(wi, WCOL_VAL), (wi, WCOL_GATE)], CONV_DIM, _ep_glu, BF16, tm=ts, tn=tn // 2,
                   name="in_proj_glu_sample")
    z = _mm_call(h, [(wi, WCOL_Z)], SSM_INNER, _ep_identity, BF16, tm=tm, tn=tn, name="in_proj_z")
    cw, cb = ssm_conv_w[0], row(ssm_conv_b[0])
    act_p, xbc_tail_p = _xbc_conv_call(h, wi, WCOL_XBC, cw, cb, n_seq=bp, seq_len=lp)
    xbc_s = _mm_call(h_s, [(wi, WCOL_XBC)], SSM_XBC_DIM, _ep_identity, BF16, tm=ts, tn=tn,
                     name="in_proj_xbc_sample")
    dt = _mm_call(h, [(wi, WCOL_DT)], V7X_LANES, _ep_softplus_bias, F32, tm=tm, tn=V7X_LANES, params=[dt_b],
                  name="in_proj_dt")
    gates = _mm_call(h, [(wi, WCOL_G)], 2 * D_MODEL, _ep_sigmoid, BF16, tm=tm, tn=tn, name="in_proj_gates")

    conf_args = (row(conv_ln_w[0]), row(conv_ln_b[0]), w_conv_out[0].astype(BF16))
    (m_a,) = _conf_call(yconv_p, gates, None, None, *conf_args, row0=0, n_rows=tp, tm=ROW_TILE,
                        n_fill=ts // ROW_TILE, name="conformer_prompt")
    gc_tiles = lp // GC_ROW_TILE
    conv_p = a_tail_p[gc_tiles - 1::gc_tiles, CONV_HALO - (CONV_WIDTH - 1):, :]
    m_a, conv_s = _conf_call(a_s, gates, (cache_conv[0], dww, dwb), m_a, *conf_args, row0=tp, n_rows=ts,
                             tm=ls, name="conformer_sample")

    ssd_args = (alog, dskip, row(ssm_norm_w[0]))
    yn, ssm_p = _ssd_call(act_p, dt, z, None, None, None, *ssd_args, row0=0, n_seq=bp, seq_len=lp,
                          ql=SSD_CHUNK, n_fill=ts // SSD_CHUNK, name="ssd_prompt")
    tiles_per_seq = lp // XBC_ROW_TILE
    xbc_p = xbc_tail_p[tiles_per_seq - 1::tiles_per_seq, SUBLANES - (SSM_CONV_WIDTH - 1):, :]
    h0 = state_ssm[0].reshape(bs, SSM_INNER, SSM_STATE)
    yn, xbc_s, ssm_s = _ssd_call(xbc_s, dt, z, (cache_ssm_conv[0], cw, cb), h0, yn, *ssd_args, row0=tp,
                                 n_seq=bs, seq_len=ls, ql=ls, name="ssd_sample")

    m = _mm_call(yn, [(w_ssm_out[0].astype(BF16), 0)], D_MODEL, _ep_merge, BF16, tm=tm, tn=tn // 2,
                 extras=[(m_a, 0), (gates, D_MODEL)], name="ssm_out_merge")
    x1 = _outproj_call(m, w_out[0].astype(BF16), xp, xs)
    y_p, y_s = _mlp_call(x1, row(mlp_norm_w[0]), w_up[0].astype(BF16), w_down[0].astype(BF16),
                         row(final_norm_w), n_prompt_rows=tp)

    hshape = (SSM_HEADS, SSM_HEAD_DIM, SSM_STATE)
    return (y_p.reshape(bp, lp, d), y_s.reshape(bs, ls, d),
            conv_p[None], xbc_p[None], ssm_p.reshape(1, bp, *hshape),
            conv_s[None], xbc_s[None], ssm_s.reshape(1, bs, *hshape))
```

```python
import functools

import jax
import jax.numpy as jnp
from jax import lax
from jax.experimental import pallas as pl
from jax.experimental.pallas import tpu as pltpu

F32 = jnp.float32
BF16 = jnp.bfloat16

D_MODEL = 2048
CONV_DIM = D_MODEL
CONV_WIDTH = 31
SSM_INNER = 2 * D_MODEL
SSM_HEAD_DIM = 64
SSM_HEADS = SSM_INNER // SSM_HEAD_DIM
SSM_GROUPS = 8
HEADS_PER_GROUP = SSM_HEADS // SSM_GROUPS
SSM_STATE = 128
SSM_CONV_WIDTH = 4
SSM_XBC_DIM = SSM_INNER + 2 * SSM_GROUPS * SSM_STATE
D_FF = 4 * D_MODEL
EPS = 1e-5

V7X_LANES = 128
V7X_VMEM_BYTES = 64 * 1024 * 1024
V7X_VMEM_REQUEST_CAP = 56 * 1024 * 1024
COMPILER_TEMP_BYTES = 12 * 1024 * 1024

SSD_CHUNK = 128
CONV_HALO = 32
SSM_HALO = 16
ROW_TILE = 512
MM_ROW_TILE = 1536


def _vmem_limit(block_bytes, scratch_bytes=0):
    need = 2 * sum(block_bytes) + scratch_bytes + COMPILER_TEMP_BYTES
    return int(min(need, V7X_VMEM_REQUEST_CAP))


def _nbytes(shape, dtype):
    n = 1
    for s in shape:
        n *= s
    return n * jnp.dtype(dtype).itemsize


def _sigmoid(x):
    return 1.0 / (1.0 + jnp.exp(-x))


def _softplus(x):
    return jnp.maximum(x, 0.0) + jnp.log1p(jnp.exp(-jnp.abs(x)))


def _rms(x, w):
    ms = jnp.mean(x * x, axis=-1, keepdims=True)
    return x * lax.rsqrt(ms + EPS) * w


def _norm_kernel(xp_ref, xs_ref, w_ref, o_ref, *, n_prompt_tiles):
    i = pl.program_id(0)

    @pl.when(i < n_prompt_tiles)
    def _():
        o_ref[...] = _rms(xp_ref[...], w_ref[...]).astype(o_ref.dtype)

    @pl.when(i >= n_prompt_tiles)
    def _():
        o_ref[...] = _rms(xs_ref[...], w_ref[...]).astype(o_ref.dtype)


def _norm_call(xp, xs, w):
    tp, d = xp.shape
    ts = xs.shape[0]
    assert ts == ROW_TILE and tp % ROW_TILE == 0
    npt = tp // ROW_TILE
    blocks = [_nbytes((ROW_TILE, d), F32)] * 2 + [_nbytes((ROW_TILE, d), BF16)]
    return pl.pallas_call(
        functools.partial(_norm_kernel, n_prompt_tiles=npt),
        grid=(npt + 1,),
        in_specs=[
            pl.BlockSpec((ROW_TILE, d), lambda i: (jnp.minimum(i, npt - 1), 0)),
            pl.BlockSpec((ROW_TILE, d), lambda i: (0, 0)),
            pl.BlockSpec((1, d), lambda i: (0, 0)),
        ],
        out_specs=pl.BlockSpec((ROW_TILE, d), lambda i: (i, 0)),
        out_shape=jax.ShapeDtypeStruct((tp + ts, d), BF16),
        compiler_params=pltpu.CompilerParams(
            dimension_semantics=("arbitrary",), vmem_limit_bytes=_vmem_limit(blocks)),
        name="mix_rmsnorm",
    )(xp, xs, w)


WPREP_COLS = 512
WCOL_VAL = 0
WCOL_GATE = CONV_DIM
WCOL_Z = 2 * CONV_DIM
WCOL_XBC = WCOL_Z + SSM_INNER
WCOL_G = WCOL_XBC + SSM_XBC_DIM
WCOL_DT = WCOL_G + 2 * D_MODEL
WPREP_TOTAL = WCOL_DT + WPREP_COLS


def _wprep_kernel(a_ref, b_ref, o_ref, *, n_plain, n_shift):
    j = pl.program_id(0)

    @pl.when(j < n_plain)
    def _():
        o_ref[...] = a_ref[...].T.astype(BF16)

    @pl.when(jnp.logical_and(j >= n_plain, j < n_plain + n_shift))
    def _():
        src = jnp.concatenate([a_ref[SSM_HEADS:, :], b_ref[:SSM_HEADS, :]], axis=0)
        o_ref[...] = src.T.astype(BF16)

    @pl.when(j == n_plain + n_shift)
    def _():
        r = lax.broadcasted_iota(jnp.int32, a_ref.shape, 0)
        o_ref[...] = jnp.where(r < SSM_HEADS, a_ref[...], 0.0).T.astype(BF16)


def _wprep_call(w_in_t):
    n_src, k = w_in_t.shape
    assert n_src == WCOL_G + SSM_HEADS + 2 * D_MODEL
    n_plain = WCOL_G // WPREP_COLS
    n_shift = 2 * D_MODEL // WPREP_COLS
    last_src = (n_src - 1) // WPREP_COLS
    blocks = [_nbytes((WPREP_COLS, k), F32)] * 3 + [_nbytes((k, WPREP_COLS), BF16)]
    return pl.pallas_call(
        functools.partial(_wprep_kernel, n_plain=n_plain, n_shift=n_shift),
        grid=(n_plain + n_shift + 1,),
        in_specs=[
            pl.BlockSpec((WPREP_COLS, k), lambda j: (jnp.where(j == n_plain + n_shift, n_plain, j), 0)),
            pl.BlockSpec((WPREP_COLS, k),
                         lambda j: (jnp.where(j < n_plain, n_plain, jnp.minimum(j + 1, last_src)), 0)),
        ],
        out_specs=pl.BlockSpec((k, WPREP_COLS), lambda j: (0, j)),
        out_shape=jax.ShapeDtypeStruct((k, WPREP_TOTAL), BF16),
        compiler_params=pltpu.CompilerParams(
            dimension_semantics=("arbitrary",), vmem_limit_bytes=_vmem_limit(blocks)),
        name="w_in_prep",
    )(w_in_t, w_in_t)


def _mm_kernel(*refs, n_w, n_ex, n_p, epilogue):
    lhs_ref = refs[0]
    w_refs = refs[1:1 + n_w]
    ex_refs = refs[1 + n_w:1 + n_w + n_ex]
    p_refs = refs[1 + n_w + n_ex:1 + n_w + n_ex + n_p]
    o_ref = refs[-1]
    lhs = lhs_ref[...]
    accs = [jnp.dot(lhs, w[...], preferred_element_type=F32) for w in w_refs]
    out = epilogue(accs, [e[...].astype(F32) for e in ex_refs], [p[...] for p in p_refs])
    o_ref[...] = out.astype(o_ref.dtype)


def _mm_call(lhs, weights, n, epilogue, out_dtype, *, tm, tn, extras=(), params=(), name):
    t, k = lhs.shape
    assert t % tm == 0 and n % tn == 0
    in_specs = [pl.BlockSpec((tm, k), lambda i, j: (i, 0))]
    operands = [lhs]
    blocks = [_nbytes((tm, k), lhs.dtype)]
    for w, col0 in weights:
        assert col0 % tn == 0
        in_specs.append(pl.BlockSpec((k, tn), lambda i, j, o=col0 // tn: (0, j + o)))
        operands.append(w)
        blocks.append(_nbytes((k, tn), w.dtype))
    for arr, col0 in extras:
        assert col0 % tn == 0
        in_specs.append(pl.BlockSpec((tm, tn), lambda i, j, o=col0 // tn: (i, j + o)))
        operands.append(arr)
        blocks.append(_nbytes((tm, tn), arr.dtype))
    for p in params:
        in_specs.append(pl.BlockSpec((1, tn), lambda i, j: (0, j)))
        operands.append(p)
        blocks.append(_nbytes((8, tn), p.dtype))
    blocks.append(_nbytes((tm, tn), out_dtype))
    return pl.pallas_call(
        functools.partial(_mm_kernel, n_w=len(weights), n_ex=len(extras), n_p=len(params),
                          epilogue=epilogue),
        grid=(t // tm, n // tn),
        in_specs=in_specs,
        out_specs=pl.BlockSpec((tm, tn), lambda i, j: (i, j)),
        out_shape=jax.ShapeDtypeStruct((t, n), out_dtype),
        compiler_params=pltpu.CompilerParams(
            dimension_semantics=("arbitrary", "arbitrary"), vmem_limit_bytes=_vmem_limit(blocks)),
        name=name,
    )(*operands)


def _ep_glu(accs, extras, params):
    return accs[0] * _sigmoid(accs[1])


def _ep_identity(accs, extras, params):
    return accs[0]


def _ep_sigmoid(accs, extras, params):
    return _sigmoid(accs[0])


def _ep_silu(accs, extras, params):
    return accs[0] * _sigmoid(accs[0])


def _ep_softplus_bias(accs, extras, params):
    return _softplus(accs[0] + params[0])


def _ep_merge(accs, extras, params):
    return extras[0] + extras[1] * accs[0]


XBC_ROW_TILE = 1024
XBC_COL_TILE = 2048
XBC_SUB_COLS = 256
SUBLANES = 8


def _xbc_conv_kernel(h_ref, w_ref, cw_ref, cb_ref, act_ref, tail_ref, carry_ref, stage_ref, *, tiles_per_seq):
    i = pl.program_id(0)
    j = pl.program_id(1)
    tm = h_ref.shape[0]
    hist = SSM_CONV_WIDTH - 1

    @pl.when((i % tiles_per_seq) == 0)
    def _():
        tile_cols = pl.ds(pl.multiple_of(j * XBC_COL_TILE, XBC_COL_TILE), XBC_COL_TILE)
        carry_ref[:, tile_cols] = jnp.zeros((SUBLANES, XBC_COL_TILE), F32)

    lhs = h_ref[...]
    for n in range(XBC_COL_TILE // XBC_SUB_COLS):
        sub = slice(n * XBC_SUB_COLS, (n + 1) * XBC_SUB_COLS)
        gcols = pl.ds(pl.multiple_of(j * XBC_COL_TILE + n * XBC_SUB_COLS, XBC_SUB_COLS), XBC_SUB_COLS)
        st = stage_ref.at[n % 2]
        st[SUBLANES:SUBLANES + tm, :] = jnp.dot(lhs, w_ref[:, sub], preferred_element_type=F32)
        st[0:SUBLANES, :] = carry_ref[:, gcols]
        last = st[tm:tm + SUBLANES, :]
        carry_ref[:, gcols] = last
        tail_ref[0, :, sub] = last
        conv = cb_ref[:, sub] + cw_ref[hist:hist + 1, sub] * st[SUBLANES:SUBLANES + tm, :]
        for k in range(hist):
            conv = conv + cw_ref[k:k + 1, sub] * st[SUBLANES - hist + k:SUBLANES - hist + k + tm, :]
        act_ref[:, sub] = (conv * _sigmoid(conv)).astype(act_ref.dtype)


def _xbc_conv_call(h, w, w_col0, cw, cb, *, n_seq, seq_len):
    k = h.shape[1]
    n = cw.shape[1]
    tm, tn = XBC_ROW_TILE, XBC_COL_TILE
    assert seq_len % tm == 0 and n % tn == 0 and w_col0 % tn == 0
    w_blk0 = w_col0 // tn
    tiles_per_seq = seq_len // tm
    stage_shape = (2, tm + SUBLANES, XBC_SUB_COLS)
    blocks = [_nbytes((tm, k), BF16), _nbytes((k, tn), BF16), _nbytes((8, tn), F32) * 3,
              _nbytes((tm, tn), BF16)]
    return pl.pallas_call(
        functools.partial(_xbc_conv_kernel, tiles_per_seq=tiles_per_seq),
        grid=(n_seq * tiles_per_seq, n // tn),
        in_specs=[
            pl.BlockSpec((tm, k), lambda i, j: (i, 0)),
            pl.BlockSpec((k, tn), lambda i, j: (0, j + w_blk0)),
            pl.BlockSpec((SSM_CONV_WIDTH, tn), lambda i, j: (0, j)),
            pl.BlockSpec((1, tn), lambda i, j: (0, j)),
        ],
        out_specs=[
            pl.BlockSpec((tm, tn), lambda i, j: (i, j)),
            pl.BlockSpec((1, SUBLANES, tn), lambda i, j: (i, 0, j)),
        ],
        out_shape=[
            jax.ShapeDtypeStruct((n_seq * seq_len, n), BF16),
            jax.ShapeDtypeStruct((n_seq * tiles_per_seq, SUBLANES, n), F32),
        ],
        scratch_shapes=[pltpu.VMEM((SUBLANES, n), F32), pltpu.VMEM(stage_shape, F32)],
        compiler_params=pltpu.CompilerParams(
            dimension_semantics=("arbitrary", "arbitrary"),
            vmem_limit_bytes=_vmem_limit(blocks, _nbytes((SUBLANES, n), F32) + _nbytes(stage_shape, F32))),
        name="in_proj_xbc_conv",
    )(h, w, cw, cb)


CONV_ROW_BLOCK = 64
LN_ROW_BLOCK = 256


def _conv31_block(src, zs, dww_ref, dwb_ref, cols, lanes, tm, store):
    hist = CONV_WIDTH - 1
    base = CONV_HALO - hist
    rb = min(CONV_ROW_BLOCK, tm)
    for s in range(SUBLANES):
        rows = tm + SUBLANES * ((hist - s) // SUBLANES)
        zs[s, 0:rows, :] = src[base + s:base + s + rows, lanes]
    bias = jnp.broadcast_to(dwb_ref[:, cols], (rb, V7X_LANES))
    for r0 in range(0, tm, rb):
        acc = bias
        for k in range(CONV_WIDTH):
            s, q = k % SUBLANES, k // SUBLANES
            acc = acc + dww_ref[k:k + 1, cols] * zs[s, r0 + SUBLANES * q:r0 + SUBLANES * q + rb, :]
        store(r0, rb, acc)


def _conf_kernel(*refs, tm, n_tiles, n_fill, conv_in_kernel, aliased):
    if n_fill == 0:
        _conf_body(refs, tm, conv_in_kernel, aliased)
        return
    i = pl.program_id(0)
    o_ref = refs[5 + 3 * int(conv_in_kernel) + int(aliased)]

    @pl.when(i < n_tiles)
    def _():
        _conf_body(refs, tm, conv_in_kernel, aliased)

    @pl.when(i >= n_tiles)
    def _():
        o_ref[...] = jnp.zeros_like(o_ref)


def _conf_body(refs, tm, conv_in_kernel, aliased):
    it = iter(refs)
    src_ref = next(it)
    if conv_in_kernel:
        hist_ref = next(it)
        dww_ref = next(it)
        dwb_ref = next(it)
    g_ref = next(it)
    lnw_ref = next(it)
    lnb_ref = next(it)
    w_ref = next(it)
    if aliased:
        next(it)
    o_ref = next(it)
    if conv_in_kernel:
        cache_ref = next(it)
        xs_ref = next(it)
        zs_ref = next(it)
        y_ref = next(it)
    s_ref = next(it)
    hist = CONV_WIDTH - 1

    if conv_in_kernel:
        xs_ref[0:CONV_HALO - hist, :] = jnp.zeros((CONV_HALO - hist, CONV_DIM), F32)
        xs_ref[CONV_HALO - hist:CONV_HALO, :] = hist_ref[0]
        xs_ref[CONV_HALO:CONV_HALO + tm, :] = src_ref[...].astype(F32)
        cache_ref[0] = xs_ref[CONV_HALO + tm - hist:CONV_HALO + tm, :]

        def conv_cols(cb, carry):
            cols = pl.ds(pl.multiple_of(cb * V7X_LANES, V7X_LANES), V7X_LANES)

            def store(r0, rb, acc):
                y_ref[r0:r0 + rb, cols] = acc

            _conv31_block(xs_ref, zs_ref, dww_ref, dwb_ref, cols, cols, tm, store)
            return carry

        lax.fori_loop(0, CONV_DIM // V7X_LANES, conv_cols, 0)

    ln_rows_per_step = min(LN_ROW_BLOCK, tm)

    def ln_rows(rb, carry):
        r0 = pl.multiple_of(rb * ln_rows_per_step, ln_rows_per_step)
        if conv_in_kernel:
            y = y_ref[pl.ds(r0, ln_rows_per_step), :]
        else:
            y = src_ref[pl.ds(r0, ln_rows_per_step), :].astype(F32)
        mu = jnp.mean(y, axis=-1, keepdims=True)
        yc = y - mu
        var = jnp.mean(yc * yc, axis=-1, keepdims=True)
        yn = yc * lax.rsqrt(var + EPS) * lnw_ref[...] + lnb_ref[...]
        s_ref[pl.ds(r0, ln_rows_per_step), :] = (yn * _sigmoid(yn)).astype(BF16)
        return carry

    lax.fori_loop(0, tm // ln_rows_per_step, ln_rows, 0)

    ya = jnp.dot(s_ref[...], w_ref[...], preferred_element_type=F32)
    o_ref[...] = (g_ref[...].astype(F32) * ya).astype(o_ref.dtype)


def _conf_call(src, gates, conv, prev, lnw, lnb, w, *, row0, n_rows, tm, name, n_fill=0):
    assert n_rows % tm == 0 and row0 % tm == 0 and src.shape[0] == n_rows
    n_tiles = n_rows // tm
    conv_in_kernel = conv is not None
    assert not (conv_in_kernel and n_fill)
    hist = CONV_WIDTH - 1
    blk0 = row0 // tm

    def tile(i):
        return jnp.minimum(i, n_tiles - 1)

    in_specs = [pl.BlockSpec((tm, CONV_DIM), lambda i: (tile(i), 0))]
    operands = [src]
    blocks = [_nbytes((tm, CONV_DIM), BF16), _nbytes((tm, D_MODEL), BF16) * 2, _nbytes((8, CONV_DIM), F32) * 2,
              _nbytes((CONV_DIM, D_MODEL), BF16)]
    scratch_dims = [((tm, CONV_DIM), BF16)]
    out_specs = [pl.BlockSpec((tm, D_MODEL), lambda i: (blk0 + i, 0))]
    out_shape = [jax.ShapeDtypeStruct((gates.shape[0], D_MODEL), BF16)]
    if conv_in_kernel:
        assert tm % CONV_HALO == 0
        in_specs += [
            pl.BlockSpec((1, hist, CONV_DIM), lambda i: (i, 0, 0)),
            pl.BlockSpec((CONV_WIDTH, CONV_DIM), lambda i: (0, 0)),
            pl.BlockSpec((1, CONV_DIM), lambda i: (0, 0)),
        ]
        operands += list(conv)
        blocks += [_nbytes((32, CONV_DIM), F32) * 3, _nbytes((8, CONV_DIM), F32)]
        scratch_dims = [((CONV_HALO + tm, CONV_DIM), F32), ((SUBLANES, CONV_HALO + tm, V7X_LANES), F32),
                        ((tm, CONV_DIM), F32)] + scratch_dims
        out_specs.append(pl.BlockSpec((1, hist, CONV_DIM), lambda i: (i, 0, 0)))
        out_shape.append(jax.ShapeDtypeStruct((n_tiles, hist, CONV_DIM), F32))
    in_specs += [
        pl.BlockSpec((tm, D_MODEL), lambda i: (blk0 + tile(i), 0)),
        pl.BlockSpec((1, CONV_DIM), lambda i: (0, 0)),
        pl.BlockSpec((1, CONV_DIM), lambda i: (0, 0)),
        pl.BlockSpec((CONV_DIM, D_MODEL), lambda i: (0, 0)),
    ]
    operands += [gates, lnw, lnb, w]
    aliases = {}
    if prev is not None:
        in_specs.append(pl.BlockSpec(memory_space=pl.ANY))
        operands.append(prev)
        aliases = {len(operands) - 1: 0}
    return pl.pallas_call(
        functools.partial(_conf_kernel, tm=tm, n_tiles=n_tiles, n_fill=n_fill, conv_in_kernel=conv_in_kernel,
                          aliased=prev is not None),
        grid=(n_tiles + n_fill,),
        in_specs=in_specs,
        out_specs=out_specs,
        out_shape=out_shape,
        scratch_shapes=[pltpu.VMEM(shape, dt) for shape, dt in scratch_dims],
        input_output_aliases=aliases,
        compiler_params=pltpu.CompilerParams(
            dimension_semantics=("arbitrary",),
            vmem_limit_bytes=_vmem_limit(blocks, sum(_nbytes(sh, dt) for sh, dt in scratch_dims))),
        name=name,
    )(*operands)


GC_ROW_TILE = 512
GC_COL_TILE = 2048
GC_SUB_COLS = 256


def _glu_conv_kernel(h_ref, wv_ref, wg_ref, dww_ref, dwb_ref, y_ref, tail_ref, carry_ref, xs_ref, zs_ref,
                     *, tiles_per_seq):
    i = pl.program_id(0)
    j = pl.program_id(1)
    tm = h_ref.shape[0]

    @pl.when((i % tiles_per_seq) == 0)
    def _():
        tile_cols = pl.ds(pl.multiple_of(j * GC_COL_TILE, GC_COL_TILE), GC_COL_TILE)
        carry_ref[:, tile_cols] = jnp.zeros((CONV_HALO, GC_COL_TILE), F32)

    lhs = h_ref[...]
    for n in range(GC_COL_TILE // GC_SUB_COLS):
        sub = slice(n * GC_SUB_COLS, (n + 1) * GC_SUB_COLS)
        gcols = pl.ds(pl.multiple_of(j * GC_COL_TILE + n * GC_SUB_COLS, GC_SUB_COLS), GC_SUB_COLS)
        xs = xs_ref.at[n % 2]
        val = jnp.dot(lhs, wv_ref[:, sub], preferred_element_type=F32)
        gate = jnp.dot(lhs, wg_ref[:, sub], preferred_element_type=F32)
        xs[CONV_HALO:CONV_HALO + tm, :] = val * _sigmoid(gate)
        xs[0:CONV_HALO, :] = carry_ref[:, gcols]
        last = xs[tm:tm + CONV_HALO, :]
        carry_ref[:, gcols] = last
        tail_ref[0, :, sub] = last
        for half in range(GC_SUB_COLS // V7X_LANES):
            lanes = slice(half * V7X_LANES, (half + 1) * V7X_LANES)
            cols = slice(n * GC_SUB_COLS + half * V7X_LANES, n * GC_SUB_COLS + (half + 1) * V7X_LANES)

            def store(r0, rb, acc, cols=cols):
                y_ref[r0:r0 + rb, cols] = acc.astype(y_ref.dtype)

            _conv31_block(xs, zs_ref.at[half], dww_ref, dwb_ref, cols, lanes, tm, store)


def _glu_conv_call(h, w, wv_col0, wg_col0, dww, dwb, *, n_seq, seq_len):
    k = h.shape[1]
    n = dww.shape[1]
    tm, tn = GC_ROW_TILE, GC_COL_TILE
    assert seq_len % tm == 0 and n % tn == 0 and wv_col0 % tn == 0 and wg_col0 % tn == 0
    wv_blk0, wg_blk0 = wv_col0 // tn, wg_col0 // tn
    tiles_per_seq = seq_len // tm
    blocks = [_nbytes((tm, k), BF16), _nbytes((k, tn), BF16) * 2, _nbytes((32, tn), F32) * 2,
              _nbytes((tm, tn), BF16), _nbytes((CONV_HALO, tn), F32)]
    scratch_dims = [((CONV_HALO, n), F32), ((2, CONV_HALO + tm, GC_SUB_COLS), F32),
                    ((GC_SUB_COLS // V7X_LANES, SUBLANES, CONV_HALO + tm, V7X_LANES), F32)]
    return pl.pallas_call(
        functools.partial(_glu_conv_kernel, tiles_per_seq=tiles_per_seq),
        grid=(n_seq * tiles_per_seq, n // tn),
        in_specs=[
            pl.BlockSpec((tm, k), lambda i, j: (i, 0)),
            pl.BlockSpec((k, tn), lambda i, j: (0, j + wv_blk0)),
            pl.BlockSpec((k, tn), lambda i, j: (0, j + wg_blk0)),
            pl.BlockSpec((CONV_WIDTH, tn), lambda i, j: (0, j)),
            pl.BlockSpec((1, tn), lambda i, j: (0, j)),
        ],
        out_specs=[
            pl.BlockSpec((tm, tn), lambda i, j: (i, j)),
            pl.BlockSpec((1, CONV_HALO, tn), lambda i, j: (i, 0, j)),
        ],
        out_shape=[
            jax.ShapeDtypeStruct((n_seq * seq_len, n), BF16),
            jax.ShapeDtypeStruct((n_seq * tiles_per_seq, CONV_HALO, n), F32),
        ],
        scratch_shapes=[pltpu.VMEM(shape, dt) for shape, dt in scratch_dims],
        compiler_params=pltpu.CompilerParams(
            dimension_semantics=("arbitrary", "arbitrary"),
            vmem_limit_bytes=_vmem_limit(blocks, sum(_nbytes(sh, dt) for sh, dt in scratch_dims))),
        name="in_proj_glu_conv",
    )(h, w, w, dww, dwb)


SSM_CONV_COL_BLOCK = 512
GROUP_COLS = HEADS_PER_GROUP * SSM_HEAD_DIM
B_COL0 = SSM_INNER
C_COL0 = SSM_INNER + SSM_GROUPS * SSM_STATE
NEG_BIG = -1e30
LOG2_E = 1.4426950408889634


def _split3(x):
    p1 = x.astype(BF16)
    r1 = x - p1.astype(F32)
    p2 = r1.astype(BF16)
    r2 = r1 - p2.astype(F32)
    return p1, p2, r2.astype(BF16)


def _ssd_kernel(*refs, ql, n_chunks, fill_per_seq, has_h0, conv_in_kernel, aliased):
    if fill_per_seq == 0:
        _ssd_body(refs, ql, n_chunks, has_h0, conv_in_kernel, aliased)
        return
    c_idx = pl.program_id(1)
    y_ref = refs[6 + 3 * int(conv_in_kernel) + int(n_chunks > 1) + int(has_h0) + int(aliased)]

    @pl.when(c_idx < n_chunks)
    def _():
        _ssd_body(refs, ql, n_chunks, has_h0, conv_in_kernel, aliased)

    @pl.when(c_idx >= n_chunks)
    def _():
        y_ref[...] = jnp.zeros_like(y_ref)


def _ssd_body(refs, ql, n_chunks, has_h0, conv_in_kernel, aliased):
    it = iter(refs)
    src_ref = next(it)
    if conv_in_kernel:
        hist_ref = next(it)
        cw_ref = next(it)
        cb_ref = next(it)
    dt_ref = next(it)
    dtn_ref = next(it) if n_chunks > 1 else None
    z_ref = next(it)
    alog_ref = next(it)
    dskip_ref = next(it)
    nw_ref = next(it)
    h0_ref = next(it) if has_h0 else None
    if aliased:
        next(it)
    y_ref = next(it)
    cache_ref = next(it) if conv_in_kernel else None
    hout_ref = next(it)
    xs_ref = next(it) if conv_in_kernel else None
    act_ref = next(it)
    acs_ref = next(it)
    rowt_ref = next(it)
    wendt_ref = next(it)
    ht_ref = next(it)

    q = SSD_CHUNK
    c_idx = pl.program_id(1)
    hist = SSM_CONV_WIDTH - 1

    @pl.when(c_idx == 0)
    def _():
        for g in range(SSM_GROUPS):
            if has_h0:
                ht_ref[g] = h0_ref[0, g * GROUP_COLS:(g + 1) * GROUP_COLS, :].T
            else:
                ht_ref[g] = jnp.zeros((SSM_STATE, GROUP_COLS), F32)

    if conv_in_kernel:
        assert n_chunks == 1 and ql % SUBLANES == 0
        xs_ref[0:SSM_HALO - hist, :] = jnp.zeros((SSM_HALO - hist, SSM_XBC_DIM), F32)
        xs_ref[SSM_HALO - hist:SSM_HALO, :] = hist_ref[0]
        xs_ref[SSM_HALO:SSM_HALO + ql, :] = src_ref[...].astype(F32)
        if ql < q:
            act_ref[ql:q, :] = jnp.zeros((q - ql, SSM_XBC_DIM), F32)
        base = SSM_HALO - hist

        def conv_cols(cb, carry):
            c0 = pl.multiple_of(cb * SSM_CONV_COL_BLOCK, SSM_CONV_COL_BLOCK)
            cols = pl.ds(c0, SSM_CONV_COL_BLOCK)
            acc = jnp.broadcast_to(cb_ref[:, cols], (ql, SSM_CONV_COL_BLOCK))
            for k in range(SSM_CONV_WIDTH):
                acc = acc + cw_ref[k:k + 1, cols] * xs_ref[base + k:base + k + ql, cols]
            act_ref[0:ql, cols] = acc * _sigmoid(acc)
            return carry

        lax.fori_loop(0, SSM_XBC_DIM // SSM_CONV_COL_BLOCK, conv_cols, 0)
    else:
        assert ql == q
        act_ref[...] = src_ref[...].astype(F32)

    r_i = lax.broadcasted_iota(jnp.int32, (q, q), 0)
    c_i = lax.broadcasted_iota(jnp.int32, (q, q), 1)
    causal = r_i >= c_i

    def compute_decay(d_ref, slot):
        dt = d_ref[...]
        if ql < q:
            dt = jnp.concatenate([dt, jnp.zeros((q - ql, V7X_LANES), F32)], axis=0)
        dta = dt * (-jnp.exp(alog_ref[...]))
        tri = jnp.where(causal, 1.0, 0.0).astype(BF16)
        p1, p2, p3 = _split3(dta)
        acs = (jnp.dot(tri, p1, preferred_element_type=F32) + jnp.dot(tri, p2, preferred_element_type=F32)
               + jnp.dot(tri, p3, preferred_element_type=F32))
        acs2 = acs * LOG2_E
        acs_ref[slot] = acs2
        acs2_t = acs2.T
        rowt = acs2_t - jnp.log2(dt.T)
        rowt_ref[slot] = rowt
        wendt_ref[slot] = jnp.exp2(acs2_t[:, q - 1:q] - rowt)

    if dtn_ref is None:
        slot = 0
        compute_decay(dt_ref, 0)
    else:
        slot = c_idx % 2

        @pl.when(c_idx == 0)
        def _():
            compute_decay(dt_ref, 0)

    left = c_i < SSM_HEAD_DIM
    left_row = left[0:1, :]
    c_l = lax.broadcasted_iota(jnp.int32, (ql, q), 1)
    causal_l = lax.broadcasted_iota(jnp.int32, (ql, q), 0) >= c_l
    left_l = c_l < SSM_HEAD_DIM

    for g in range(SSM_GROUPS):
        bg = act_ref[:, B_COL0 + g * SSM_STATE:B_COL0 + (g + 1) * SSM_STATE]
        cg = act_ref[0:ql, C_COL0 + g * SSM_STATE:C_COL0 + (g + 1) * SSM_STATE].astype(BF16)
        cbm = lax.dot_general(cg, bg.astype(BF16), (((1,), (1,)), ((), ())),
                              preferred_element_type=F32)
        bg_t = bg.T
        ht_g = ht_ref[g]
        yoff = jnp.dot(cg, ht_g.astype(BF16), preferred_element_type=F32)
        y_parts, inc_parts, dec_parts = [], [], []
        for j in range(HEADS_PER_GROUP // 2):
            h_a = g * HEADS_PER_GROUP + 2 * j
            h_b = h_a + 1
            lo = h_a * SSM_HEAD_DIM
            xp = act_ref[:, lo:lo + V7X_LANES]
            lhs_parts = []
            cols = []
            for h in (h_a, h_b):
                col = jnp.broadcast_to(acs_ref[slot, 0:ql, h:h + 1], (ql, q))
                cols.append(col)
                seg = jnp.where(causal_l, col - rowt_ref[slot, h:h + 1, :], NEG_BIG)
                m_h = cbm * jnp.exp2(seg)
                w_h = bg_t * wendt_ref[slot, h:h + 1, :]
                lhs_parts.append(jnp.concatenate([m_h, w_h], axis=0).astype(BF16))
            rhs = jnp.concatenate([jnp.where(left, xp, 0.0).astype(BF16),
                                   jnp.where(left, 0.0, xp).astype(BF16)], axis=0)
            res = jnp.dot(jnp.concatenate(lhs_parts, axis=1), rhs, preferred_element_type=F32)
            colsel = jnp.where(left_l, cols[0], cols[1])
            y_parts.append(res[0:ql] + jnp.exp2(colsel) * yoff[:, j * V7X_LANES:(j + 1) * V7X_LANES]
                           + dskip_ref[:, lo:lo + V7X_LANES] * xp[0:ql])
            inc_parts.append(res[ql:ql + q])
            last_a = jnp.broadcast_to(acs_ref[slot, q - 1:q, h_a:h_a + 1], (1, V7X_LANES))
            last_b = jnp.broadcast_to(acs_ref[slot, q - 1:q, h_b:h_b + 1], (1, V7X_LANES))
            dec_parts.append(jnp.exp2(jnp.where(left_row, last_a, last_b)))
        ht_ref[g] = jnp.concatenate(dec_parts, axis=1) * ht_g + jnp.concatenate(inc_parts, axis=1)
        gc = slice(g * GROUP_COLS, (g + 1) * GROUP_COLS)
        yg = jnp.concatenate(y_parts, axis=1)
        yz = yg * z_ref[:, gc].astype(F32)
        ms = jnp.mean(yz * yz, axis=-1, keepdims=True)
        y_ref[:, gc] = (yz * lax.rsqrt(ms + EPS) * nw_ref[:, gc]).astype(y_ref.dtype)

    if dtn_ref is not None:
        compute_decay(dtn_ref, 1 - slot)

    @pl.when(c_idx == n_chunks - 1)
    def _():
        if conv_in_kernel:
            cache_ref[0] = xs_ref[SSM_HALO + ql - hist:SSM_HALO + ql, :]
        for g in range(SSM_GROUPS):
            hout_ref[0, g * GROUP_COLS:(g + 1) * GROUP_COLS, :] = ht_ref[g].T


def _ssd_call(src, dt, z, conv, h0, prev, alog, dskip, nw, *, row0, n_seq, seq_len, ql, name, n_fill=0):
    assert seq_len % ql == 0 and row0 % ql == 0 and ql % SSM_HALO == 0 and ql <= SSD_CHUNK
    assert src.shape[0] == n_seq * seq_len
    n_chunks = seq_len // ql
    conv_in_kernel = conv is not None
    has_h0 = h0 is not None
    hist = SSM_CONV_WIDTH - 1
    blk0 = row0 // ql
    q = SSD_CHUNK

    assert n_fill % n_seq == 0
    fill_per_seq = n_fill // n_seq

    def chunk(c):
        return jnp.minimum(c, n_chunks - 1)

    def row_blk(b, c):
        return blk0 + b * n_chunks + chunk(c)

    def out_blk(b, c):
        fill_blk = blk0 + n_seq * n_chunks + b * fill_per_seq + (c - n_chunks)
        return jnp.where(c < n_chunks, row_blk(b, c), fill_blk)

    in_specs = [pl.BlockSpec((ql, SSM_XBC_DIM), lambda b, c: (b * n_chunks + chunk(c), 0))]
    operands = [src]
    if conv_in_kernel:
        in_specs += [
            pl.BlockSpec((1, hist, SSM_XBC_DIM), lambda b, c: (b, 0, 0)),
            pl.BlockSpec((SSM_CONV_WIDTH, SSM_XBC_DIM), lambda b, c: (0, 0)),
            pl.BlockSpec((1, SSM_XBC_DIM), lambda b, c: (0, 0)),
        ]
        operands += list(conv)
    in_specs.append(pl.BlockSpec((ql, V7X_LANES), lambda b, c: (row_blk(b, c), 0)))
    operands.append(dt)
    if n_chunks > 1:
        in_specs.append(pl.BlockSpec((ql, V7X_LANES), lambda b, c: (row_blk(b, c + 1), 0)))
        operands.append(dt)
    in_specs += [
        pl.BlockSpec((ql, SSM_INNER), lambda b, c: (row_blk(b, c), 0)),
        pl.BlockSpec((1, V7X_LANES), lambda b, c: (0, 0)),
        pl.BlockSpec((1, SSM_INNER), lambda b, c: (0, 0)),
        pl.BlockSpec((1, SSM_INNER), lambda b, c: (0, 0)),
    ]
    operands += [z, alog, dskip, nw]
    if has_h0:
        in_specs.append(pl.BlockSpec((1, SSM_INNER, SSM_STATE), lambda b, c: (b, 0, 0)))
        operands.append(h0)
    aliases = {}
    if prev is not None:
        in_specs.append(pl.BlockSpec(memory_space=pl.ANY))
        operands.append(prev)
        aliases = {len(operands) - 1: 0}
    state_bytes = _nbytes((SSM_INNER, SSM_STATE), F32)
    blocks = [_nbytes((ql, SSM_XBC_DIM), BF16), _nbytes((SSM_HALO, SSM_XBC_DIM), F32),
              _nbytes((ql, V7X_LANES), F32), _nbytes((ql, SSM_INNER), BF16),
              _nbytes((8, SSM_XBC_DIM), F32) * 2, _nbytes((8, SSM_INNER), F32) * 2,
              _nbytes((ql, SSM_INNER), BF16), _nbytes((8, SSM_XBC_DIM), F32),
              state_bytes * (2 if has_h0 else 1)]
    scratch_dims = ([((SSM_HALO + q, SSM_XBC_DIM), F32)] if conv_in_kernel else []) + [
        ((q, SSM_XBC_DIM), F32),
        ((2, q, V7X_LANES), F32),
        ((2, V7X_LANES, q), F32),
        ((2, V7X_LANES, q), F32),
        ((SSM_GROUPS, SSM_STATE, GROUP_COLS), F32),
    ]
    out_specs = [pl.BlockSpec((ql, SSM_INNER), lambda b, c: (out_blk(b, c), 0))]
    out_shape = [jax.ShapeDtypeStruct((dt.shape[0], SSM_INNER), BF16)]
    if conv_in_kernel:
        out_specs.append(pl.BlockSpec((1, hist, SSM_XBC_DIM), lambda b, c: (b, 0, 0)))
        out_shape.append(jax.ShapeDtypeStruct((n_seq, hist, SSM_XBC_DIM), F32))
    out_specs.append(pl.BlockSpec((1, SSM_INNER, SSM_STATE), lambda b, c: (b, 0, 0)))
    out_shape.append(jax.ShapeDtypeStruct((n_seq, SSM_INNER, SSM_STATE), F32))
    return pl.pallas_call(
        functools.partial(_ssd_kernel, ql=ql, n_chunks=n_chunks, fill_per_seq=fill_per_seq, has_h0=has_h0,
                          conv_in_kernel=conv_in_kernel, aliased=prev is not None),
        grid=(n_seq, n_chunks + fill_per_seq),
        in_specs=in_specs,
        out_specs=out_specs,
        out_shape=out_shape,
        scratch_shapes=[pltpu.VMEM(shape, dt_) for shape, dt_ in scratch_dims],
        input_output_aliases=aliases,
        compiler_params=pltpu.CompilerParams(
            dimension_semantics=("arbitrary", "arbitrary"),
            vmem_limit_bytes=_vmem_limit(blocks, sum(_nbytes(sh, dt_) for sh, dt_ in scratch_dims))),
        name=name,
    )(*operands)


def _outproj_kernel(m_ref, w_ref, xp_ref, xs_ref, o_ref, *, n_prompt_tiles):
    i = pl.program_id(0)
    acc = jnp.dot(m_ref[...], w_ref[...], preferred_element_type=F32)

    @pl.when(i < n_prompt_tiles)
    def _():
        o_ref[...] = xp_ref[...] + acc

    @pl.when(i >= n_prompt_tiles)
    def _():
        o_ref[...] = xs_ref[...] + acc


def _outproj_call(m, w, xp, xs):
    t, d = m.shape
    npt = xp.shape[0] // ROW_TILE
    blocks = [_nbytes((ROW_TILE, d), BF16), _nbytes((d, d), BF16)] + [_nbytes((ROW_TILE, d), F32)] * 3
    return pl.pallas_call(
        functools.partial(_outproj_kernel, n_prompt_tiles=npt),
        grid=(t // ROW_TILE,),
        in_specs=[
            pl.BlockSpec((ROW_TILE, d), lambda i: (i, 0)),
            pl.BlockSpec((d, d), lambda i: (0, 0)),
            pl.BlockSpec((ROW_TILE, d), lambda i: (jnp.minimum(i, npt - 1), 0)),
            pl.BlockSpec((ROW_TILE, d), lambda i: (0, 0)),
        ],
        out_specs=pl.BlockSpec((ROW_TILE, d), lambda i: (i, 0)),
        out_shape=jax.ShapeDtypeStruct((t, d), F32),
        compiler_params=pltpu.CompilerParams(
            dimension_semantics=("arbitrary",), vmem_limit_bytes=_vmem_limit(blocks)),
        name="out_proj_residual",
    )(m, w, xp, xs)


MLP_FF_TILE = 1024


def _mlp_kernel(x1_ref, nw_ref, wup_ref, wdn_ref, fw_ref, yp_ref, ys_ref, h2_ref, acc_ref,
                *, n_prompt_tiles, n_ff):
    i = pl.program_id(0)
    f = pl.program_id(1)

    @pl.when(f == 0)
    def _():
        h2_ref[...] = _rms(x1_ref[...], nw_ref[...]).astype(BF16)
        acc_ref[...] = jnp.zeros_like(acc_ref)

    hid = jnp.dot(h2_ref[...], wup_ref[...], preferred_element_type=F32)
    hid = jnp.square(jnp.maximum(hid, 0.0))
    acc_ref[...] += jnp.dot(hid.astype(BF16), wdn_ref[...], preferred_element_type=F32)

    @pl.when(f == n_ff - 1)
    def _():
        out = _rms(x1_ref[...] + acc_ref[...], fw_ref[...])

        @pl.when(i < n_prompt_tiles)
        def _():
            yp_ref[...] = out

        @pl.when(i >= n_prompt_tiles)
        def _():
            ys_ref[...] = out


def _mlp_call(x1, nw, wup, wdn, fw, *, n_prompt_rows):
    t, d = x1.shape
    npt = n_prompt_rows // ROW_TILE
    n_ff = D_FF // MLP_FF_TILE
    blocks = [_nbytes((ROW_TILE, d), F32), _nbytes((d, MLP_FF_TILE), BF16),
              _nbytes((MLP_FF_TILE, d), BF16), _nbytes((ROW_TILE, d), F32) * 2]
    scratch_bytes = _nbytes((ROW_TILE, d), BF16) + _nbytes((ROW_TILE, d), F32)
    return pl.pallas_call(
        functools.partial(_mlp_kernel, n_prompt_tiles=npt, n_ff=n_ff),
        grid=(t // ROW_TILE, n_ff),
        in_specs=[
            pl.BlockSpec((ROW_TILE, d), lambda i, f: (i, 0)),
            pl.BlockSpec((1, d), lambda i, f: (0, 0)),
            pl.BlockSpec((d, MLP_FF_TILE), lambda i, f: (0, f)),
            pl.BlockSpec((MLP_FF_TILE, d), lambda i, f: (f, 0)),
            pl.BlockSpec((1, d), lambda i, f: (0, 0)),
        ],
        out_specs=[
            pl.BlockSpec((ROW_TILE, d), lambda i, f: (jnp.minimum(i, npt - 1), 0)),
            pl.BlockSpec((ROW_TILE, d), lambda i, f: (0, 0)),
        ],
        out_shape=[
            jax.ShapeDtypeStruct((n_prompt_rows, d), F32),
            jax.ShapeDtypeStruct((t - n_prompt_rows, d), F32),
        ],
        scratch_shapes=[pltpu.VMEM((ROW_TILE, d), BF16), pltpu.VMEM((ROW_TILE, d), F32)],
        compiler_params=pltpu.CompilerParams(
            dimension_semantics=("arbitrary", "arbitrary"),
            vmem_limit_bytes=_vmem_limit(blocks, scratch_bytes)),
        name="mlp_final_norm",
    )(x1, nw, wup, wdn, fw)


def kernel(x_prompt, x_sample, cache_conv, cache_ssm_conv, state_ssm, mix_norm_w, w_in, conv_dw_w,
           conv_dw_b, conv_ln_w, conv_ln_b, w_conv_out, ssm_conv_w, ssm_conv_b, dt_bias, a_log, d_skip,
           ssm_norm_w, w_ssm_out, w_out, mlp_norm_w, w_up, w_down, final_norm_w):
    bp, lp, d = x_prompt.shape
    bs, ls, _ = x_sample.shape
    tp, ts = bp * lp, bs * ls
    xp = x_prompt.reshape(tp, d)
    xs = x_sample.reshape(ts, d)

    wi = _wprep_call(w_in[0].T)
    dt_b =jnp.pad(dt_bias[0], (0, V7X_LANES - SSM_HEADS)).reshape(1, V7X_LANES)
    alog = jnp.pad(a_log[0], (0, V7X_LANES - SSM_HEADS)).reshape(1, V7X_LANES)
    dskip = jnp.repeat(d_skip[0], SSM_HEAD_DIM).reshape(1, SSM_INNER)
    row = lambda v: v.reshape(1, -1)

    h = _norm_call(xp, xs, row(mix_norm_w[0]))

    tm, tn = MM_ROW_TILE, 1024
    dww, dwb = conv_dw_w[0], row(conv_dw_b[0])
    yconv_p, a_tail_p = _glu_conv_call(h, wi, WCOL_VAL, WCOL_GATE, dww, dwb, n_seq=bp, seq_len=lp)
    h_s = h[tp:]
    a_s = _mm_call(h_s, [(wi, WCOL_VAL), (wi, WCOL_GATE)], CONV_DIM, _ep_glu, BF16, tm=ts, tn=tn // 2,
                   name="in_proj_glu_sample")
    zs = _mm_call(h, [(wi, WCOL_Z)], SSM_INNER, _ep_silu, BF16, tm=tm, tn=tn, name="in_proj_silu_z")
    cw, cb = ssm_conv_w[0], row(ssm_conv_b[0])
    act_p, xbc_tail_p = _xbc_conv_call(h, wi, WCOL_XBC, cw, cb, n_seq=bp, seq_len=lp)
    xbc_s = _mm_call(h_s, [(wi, WCOL_XBC)], SSM_XBC_DIM, _ep_identity, BF16, tm=ts, tn=tn,
                     name="in_proj_xbc_sample")
    dt = _mm_call(h, [(wi, WCOL_DT)], V7X_LANES, _ep_softplus_bias, F32, tm=tm, tn=V7X_LANES, params=[dt_b],
                  name="in_proj_dt")
    gates = _mm_call(h, [(wi, WCOL_G)], 2 * D_MODEL, _ep_sigmoid, BF16, tm=tm, tn=tn, name="in_proj_gates")

    conf_args = (row(conv_ln_w[0]), row(conv_ln_b[0]), w_conv_out[0].astype(BF16))
    (m_a,) = _conf_call(yconv_p, gates, None, None, *conf_args, row0=0, n_rows=tp, tm=ROW_TILE,
                        n_fill=ts // ROW_TILE, name="conformer_prompt")
    gc_tiles = lp // GC_ROW_TILE
    conv_p = a_tail_p[gc_tiles - 1::gc_tiles, CONV_HALO - (CONV_WIDTH - 1):, :]
    m_a, conv_s = _conf_call(a_s, gates, (cache_conv[0], dww, dwb), m_a, *conf_args, row0=tp, n_rows=ts,
                             tm=ls, name="conformer_sample")

    ssd_args = (alog, dskip, row(ssm_norm_w[0]))
    yn, ssm_p = _ssd_call(act_p, dt, zs, None, None, None, *ssd_args, row0=0, n_seq=bp, seq_len=lp,
                          ql=SSD_CHUNK, n_fill=ts // SSD_CHUNK, name="ssd_prompt")
    tiles_per_seq = lp // XBC_ROW_TILE
    xbc_p = xbc_tail_p[tiles_per_seq - 1::tiles_per_seq, SUBLANES - (SSM_CONV_WIDTH - 1):, :]
    h0 = state_ssm[0].reshape(bs, SSM_INNER, SSM_STATE)
    yn, xbc_s, ssm_s = _ssd_call(xbc_s, dt, zs, (cache_ssm_conv[0], cw, cb), h0, yn, *ssd_args, row0=tp,
                                 n_seq=bs, seq_len=ls, ql=ls, name="ssd_sample")

    m = _mm_call(yn, [(w_ssm_out[0].astype(BF16), 0)], D_MODEL, _ep_merge, BF16, tm=tm, tn=tn // 2,
                 extras=[(m_a, 0), (gates, D_MODEL)], name="ssm_out_merge")
    x1 = _outproj_call(m, w_out[0].astype(BF16), xp, xs)
    y_p, y_s = _mlp_call(x1, row(mlp_norm_w[0]), w_up[0].astype(BF16), w_down[0].astype(BF16),
                         row(final_norm_w), n_prompt_rows=tp)

    hshape = (SSM_HEADS, SSM_HEAD_DIM, SSM_STATE)
    return (y_p.reshape(bp, lp, d), y_s.reshape(bs, ls, d),
            conv_p[None], xbc_p[None], ssm_p.reshape(1, bp, *hshape),
            conv_s[None], xbc_s[None], ssm_s.reshape(1, bs, *hshape))
```

```python
import functools

import jax
import jax.numpy as jnp
from jax import lax
from jax.experimental import pallas as pl
from jax.experimental.pallas import tpu as pltpu

F32 = jnp.float32
BF16 = jnp.bfloat16

D_MODEL = 2048
CONV_DIM = D_MODEL
CONV_WIDTH = 31
SSM_INNER = 2 * D_MODEL
SSM_HEAD_DIM = 64
SSM_HEADS = SSM_INNER // SSM_HEAD_DIM
SSM_GROUPS = 8
HEADS_PER_GROUP = SSM_HEADS // SSM_GROUPS
SSM_STATE = 128
SSM_CONV_WIDTH = 4
SSM_XBC_DIM = SSM_INNER + 2 * SSM_GROUPS * SSM_STATE
D_FF = 4 * D_MODEL
EPS = 1e-5

V7X_LANES = 128
V7X_VMEM_BYTES = 64 * 1024 * 1024
V7X_VMEM_REQUEST_CAP = 56 * 1024 * 1024
COMPILER_TEMP_BYTES = 12 * 1024 * 1024

SSD_CHUNK = 128
CONV_HALO = 32
SSM_HALO = 16
ROW_TILE = 512
MM_ROW_TILE = 1536
EPILOGUE_SUB_COLS = 256


def _vmem_limit(block_bytes, scratch_bytes=0):
    need = 2 * sum(block_bytes) + scratch_bytes + COMPILER_TEMP_BYTES
    return int(min(need, V7X_VMEM_REQUEST_CAP))


def _nbytes(shape, dtype):
    n = 1
    for s in shape:
        n *= s
    return n * jnp.dtype(dtype).itemsize


def _sigmoid(x):
    return 1.0 / (1.0 + jnp.exp(-x))


def _softplus(x):
    return jnp.maximum(x, 0.0) + jnp.log1p(jnp.exp(-jnp.abs(x)))


def _rms(x, w):
    ms = jnp.mean(x * x, axis=-1, keepdims=True)
    return x * lax.rsqrt(ms + EPS) * w


def _norm_kernel(xp_ref, xs_ref, w_ref, o_ref, *, n_prompt_tiles):
    i = pl.program_id(0)

    @pl.when(i < n_prompt_tiles)
    def _():
        o_ref[...] = _rms(xp_ref[...], w_ref[...]).astype(o_ref.dtype)

    @pl.when(i >= n_prompt_tiles)
    def _():
        o_ref[...] = _rms(xs_ref[...], w_ref[...]).astype(o_ref.dtype)


def _norm_call(xp, xs, w):
    tp, d = xp.shape
    ts = xs.shape[0]
    assert ts == ROW_TILE and tp % ROW_TILE == 0
    npt = tp // ROW_TILE
    blocks = [_nbytes((ROW_TILE, d), F32)] * 2 + [_nbytes((ROW_TILE, d), BF16)]
    return pl.pallas_call(
        functools.partial(_norm_kernel, n_prompt_tiles=npt),
        grid=(npt + 1,),
        in_specs=[
            pl.BlockSpec((ROW_TILE, d), lambda i: (jnp.minimum(i, npt - 1), 0)),
            pl.BlockSpec((ROW_TILE, d), lambda i: (0, 0)),
            pl.BlockSpec((1, d), lambda i: (0, 0)),
        ],
        out_specs=pl.BlockSpec((ROW_TILE, d), lambda i: (i, 0)),
        out_shape=jax.ShapeDtypeStruct((tp + ts, d), BF16),
        compiler_params=pltpu.CompilerParams(
            dimension_semantics=("arbitrary",), vmem_limit_bytes=_vmem_limit(blocks)),
        name="mix_rmsnorm",
    )(xp, xs, w)


WPREP_COLS = 512
WCOL_VAL = 0
WCOL_GATE = CONV_DIM
WCOL_Z = 2 * CONV_DIM
WCOL_XBC = WCOL_Z + SSM_INNER
WCOL_G = WCOL_XBC + SSM_XBC_DIM
WCOL_DT = WCOL_G + 2 * D_MODEL
WPREP_TOTAL = WCOL_DT + WPREP_COLS


def _wprep_kernel(a_ref, b_ref, o_ref, *, n_plain, n_shift):
    j = pl.program_id(0)

    @pl.when(j < n_plain)
    def _():
        o_ref[...] = a_ref[...].T.astype(BF16)

    @pl.when(jnp.logical_and(j >= n_plain, j < n_plain + n_shift))
    def _():
        src = jnp.concatenate([a_ref[SSM_HEADS:, :], b_ref[:SSM_HEADS, :]], axis=0)
        o_ref[...] = src.T.astype(BF16)

    @pl.when(j == n_plain + n_shift)
    def _():
        r = lax.broadcasted_iota(jnp.int32, a_ref.shape, 0)
        o_ref[...] = jnp.where(r < SSM_HEADS, a_ref[...], 0.0).T.astype(BF16)


def _wprep_call(w_in_t):
    n_src, k = w_in_t.shape
    assert n_src == WCOL_G + SSM_HEADS + 2 * D_MODEL
    n_plain = WCOL_G // WPREP_COLS
    n_shift = 2 * D_MODEL // WPREP_COLS
    last_src = (n_src - 1) // WPREP_COLS
    blocks = [_nbytes((WPREP_COLS, k), F32)] * 3 + [_nbytes((k, WPREP_COLS), BF16)]
    return pl.pallas_call(
        functools.partial(_wprep_kernel, n_plain=n_plain, n_shift=n_shift),
        grid=(n_plain + n_shift + 1,),
        in_specs=[
            pl.BlockSpec((WPREP_COLS, k), lambda j: (jnp.where(j == n_plain + n_shift, n_plain, j), 0)),
            pl.BlockSpec((WPREP_COLS, k),
                         lambda j: (jnp.where(j < n_plain, n_plain, jnp.minimum(j + 1, last_src)), 0)),
        ],
        out_specs=pl.BlockSpec((k, WPREP_COLS), lambda j: (0, j)),
        out_shape=jax.ShapeDtypeStruct((k, WPREP_TOTAL), BF16),
        compiler_params=pltpu.CompilerParams(
            dimension_semantics=("arbitrary",), vmem_limit_bytes=_vmem_limit(blocks)),
        name="w_in_prep",
    )(w_in_t, w_in_t)


def _mm_kernel(*refs, n_w, n_ex, n_p, epilogue, sub_cols):
    lhs_ref = refs[0]
    w_refs = refs[1:1 + n_w]
    ex_refs = refs[1 + n_w:1 + n_w + n_ex]
    p_refs = refs[1 + n_w + n_ex:1 + n_w + n_ex + n_p]
    o_ref = refs[-1]
    lhs = lhs_ref[...]
    for c0 in range(0, o_ref.shape[1], sub_cols):
        sub = slice(c0, c0 + sub_cols)
        accs = [jnp.dot(lhs, w[:, sub], preferred_element_type=F32) for w in w_refs]
        out = epilogue(accs, [e[:, sub].astype(F32) for e in ex_refs], [p[:, sub] for p in p_refs])
        o_ref[:, sub] = out.astype(o_ref.dtype)


def _mm_call(lhs, weights, n, epilogue, out_dtype, *, tm, tn, extras=(), params=(), name, sub_cols=None):
    sub_cols = tn if sub_cols is None else sub_cols
    assert tn % sub_cols == 0
    t, k = lhs.shape
    assert t % tm == 0 and n % tn == 0
    in_specs = [pl.BlockSpec((tm, k), lambda i, j: (i, 0))]
    operands = [lhs]
    blocks = [_nbytes((tm, k), lhs.dtype)]
    for w, col0 in weights:
        assert col0 % tn == 0
        in_specs.append(pl.BlockSpec((k, tn), lambda i, j, o=col0 // tn: (0, j + o)))
        operands.append(w)
        blocks.append(_nbytes((k, tn), w.dtype))
    for arr, col0 in extras:
        assert col0 % tn == 0
        in_specs.append(pl.BlockSpec((tm, tn), lambda i, j, o=col0 // tn: (i, j + o)))
        operands.append(arr)
        blocks.append(_nbytes((tm, tn), arr.dtype))
    for p in params:
        in_specs.append(pl.BlockSpec((1, tn), lambda i, j: (0, j)))
        operands.append(p)
        blocks.append(_nbytes((8, tn), p.dtype))
    blocks.append(_nbytes((tm, tn), out_dtype))
    return pl.pallas_call(
        functools.partial(_mm_kernel, n_w=len(weights), n_ex=len(extras), n_p=len(params),
                          epilogue=epilogue, sub_cols=sub_cols),
        grid=(t // tm, n // tn),
        in_specs=in_specs,
        out_specs=pl.BlockSpec((tm, tn), lambda i, j: (i, j)),
        out_shape=jax.ShapeDtypeStruct((t, n), out_dtype),
        compiler_params=pltpu.CompilerParams(
            dimension_semantics=("arbitrary", "arbitrary"), vmem_limit_bytes=_vmem_limit(blocks)),
        name=name,
    )(*operands)


def _ep_glu(accs, extras, params):
    return accs[0] * _sigmoid(accs[1])


def _ep_identity(accs, extras, params):
    return accs[0]


def _ep_sigmoid(accs, extras, params):
    return _sigmoid(accs[0])


def _ep_silu(accs, extras, params):
    return accs[0] * _sigmoid(accs[0])


def _ep_softplus_bias(accs, extras, params):
    return _softplus(accs[0] + params[0])


def _ep_merge(accs, extras, params):
    return extras[0] + extras[1] * accs[0]


XBC_ROW_TILE = 1024
XBC_COL_TILE = 2048
XBC_SUB_COLS = 256
SUBLANES = 8


def _xbc_conv_kernel(h_ref, w_ref, cw_ref, cb_ref, act_ref, tail_ref, carry_ref, stage_ref, *, tiles_per_seq):
    i = pl.program_id(0)
    j = pl.program_id(1)
    tm = h_ref.shape[0]
    hist = SSM_CONV_WIDTH - 1

    @pl.when((i % tiles_per_seq) == 0)
    def _():
        tile_cols = pl.ds(pl.multiple_of(j * XBC_COL_TILE, XBC_COL_TILE), XBC_COL_TILE)
        carry_ref[:, tile_cols] = jnp.zeros((SUBLANES, XBC_COL_TILE), F32)

    lhs = h_ref[...]
    for n in range(XBC_COL_TILE // XBC_SUB_COLS):
        sub = slice(n * XBC_SUB_COLS, (n + 1) * XBC_SUB_COLS)
        gcols = pl.ds(pl.multiple_of(j * XBC_COL_TILE + n * XBC_SUB_COLS, XBC_SUB_COLS), XBC_SUB_COLS)
        st = stage_ref.at[n % 2]
        st[SUBLANES:SUBLANES + tm, :] = jnp.dot(lhs, w_ref[:, sub], preferred_element_type=F32)
        st[0:SUBLANES, :] = carry_ref[:, gcols]
        last = st[tm:tm + SUBLANES, :]
        carry_ref[:, gcols] = last
        tail_ref[0, :, sub] = last
        conv = cb_ref[:, sub] + cw_ref[hist:hist + 1, sub] * st[SUBLANES:SUBLANES + tm, :]
        for k in range(hist):
            conv = conv + cw_ref[k:k + 1, sub] * st[SUBLANES - hist + k:SUBLANES - hist + k + tm, :]
        act_ref[:, sub] = (conv * _sigmoid(conv)).astype(act_ref.dtype)


def _xbc_conv_call(h, w, w_col0, cw, cb, *, n_seq, seq_len):
    k = h.shape[1]
    n = cw.shape[1]
    tm, tn = XBC_ROW_TILE, XBC_COL_TILE
    assert seq_len % tm == 0 and n % tn == 0 and w_col0 % tn == 0
    w_blk0 = w_col0 // tn
    tiles_per_seq = seq_len // tm
    stage_shape = (2, tm + SUBLANES, XBC_SUB_COLS)
    blocks = [_nbytes((tm, k), BF16), _nbytes((k, tn), BF16), _nbytes((8, tn), F32) * 3,
              _nbytes((tm, tn), BF16)]
    return pl.pallas_call(
        functools.partial(_xbc_conv_kernel, tiles_per_seq=tiles_per_seq),
        grid=(n_seq * tiles_per_seq, n // tn),
        in_specs=[
            pl.BlockSpec((tm, k), lambda i, j: (i, 0)),
            pl.BlockSpec((k, tn), lambda i, j: (0, j + w_blk0)),
            pl.BlockSpec((SSM_CONV_WIDTH, tn), lambda i, j: (0, j)),
            pl.BlockSpec((1, tn), lambda i, j: (0, j)),
        ],
        out_specs=[
            pl.BlockSpec((tm, tn), lambda i, j: (i, j)),
            pl.BlockSpec((1, SUBLANES, tn), lambda i, j: (i, 0, j)),
        ],
        out_shape=[
            jax.ShapeDtypeStruct((n_seq * seq_len, n), BF16),
            jax.ShapeDtypeStruct((n_seq * tiles_per_seq, SUBLANES, n), F32),
        ],
        scratch_shapes=[pltpu.VMEM((SUBLANES, n), F32), pltpu.VMEM(stage_shape, F32)],
        compiler_params=pltpu.CompilerParams(
            dimension_semantics=("arbitrary", "arbitrary"),
            vmem_limit_bytes=_vmem_limit(blocks, _nbytes((SUBLANES, n), F32) + _nbytes(stage_shape, F32))),
        name="in_proj_xbc_conv",
    )(h, w, cw, cb)


CONV_ROW_BLOCK = 64
LN_ROW_BLOCK = 256


def _conv31_block(src, zs, dww_ref, dwb_ref, cols, lanes, tm, store):
    hist = CONV_WIDTH - 1
    base = CONV_HALO - hist
    rb = min(CONV_ROW_BLOCK, tm)
    for s in range(SUBLANES):
        rows = tm + SUBLANES * ((hist - s) // SUBLANES)
        zs[s, 0:rows, :] = src[base + s:base + s + rows, lanes]
    bias = jnp.broadcast_to(dwb_ref[:, cols], (rb, V7X_LANES))
    for r0 in range(0, tm, rb):
        acc = bias
        for k in range(CONV_WIDTH):
            s, q = k % SUBLANES, k // SUBLANES
            acc = acc + dww_ref[k:k + 1, cols] * zs[s, r0 + SUBLANES * q:r0 + SUBLANES * q + rb, :]
        store(r0, rb, acc)


def _conf_kernel(*refs, tm, n_tiles, n_fill, conv_in_kernel, aliased):
    if n_fill == 0:
        _conf_body(refs, tm, conv_in_kernel, aliased)
        return
    i = pl.program_id(0)
    o_ref = refs[5 + 3 * int(conv_in_kernel) + int(aliased)]

    @pl.when(i < n_tiles)
    def _():
        _conf_body(refs, tm, conv_in_kernel, aliased)

    @pl.when(i >= n_tiles)
    def _():
        o_ref[...] = jnp.zeros_like(o_ref)


def _conf_body(refs, tm, conv_in_kernel, aliased):
    it = iter(refs)
    src_ref = next(it)
    if conv_in_kernel:
        hist_ref = next(it)
        dww_ref = next(it)
        dwb_ref = next(it)
    g_ref = next(it)
    lnw_ref = next(it)
    lnb_ref = next(it)
    w_ref = next(it)
    if aliased:
        next(it)
    o_ref = next(it)
    if conv_in_kernel:
        cache_ref = next(it)
        xs_ref = next(it)
        zs_ref = next(it)
        y_ref = next(it)
    s_ref = next(it)
    hist = CONV_WIDTH - 1

    if conv_in_kernel:
        xs_ref[0:CONV_HALO - hist, :] = jnp.zeros((CONV_HALO - hist, CONV_DIM), F32)
        xs_ref[CONV_HALO - hist:CONV_HALO, :] = hist_ref[0]
        xs_ref[CONV_HALO:CONV_HALO + tm, :] = src_ref[...].astype(F32)
        cache_ref[0] = xs_ref[CONV_HALO + tm - hist:CONV_HALO + tm, :]

        def conv_cols(cb, carry):
            cols = pl.ds(pl.multiple_of(cb * V7X_LANES, V7X_LANES), V7X_LANES)

            def store(r0, rb, acc):
                y_ref[r0:r0 + rb, cols] = acc

            _conv31_block(xs_ref, zs_ref, dww_ref, dwb_ref, cols, cols, tm, store)
            return carry

        lax.fori_loop(0, CONV_DIM // V7X_LANES, conv_cols, 0)

    ln_rows_per_step = min(LN_ROW_BLOCK, tm)

    def ln_rows(rb, carry):
        r0 = pl.multiple_of(rb * ln_rows_per_step, ln_rows_per_step)
        if conv_in_kernel:
            y = y_ref[pl.ds(r0, ln_rows_per_step), :]
        else:
            y = src_ref[pl.ds(r0, ln_rows_per_step), :].astype(F32)
        mu = jnp.mean(y, axis=-1, keepdims=True)
        yc = y - mu
        var = jnp.mean(yc * yc, axis=-1, keepdims=True)
        yn = yc * lax.rsqrt(var + EPS) * lnw_ref[...] + lnb_ref[...]
        s_ref[pl.ds(r0, ln_rows_per_step), :] = (yn * _sigmoid(yn)).astype(BF16)
        return carry

    lax.fori_loop(0, tm // ln_rows_per_step, ln_rows, 0)

    ya = jnp.dot(s_ref[...], w_ref[...], preferred_element_type=F32)
    o_ref[...] = (g_ref[...].astype(F32) * ya).astype(o_ref.dtype)


def _conf_call(src, gates, conv, prev, lnw, lnb, w, *, row0, n_rows, tm, name, n_fill=0):
    assert n_rows % tm == 0 and row0 % tm == 0 and src.shape[0] == n_rows
    n_tiles = n_rows // tm
    conv_in_kernel = conv is not None
    assert not (conv_in_kernel and n_fill)
    hist = CONV_WIDTH - 1
    blk0 = row0 // tm

    def tile(i):
        return jnp.minimum(i, n_tiles - 1)

    in_specs = [pl.BlockSpec((tm, CONV_DIM), lambda i: (tile(i), 0))]
    operands = [src]
    blocks = [_nbytes((tm, CONV_DIM), BF16), _nbytes((tm, D_MODEL), BF16) * 2, _nbytes((8, CONV_DIM), F32) * 2,
              _nbytes((CONV_DIM, D_MODEL), BF16)]
    scratch_dims = [((tm, CONV_DIM), BF16)]
    out_specs = [pl.BlockSpec((tm, D_MODEL), lambda i: (blk0 + i, 0))]
    out_shape = [jax.ShapeDtypeStruct((gates.shape[0], D_MODEL), BF16)]
    if conv_in_kernel:
        assert tm % CONV_HALO == 0
        in_specs += [
            pl.BlockSpec((1, hist, CONV_DIM), lambda i: (i, 0, 0)),
            pl.BlockSpec((CONV_WIDTH, CONV_DIM), lambda i: (0, 0)),
            pl.BlockSpec((1, CONV_DIM), lambda i: (0, 0)),
        ]
        operands += list(conv)
        blocks += [_nbytes((32, CONV_DIM), F32) * 3, _nbytes((8, CONV_DIM), F32)]
        scratch_dims = [((CONV_HALO + tm, CONV_DIM), F32), ((SUBLANES, CONV_HALO + tm, V7X_LANES), F32),
                        ((tm, CONV_DIM), F32)] + scratch_dims
        out_specs.append(pl.BlockSpec((1, hist, CONV_DIM), lambda i: (i, 0, 0)))
        out_shape.append(jax.ShapeDtypeStruct((n_tiles, hist, CONV_DIM), F32))
    in_specs += [
        pl.BlockSpec((tm, D_MODEL), lambda i: (blk0 + tile(i), 0)),
        pl.BlockSpec((1, CONV_DIM), lambda i: (0, 0)),
        pl.BlockSpec((1, CONV_DIM), lambda i: (0, 0)),
        pl.BlockSpec((CONV_DIM, D_MODEL), lambda i: (0, 0)),
    ]
    operands += [gates, lnw, lnb, w]
    aliases = {}
    if prev is not None:
        in_specs.append(pl.BlockSpec(memory_space=pl.ANY))
        operands.append(prev)
        aliases = {len(operands) - 1: 0}
    return pl.pallas_call(
        functools.partial(_conf_kernel, tm=tm, n_tiles=n_tiles, n_fill=n_fill, conv_in_kernel=conv_in_kernel,
                          aliased=prev is not None),
        grid=(n_tiles + n_fill,),
        in_specs=in_specs,
        out_specs=out_specs,
        out_shape=out_shape,
        scratch_shapes=[pltpu.VMEM(shape, dt) for shape, dt in scratch_dims],
        input_output_aliases=aliases,
        compiler_params=pltpu.CompilerParams(
            dimension_semantics=("arbitrary",),
            vmem_limit_bytes=_vmem_limit(blocks, sum(_nbytes(sh, dt) for sh, dt in scratch_dims))),
        name=name,
    )(*operands)


GC_ROW_TILE = 512
GC_COL_TILE = 2048
GC_SUB_COLS = 256


def _glu_conv_kernel(h_ref, wv_ref, wg_ref, dww_ref, dwb_ref, y_ref, tail_ref, carry_ref, xs_ref, zs_ref,
                     *, tiles_per_seq):
    i = pl.program_id(0)
    j = pl.program_id(1)
    tm = h_ref.shape[0]

    @pl.when((i % tiles_per_seq) == 0)
    def _():
        tile_cols = pl.ds(pl.multiple_of(j * GC_COL_TILE, GC_COL_TILE), GC_COL_TILE)
        carry_ref[:, tile_cols] = jnp.zeros((CONV_HALO, GC_COL_TILE), F32)

    lhs = h_ref[...]
    for n in range(GC_COL_TILE // GC_SUB_COLS):
        sub = slice(n * GC_SUB_COLS, (n + 1) * GC_SUB_COLS)
        gcols = pl.ds(pl.multiple_of(j * GC_COL_TILE + n * GC_SUB_COLS, GC_SUB_COLS), GC_SUB_COLS)
        xs = xs_ref.at[n % 2]
        val = jnp.dot(lhs, wv_ref[:, sub], preferred_element_type=F32)
        gate = jnp.dot(lhs, wg_ref[:, sub], preferred_element_type=F32)
        xs[CONV_HALO:CONV_HALO + tm, :] = val * _sigmoid(gate)
        xs[0:CONV_HALO, :] = carry_ref[:, gcols]
        last = xs[tm:tm + CONV_HALO, :]
        carry_ref[:, gcols] = last
        tail_ref[0, :, sub] = last
        for half in range(GC_SUB_COLS // V7X_LANES):
            lanes = slice(half * V7X_LANES, (half + 1) * V7X_LANES)
            cols = slice(n * GC_SUB_COLS + half * V7X_LANES, n * GC_SUB_COLS + (half + 1) * V7X_LANES)

            def store(r0, rb, acc, cols=cols):
                y_ref[r0:r0 + rb, cols] = acc.astype(y_ref.dtype)

            _conv31_block(xs, zs_ref.at[half], dww_ref, dwb_ref, cols, lanes, tm, store)


def _glu_conv_call(h, w, wv_col0, wg_col0, dww, dwb, *, n_seq, seq_len):
    k = h.shape[1]
    n = dww.shape[1]
    tm, tn = GC_ROW_TILE, GC_COL_TILE
    assert seq_len % tm == 0 and n % tn == 0 and wv_col0 % tn == 0 and wg_col0 % tn == 0
    wv_blk0, wg_blk0 = wv_col0 // tn, wg_col0 // tn
    tiles_per_seq = seq_len // tm
    blocks = [_nbytes((tm, k), BF16), _nbytes((k, tn), BF16) * 2, _nbytes((32, tn), F32) * 2,
              _nbytes((tm, tn), BF16), _nbytes((CONV_HALO, tn), F32)]
    scratch_dims = [((CONV_HALO, n), F32), ((2, CONV_HALO + tm, GC_SUB_COLS), F32),
                    ((GC_SUB_COLS // V7X_LANES, SUBLANES, CONV_HALO + tm, V7X_LANES), F32)]
    return pl.pallas_call(
        functools.partial(_glu_conv_kernel, tiles_per_seq=tiles_per_seq),
        grid=(n_seq * tiles_per_seq, n // tn),
        in_specs=[
            pl.BlockSpec((tm, k), lambda i, j: (i, 0)),
            pl.BlockSpec((k, tn), lambda i, j: (0, j + wv_blk0)),
            pl.BlockSpec((k, tn), lambda i, j: (0, j + wg_blk0)),
            pl.BlockSpec((CONV_WIDTH, tn), lambda i, j: (0, j)),
            pl.BlockSpec((1, tn), lambda i, j: (0, j)),
        ],
        out_specs=[
            pl.BlockSpec((tm, tn), lambda i, j: (i, j)),
            pl.BlockSpec((1, CONV_HALO, tn), lambda i, j: (i, 0, j)),
        ],
        out_shape=[
            jax.ShapeDtypeStruct((n_seq * seq_len, n), BF16),
            jax.ShapeDtypeStruct((n_seq * tiles_per_seq, CONV_HALO, n), F32),
        ],
        scratch_shapes=[pltpu.VMEM(shape, dt) for shape, dt in scratch_dims],
        compiler_params=pltpu.CompilerParams(
            dimension_semantics=("arbitrary", "arbitrary"),
            vmem_limit_bytes=_vmem_limit(blocks, sum(_nbytes(sh, dt) for sh, dt in scratch_dims))),
        name="in_proj_glu_conv",
    )(h, w, w, dww, dwb)


SSM_CONV_COL_BLOCK = 512
GROUP_COLS = HEADS_PER_GROUP * SSM_HEAD_DIM
B_COL0 = SSM_INNER
C_COL0 = SSM_INNER + SSM_GROUPS * SSM_STATE
NEG_BIG = -1e30
LOG2_E = 1.4426950408889634


def _split3(x):
    p1 = x.astype(BF16)
    r1 = x - p1.astype(F32)
    p2 = r1.astype(BF16)
    r2 = r1 - p2.astype(F32)
    return p1, p2, r2.astype(BF16)


def _ssd_kernel(*refs, ql, n_chunks, fill_per_seq, has_h0, conv_in_kernel, aliased):
    if fill_per_seq == 0:
        _ssd_body(refs, ql, n_chunks, has_h0, conv_in_kernel, aliased)
        return
    c_idx = pl.program_id(1)
    y_ref = refs[6 + 3 * int(conv_in_kernel) + int(n_chunks > 1) + int(has_h0) + int(aliased)]

    @pl.when(c_idx < n_chunks)
    def _():
        _ssd_body(refs, ql, n_chunks, has_h0, conv_in_kernel, aliased)

    @pl.when(c_idx >= n_chunks)
    def _():
        y_ref[...] = jnp.zeros_like(y_ref)


def _ssd_body(refs, ql, n_chunks, has_h0, conv_in_kernel, aliased):
    it = iter(refs)
    src_ref = next(it)
    if conv_in_kernel:
        hist_ref = next(it)
        cw_ref = next(it)
        cb_ref = next(it)
    dt_ref = next(it)
    dtn_ref = next(it) if n_chunks > 1 else None
    z_ref = next(it)
    alog_ref = next(it)
    dskip_ref = next(it)
    nw_ref = next(it)
    h0_ref = next(it) if has_h0 else None
    if aliased:
        next(it)
    y_ref = next(it)
    cache_ref = next(it) if conv_in_kernel else None
    hout_ref = next(it)
    xs_ref = next(it) if conv_in_kernel else None
    act_ref = next(it)
    acs_ref = next(it)
    rowt_ref = next(it)
    wendt_ref = next(it)
    ht_ref = next(it)

    q = SSD_CHUNK
    c_idx = pl.program_id(1)
    hist = SSM_CONV_WIDTH - 1

    @pl.when(c_idx == 0)
    def _():
        for g in range(SSM_GROUPS):
            if has_h0:
                ht_ref[g] = h0_ref[0, g * GROUP_COLS:(g + 1) * GROUP_COLS, :].T
            else:
                ht_ref[g] = jnp.zeros((SSM_STATE, GROUP_COLS), F32)

    if conv_in_kernel:
        assert n_chunks == 1 and ql % SUBLANES == 0
        xs_ref[0:SSM_HALO - hist, :] = jnp.zeros((SSM_HALO - hist, SSM_XBC_DIM), F32)
        xs_ref[SSM_HALO - hist:SSM_HALO, :] = hist_ref[0]
        xs_ref[SSM_HALO:SSM_HALO + ql, :] = src_ref[...].astype(F32)
        if ql < q:
            act_ref[ql:q, :] = jnp.zeros((q - ql, SSM_XBC_DIM), F32)
        base = SSM_HALO - hist

        def conv_cols(cb, carry):
            c0 = pl.multiple_of(cb * SSM_CONV_COL_BLOCK, SSM_CONV_COL_BLOCK)
            cols = pl.ds(c0, SSM_CONV_COL_BLOCK)
            acc = jnp.broadcast_to(cb_ref[:, cols], (ql, SSM_CONV_COL_BLOCK))
            for k in range(SSM_CONV_WIDTH):
                acc = acc + cw_ref[k:k + 1, cols] * xs_ref[base + k:base + k + ql, cols]
            act_ref[0:ql, cols] = acc * _sigmoid(acc)
            return carry

        lax.fori_loop(0, SSM_XBC_DIM // SSM_CONV_COL_BLOCK, conv_cols, 0)
    else:
        assert ql == q
        act_ref[...] = src_ref[...].astype(F32)

    r_i = lax.broadcasted_iota(jnp.int32, (q, q), 0)
    c_i = lax.broadcasted_iota(jnp.int32, (q, q), 1)
    causal = r_i >= c_i

    def compute_decay(d_ref, slot):
        dt = d_ref[...]
        if ql < q:
            dt = jnp.concatenate([dt, jnp.zeros((q - ql, V7X_LANES), F32)], axis=0)
        dta = dt * (-jnp.exp(alog_ref[...]))
        tri = jnp.where(causal, 1.0, 0.0).astype(BF16)
        p1, p2, p3 = _split3(dta)
        acs = (jnp.dot(tri, p1, preferred_element_type=F32) + jnp.dot(tri, p2, preferred_element_type=F32)
               + jnp.dot(tri, p3, preferred_element_type=F32))
        acs2 = acs * LOG2_E
        acs_ref[slot] = acs2
        acs2_t = acs2.T
        rowt = acs2_t - jnp.log2(dt.T)
        rowt_ref[slot] = rowt
        wendt_ref[slot] = jnp.exp2(acs2_t[:, q - 1:q] - rowt)

    if dtn_ref is None:
        slot = 0
        compute_decay(dt_ref, 0)
    else:
        slot = c_idx % 2

        @pl.when(c_idx == 0)
        def _():
            compute_decay(dt_ref, 0)

    left = c_i < SSM_HEAD_DIM
    left_row = left[0:1, :]
    c_l = lax.broadcasted_iota(jnp.int32, (ql, q), 1)
    causal_l = lax.broadcasted_iota(jnp.int32, (ql, q), 0) >= c_l
    left_l = c_l < SSM_HEAD_DIM

    for g in range(SSM_GROUPS):
        bg = act_ref[:, B_COL0 + g * SSM_STATE:B_COL0 + (g + 1) * SSM_STATE]
        cg = act_ref[0:ql, C_COL0 + g * SSM_STATE:C_COL0 + (g + 1) * SSM_STATE].astype(BF16)
        cbm = lax.dot_general(cg, bg.astype(BF16), (((1,), (1,)), ((), ())),
                              preferred_element_type=F32)
        bg_t = bg.T
        ht_g = ht_ref[g]
        yoff = jnp.dot(cg, ht_g.astype(BF16), preferred_element_type=F32)
        y_parts, inc_parts, dec_parts = [], [], []
        for j in range(HEADS_PER_GROUP // 2):
            h_a = g * HEADS_PER_GROUP + 2 * j
            h_b = h_a + 1
            lo = h_a * SSM_HEAD_DIM
            xp = act_ref[:, lo:lo + V7X_LANES]
            lhs_parts = []
            cols = []
            for h in (h_a, h_b):
                col = jnp.broadcast_to(acs_ref[slot, 0:ql, h:h + 1], (ql, q))
                cols.append(col)
                seg = jnp.where(causal_l, col - rowt_ref[slot, h:h + 1, :], NEG_BIG)
                m_h = cbm * jnp.exp2(seg)
                w_h = bg_t * wendt_ref[slot, h:h + 1, :]
                lhs_parts.append(jnp.concatenate([m_h, w_h], axis=0).astype(BF16))
            rhs = jnp.concatenate([jnp.where(left, xp, 0.0).astype(BF16),
                                   jnp.where(left, 0.0, xp).astype(BF16)], axis=0)
            res = jnp.dot(jnp.concatenate(lhs_parts, axis=1), rhs, preferred_element_type=F32)
            colsel = jnp.where(left_l, cols[0], cols[1])
            y_parts.append(res[0:ql] + jnp.exp2(colsel) * yoff[:, j * V7X_LANES:(j + 1) * V7X_LANES]
                           + dskip_ref[:, lo:lo + V7X_LANES] * xp[0:ql])
            inc_parts.append(res[ql:ql + q])
            last_a = jnp.broadcast_to(acs_ref[slot, q - 1:q, h_a:h_a + 1], (1, V7X_LANES))
            last_b = jnp.broadcast_to(acs_ref[slot, q - 1:q, h_b:h_b + 1], (1, V7X_LANES))
            dec_parts.append(jnp.exp2(jnp.where(left_row, last_a, last_b)))
        ht_ref[g] = jnp.concatenate(dec_parts, axis=1) * ht_g + jnp.concatenate(inc_parts, axis=1)
        gc = slice(g * GROUP_COLS, (g + 1) * GROUP_COLS)
        yg = jnp.concatenate(y_parts, axis=1)
        yz = yg * z_ref[:, gc].astype(F32)
        ms = jnp.mean(yz * yz, axis=-1, keepdims=True)
        y_ref[:, gc] = (yz * lax.rsqrt(ms + EPS) * nw_ref[:, gc]).astype(y_ref.dtype)

    if dtn_ref is not None:
        compute_decay(dtn_ref, 1 - slot)

    @pl.when(c_idx == n_chunks - 1)
    def _():
        if conv_in_kernel:
            cache_ref[0] = xs_ref[SSM_HALO + ql - hist:SSM_HALO + ql, :]
        for g in range(SSM_GROUPS):
            hout_ref[0, g * GROUP_COLS:(g + 1) * GROUP_COLS, :] = ht_ref[g].T


def _ssd_call(src, dt, z, conv, h0, prev, alog, dskip, nw, *, row0, n_seq, seq_len, ql, name, n_fill=0):
    assert seq_len % ql == 0 and row0 % ql == 0 and ql % SSM_HALO == 0 and ql <= SSD_CHUNK
    assert src.shape[0] == n_seq * seq_len
    n_chunks = seq_len // ql
    conv_in_kernel = conv is not None
    has_h0 = h0 is not None
    hist = SSM_CONV_WIDTH - 1
    blk0 = row0 // ql
    q = SSD_CHUNK

    assert n_fill % n_seq == 0
    fill_per_seq = n_fill // n_seq

    def chunk(c):
        return jnp.minimum(c, n_chunks - 1)

    def row_blk(b, c):
        return blk0 + b * n_chunks + chunk(c)

    def out_blk(b, c):
        fill_blk = blk0 + n_seq * n_chunks + b * fill_per_seq + (c - n_chunks)
        return jnp.where(c < n_chunks, row_blk(b, c), fill_blk)

    in_specs = [pl.BlockSpec((ql, SSM_XBC_DIM), lambda b, c: (b * n_chunks + chunk(c), 0))]
    operands = [src]
    if conv_in_kernel:
        in_specs += [
            pl.BlockSpec((1, hist, SSM_XBC_DIM), lambda b, c: (b, 0, 0)),
            pl.BlockSpec((SSM_CONV_WIDTH, SSM_XBC_DIM), lambda b, c: (0, 0)),
            pl.BlockSpec((1, SSM_XBC_DIM), lambda b, c: (0, 0)),
        ]
        operands += list(conv)
    in_specs.append(pl.BlockSpec((ql, V7X_LANES), lambda b, c: (row_blk(b, c), 0)))
    operands.append(dt)
    if n_chunks > 1:
        in_specs.append(pl.BlockSpec((ql, V7X_LANES), lambda b, c: (row_blk(b, c + 1), 0)))
        operands.append(dt)
    in_specs += [
        pl.BlockSpec((ql, SSM_INNER), lambda b, c: (row_blk(b, c), 0)),
        pl.BlockSpec((1, V7X_LANES), lambda b, c: (0, 0)),
        pl.BlockSpec((1, SSM_INNER), lambda b, c: (0, 0)),
        pl.BlockSpec((1, SSM_INNER), lambda b, c: (0, 0)),
    ]
    operands += [z, alog, dskip, nw]
    if has_h0:
        in_specs.append(pl.BlockSpec((1, SSM_INNER, SSM_STATE), lambda b, c: (b, 0, 0)))
        operands.append(h0)
    aliases = {}
    if prev is not None:
        in_specs.append(pl.BlockSpec(memory_space=pl.ANY))
        operands.append(prev)
        aliases = {len(operands) - 1: 0}
    state_bytes = _nbytes((SSM_INNER, SSM_STATE), F32)
    blocks = [_nbytes((ql, SSM_XBC_DIM), BF16), _nbytes((SSM_HALO, SSM_XBC_DIM), F32),
              _nbytes((ql, V7X_LANES), F32), _nbytes((ql, SSM_INNER), BF16),
              _nbytes((8, SSM_XBC_DIM), F32) * 2, _nbytes((8, SSM_INNER), F32) * 2,
              _nbytes((ql, SSM_INNER), BF16), _nbytes((8, SSM_XBC_DIM), F32),
              state_bytes * (2 if has_h0 else 1)]
    scratch_dims = ([((SSM_HALO + q, SSM_XBC_DIM), F32)] if conv_in_kernel else []) + [
        ((q, SSM_XBC_DIM), F32),
        ((2, q, V7X_LANES), F32),
        ((2, V7X_LANES, q), F32),
        ((2, V7X_LANES, q), F32),
        ((SSM_GROUPS, SSM_STATE, GROUP_COLS), F32),
    ]
    out_specs = [pl.BlockSpec((ql, SSM_INNER), lambda b, c: (out_blk(b, c), 0))]
    out_shape = [jax.ShapeDtypeStruct((dt.shape[0], SSM_INNER), BF16)]
    if conv_in_kernel:
        out_specs.append(pl.BlockSpec((1, hist, SSM_XBC_DIM), lambda b, c: (b, 0, 0)))
        out_shape.append(jax.ShapeDtypeStruct((n_seq, hist, SSM_XBC_DIM), F32))
    out_specs.append(pl.BlockSpec((1, SSM_INNER, SSM_STATE), lambda b, c: (b, 0, 0)))
    out_shape.append(jax.ShapeDtypeStruct((n_seq, SSM_INNER, SSM_STATE), F32))
    return pl.pallas_call(
        functools.partial(_ssd_kernel, ql=ql, n_chunks=n_chunks, fill_per_seq=fill_per_seq, has_h0=has_h0,
                          conv_in_kernel=conv_in_kernel, aliased=prev is not None),
        grid=(n_seq, n_chunks + fill_per_seq),
        in_specs=in_specs,
        out_specs=out_specs,
        out_shape=out_shape,
        scratch_shapes=[pltpu.VMEM(shape, dt_) for shape, dt_ in scratch_dims],
        input_output_aliases=aliases,
        compiler_params=pltpu.CompilerParams(
            dimension_semantics=("arbitrary", "arbitrary"),
            vmem_limit_bytes=_vmem_limit(blocks, sum(_nbytes(sh, dt_) for sh, dt_ in scratch_dims))),
        name=name,
    )(*operands)


def _outproj_kernel(m_ref, w_ref, xp_ref, xs_ref, o_ref, *, n_prompt_tiles):
    i = pl.program_id(0)
    acc = jnp.dot(m_ref[...], w_ref[...], preferred_element_type=F32)

    @pl.when(i < n_prompt_tiles)
    def _():
        o_ref[...] = xp_ref[...] + acc

    @pl.when(i >= n_prompt_tiles)
    def _():
        o_ref[...] = xs_ref[...] + acc


def _outproj_call(m, w, xp, xs):
    t, d = m.shape
    npt = xp.shape[0] // ROW_TILE
    blocks = [_nbytes((ROW_TILE, d), BF16), _nbytes((d, d), BF16)] + [_nbytes((ROW_TILE, d), F32)] * 3
    return pl.pallas_call(
        functools.partial(_outproj_kernel, n_prompt_tiles=npt),
        grid=(t // ROW_TILE,),
        in_specs=[
            pl.BlockSpec((ROW_TILE, d), lambda i: (i, 0)),
            pl.BlockSpec((d, d), lambda i: (0, 0)),
            pl.BlockSpec((ROW_TILE, d), lambda i: (jnp.minimum(i, npt - 1), 0)),
            pl.BlockSpec((ROW_TILE, d), lambda i: (0, 0)),
        ],
        out_specs=pl.BlockSpec((ROW_TILE, d), lambda i: (i, 0)),
        out_shape=jax.ShapeDtypeStruct((t, d), F32),
        compiler_params=pltpu.CompilerParams(
            dimension_semantics=("arbitrary",), vmem_limit_bytes=_vmem_limit(blocks)),
        name="out_proj_residual",
    )(m, w, xp, xs)


MLP_FF_TILE = 1024


def _mlp_kernel(x1_ref, nw_ref, wup_ref, wdn_ref, fw_ref, yp_ref, ys_ref, h2_ref, acc_ref,
                *, n_prompt_tiles, n_ff):
    i = pl.program_id(0)
    f = pl.program_id(1)

    @pl.when(f == 0)
    def _():
        h2_ref[...] = _rms(x1_ref[...], nw_ref[...]).astype(BF16)
        acc_ref[...] = jnp.zeros_like(acc_ref)

    hid = jnp.dot(h2_ref[...], wup_ref[...], preferred_element_type=F32)
    hid = jnp.square(jnp.maximum(hid, 0.0))
    acc_ref[...] += jnp.dot(hid.astype(BF16), wdn_ref[...], preferred_element_type=F32)

    @pl.when(f == n_ff - 1)
    def _():
        out = _rms(x1_ref[...] + acc_ref[...], fw_ref[...])

        @pl.when(i < n_prompt_tiles)
        def _():
            yp_ref[...] = out

        @pl.when(i >= n_prompt_tiles)
        def _():
            ys_ref[...] = out


def _mlp_call(x1, nw, wup, wdn, fw, *, n_prompt_rows):
    t, d = x1.shape
    npt = n_prompt_rows // ROW_TILE
    n_ff = D_FF // MLP_FF_TILE
    blocks = [_nbytes((ROW_TILE, d), F32), _nbytes((d, MLP_FF_TILE), BF16),
              _nbytes((MLP_FF_TILE, d), BF16), _nbytes((ROW_TILE, d), F32) * 2]
    scratch_bytes = _nbytes((ROW_TILE, d), BF16) + _nbytes((ROW_TILE, d), F32)
    return pl.pallas_call(
        functools.partial(_mlp_kernel, n_prompt_tiles=npt, n_ff=n_ff),
        grid=(t // ROW_TILE, n_ff),
        in_specs=[
            pl.BlockSpec((ROW_TILE, d), lambda i, f: (i, 0)),
            pl.BlockSpec((1, d), lambda i, f: (0, 0)),
            pl.BlockSpec((d, MLP_FF_TILE), lambda i, f: (0, f)),
            pl.BlockSpec((MLP_FF_TILE, d), lambda i, f: (f, 0)),
            pl.BlockSpec((1, d), lambda i, f: (0, 0)),
        ],
        out_specs=[
            pl.BlockSpec((ROW_TILE, d), lambda i, f: (jnp.minimum(i, npt - 1), 0)),
            pl.BlockSpec((ROW_TILE, d), lambda i, f: (0, 0)),
        ],
        out_shape=[
            jax.ShapeDtypeStruct((n_prompt_rows, d), F32),
            jax.ShapeDtypeStruct((t - n_prompt_rows, d), F32),
        ],
        scratch_shapes=[pltpu.VMEM((ROW_TILE, d), BF16), pltpu.VMEM((ROW_TILE, d), F32)],
        compiler_params=pltpu.CompilerParams(
            dimension_semantics=("arbitrary", "arbitrary"),
            vmem_limit_bytes=_vmem_limit(blocks, scratch_bytes)),
        name="mlp_final_norm",
    )(x1, nw, wup, wdn, fw)


def kernel(x_prompt, x_sample, cache_conv, cache_ssm_conv, state_ssm, mix_norm_w, w_in, conv_dw_w,
           conv_dw_b, conv_ln_w, conv_ln_b, w_conv_out, ssm_conv_w, ssm_conv_b, dt_bias, a_log, d_skip,
           ssm_norm_w, w_ssm_out, w_out, mlp_norm_w, w_up, w_down, final_norm_w):
    bp, lp, d = x_prompt.shape
    bs, ls, _ = x_sample.shape
    tp, ts = bp * lp, bs * ls
    xp = x_prompt.reshape(tp, d)
    xs = x_sample.reshape(ts, d)

    wi = _wprep_call(w_in[0].T)
    dt_b =jnp.pad(dt_bias[0], (0, V7X_LANES - SSM_HEADS)).reshape(1, V7X_LANES)
    alog = jnp.pad(a_log[0], (0, V7X_LANES - SSM_HEADS)).reshape(1, V7X_LANES)
    dskip = jnp.repeat(d_skip[0], SSM_HEAD_DIM).reshape(1, SSM_INNER)
    row = lambda v: v.reshape(1, -1)

    h = _norm_call(xp, xs, row(mix_norm_w[0]))

    tm, tn = MM_ROW_TILE, 1024
    dww, dwb = conv_dw_w[0], row(conv_dw_b[0])
    yconv_p, a_tail_p = _glu_conv_call(h, wi, WCOL_VAL, WCOL_GATE, dww, dwb, n_seq=bp, seq_len=lp)
    h_s = h[tp:]
    a_s = _mm_call(h_s, [(wi, WCOL_VAL), (wi, WCOL_GATE)], CONV_DIM, _ep_glu, BF16, tm=ts, tn=tn // 2,
                   name="in_proj_glu_sample")
    zs = _mm_call(h, [(wi, WCOL_Z)], SSM_INNER, _ep_silu, BF16, tm=tm, tn=2 * tn, sub_cols=EPILOGUE_SUB_COLS,
                  name="in_proj_silu_z")
    cw, cb = ssm_conv_w[0], row(ssm_conv_b[0])
    act_p, xbc_tail_p = _xbc_conv_call(h, wi, WCOL_XBC, cw, cb, n_seq=bp, seq_len=lp)
    xbc_s = _mm_call(h_s, [(wi, WCOL_XBC)], SSM_XBC_DIM, _ep_identity, BF16, tm=ts, tn=tn,
                     name="in_proj_xbc_sample")
    dt = _mm_call(h, [(wi, WCOL_DT)], V7X_LANES, _ep_softplus_bias, F32, tm=tm, tn=V7X_LANES, params=[dt_b],
                  name="in_proj_dt")
    gates = _mm_call(h, [(wi, WCOL_G)], 2 * D_MODEL, _ep_sigmoid, BF16, tm=tm, tn=2 * tn,
                     sub_cols=EPILOGUE_SUB_COLS, name="in_proj_gates")

    conf_args = (row(conv_ln_w[0]), row(conv_ln_b[0]), w_conv_out[0].astype(BF16))
    (m_a,) = _conf_call(yconv_p, gates, None, None, *conf_args, row0=0, n_rows=tp, tm=ROW_TILE,
                        n_fill=ts // ROW_TILE, name="conformer_prompt")
    gc_tiles = lp // GC_ROW_TILE
    conv_p = a_tail_p[gc_tiles - 1::gc_tiles, CONV_HALO - (CONV_WIDTH - 1):, :]
    m_a, conv_s = _conf_call(a_s, gates, (cache_conv[0], dww, dwb), m_a, *conf_args, row0=tp, n_rows=ts,
                             tm=ls, name="conformer_sample")

    ssd_args = (alog, dskip, row(ssm_norm_w[0]))
    yn, ssm_p = _ssd_call(act_p, dt, zs, None, None, None, *ssd_args, row0=0, n_seq=bp, seq_len=lp,
                          ql=SSD_CHUNK, n_fill=ts // SSD_CHUNK, name="ssd_prompt")
    tiles_per_seq = lp // XBC_ROW_TILE
    xbc_p = xbc_tail_p[tiles_per_seq - 1::tiles_per_seq, SUBLANES - (SSM_CONV_WIDTH - 1):, :]
    h0 = state_ssm[0].reshape(bs, SSM_INNER, SSM_STATE)
    yn, xbc_s, ssm_s = _ssd_call(xbc_s, dt, zs, (cache_ssm_conv[0], cw, cb), h0, yn, *ssd_args, row0=tp,
                                 n_seq=bs, seq_len=ls, ql=ls, name="ssd_sample")

    m = _mm_call(yn, [(w_ssm_out[0].astype(BF16), 0)], D_MODEL, _ep_merge, BF16, tm=tm, tn=tn // 2,
                 extras=[(m_a, 0), (gates, D_MODEL)], name="ssm_out_merge")
    x1 = _outproj_call(m, w_out[0].astype(BF16), xp, xs)
    y_p, y_s = _mlp_call(x1, row(mlp_norm_w[0]), w_up[0].astype(BF16), w_down[0].astype(BF16),
                         row(final_norm_w), n_prompt_rows=tp)

    hshape = (SSM_HEADS, SSM_HEAD_DIM, SSM_STATE)
    return (y_p.reshape(bp, lp, d), y_s.reshape(bs, ls, d),
            conv_p[None], xbc_p[None], ssm_p.reshape(1, bp, *hshape),
            conv_s[None], xbc_s[None], ssm_s.reshape(1, bs, *hshape))
```
